```python
import jax
import jax.numpy as jnp
from jax import lax
import numpy as np

D_MODEL = 2048
BATCH = 8
SEQ = 2048
DEPTH = 1

HG_HEADS = 8
HG_DK = 128
HG_DV = 128
HG_FDIM = HG_HEADS * HG_DK
HG_VDIM = HG_HEADS * HG_DV

GLA_HEADS = 4
GLA_DK = 128
GLA_DV = 256
GLA_KDIM = GLA_HEADS * GLA_DK
GLA_VDIM = GLA_HEADS * GLA_DV
GLA_GATE_RANK = 16
GLA_GATE_NORM = 16.0

CHUNK = 64

N_EXPERTS = 32
TOP_K = 4
D_FF = D_MODEL
SWIGLU_LIMIT = 7.0
SWIGLU_ALPHA = 1.702
MOE_BLOCK = 256

RMS_EPS = 1e-5

IN_SPLITS = (HG_FDIM, HG_FDIM, HG_FDIM, HG_VDIM, HG_VDIM,
             GLA_KDIM, GLA_KDIM, GLA_VDIM, GLA_VDIM, GLA_GATE_RANK, GLA_GATE_RANK,
             D_MODEL, D_MODEL)
D_IN = 3 * HG_FDIM + 2 * HG_VDIM + 2 * GLA_KDIM + 2 * GLA_VDIM + 2 * GLA_GATE_RANK + 2 * D_MODEL

kernel_name = 'hybrid_hgrn2_gla_moe_encoder'


def rms_norm(x, g):
    xf = x.astype(jnp.float32)
    y = xf * lax.rsqrt(jnp.mean(xf * xf, axis=-1, keepdims=True) + RMS_EPS)
    return (y * g.astype(jnp.float32)).astype(x.dtype)


def split_points():
    pts, acc = [], 0
    for w in IN_SPLITS[:-1]:
        acc += w
        pts.append(acc)
    return pts


def heads(a, n):
    return a.reshape(a.shape[0], a.shape[1], n, -1)


def chunk_gated_recurrence(q, k, v, log_g):
    B, T, H, dk = q.shape
    dv = v.shape[-1]
    n_chunks = T // CHUNK

    def to_chunks(a):
        return a.astype(jnp.float32).reshape(B, n_chunks, CHUNK, H, a.shape[-1]).transpose(1, 0, 3, 2, 4)

    qc, kc, vc = to_chunks(q), to_chunks(k), to_chunks(v)
    bc = jnp.cumsum(to_chunks(log_g), axis=-2)
    causal = jnp.tril(jnp.ones((CHUNK, CHUNK), dtype=bool))[:, :, None]

    def step(state, inp):
        qi, ki, vi, bi = inp
        b_last = bi[..., -1:, :]
        diff = bi[..., :, None, :] - bi[..., None, :, :]
        decay = jnp.exp(jnp.where(causal, diff, -jnp.inf))
        scores = jnp.einsum('bhtk,bhsk,bhtsk->bhts', qi, ki, decay)
        o = (jnp.einsum('bhts,bhsv->bhtv', scores, vi)
             + jnp.einsum('bhtk,bhkv->bhtv', qi * jnp.exp(bi), state))
        state = (state * jnp.exp(b_last)[..., 0, :, None]
                 + jnp.einsum('bhsk,bhsv->bhkv', ki * jnp.exp(b_last - bi), vi))
        return state, o

    state0 = jnp.zeros((B, H, dk, dv), jnp.float32)
    _, o = lax.scan(step, state0, (qc, kc, vc, bc))
    return o.transpose(1, 0, 3, 2, 4).reshape(B, T, H, dv).astype(v.dtype)


def bidirectional_recurrence(q, v, k_fwd, g_fwd, k_bwd, g_bwd):
    flip = lambda a: jnp.flip(a, axis=1)
    fwd = chunk_gated_recurrence(q, k_fwd, v, g_fwd)
    bwd = flip(chunk_gated_recurrence(flip(q), flip(k_bwd), flip(v), flip(g_bwd)))
    return fwd + bwd


def hgrn_lower_bound(lb_logits, layer):
    p = jax.nn.softmax(lb_logits.astype(jnp.float32), axis=1)
    return jnp.cumsum(p, axis=1)[:, layer]


def mixer_block(xn, w_in, lb, hg_norm_g, gla_gate_w2, gla_gate_b, gla_norm_g, w_proj_a, w_proj_b, w_out):
    B, T, _ = xn.shape
    z = xn @ w_in
    (hq, hf_fwd, hf_bwd, hi, hog, gq, gk, gv, gr,
     glr_fwd, glr_bwd, mg_a, mg_b) = jnp.split(z, split_points(), axis=-1)

    q_a = heads(jax.nn.silu(hq), HG_HEADS) * (HG_DK ** -0.5)
    v_a = heads(hi, HG_HEADS)

    def hg_gates(f_pre, lb_d):
        f = lb_d + (1.0 - lb_d) * jax.nn.sigmoid(f_pre.astype(jnp.float32))
        return heads(1.0 - f, HG_HEADS), heads(jnp.log(f), HG_HEADS)

    k_af, g_af = hg_gates(hf_fwd, lb[0])
    k_ab, g_ab = hg_gates(hf_bwd, lb[1])
    o_a = bidirectional_recurrence(q_a, v_a, k_af, g_af, k_ab, g_ab).astype(xn.dtype)
    o_a = rms_norm(o_a, hg_norm_g) * jax.nn.silu(heads(hog, HG_HEADS))
    y_a = o_a.reshape(B, T, HG_VDIM) @ w_proj_a

    q_b = heads(gq, GLA_HEADS) * (GLA_DK ** -0.5)
    k_b = heads(gk, GLA_HEADS)
    v_b = heads(gv, GLA_HEADS)

    def gla_log_gate(lr, d):
        logits = (lr @ gla_gate_w2[d] + gla_gate_b[d]).astype(jnp.float32)
        return heads(jax.nn.log_sigmoid(logits) / GLA_GATE_NORM, GLA_HEADS)

    o_b = bidirectional_recurrence(q_b, v_b, k_b, gla_log_gate(glr_fwd, 0),
                                   k_b, gla_log_gate(glr_bwd, 1)).astype(xn.dtype)
    o_b = rms_norm(o_b, gla_norm_g) * jax.nn.silu(heads(gr, GLA_HEADS))
    y_b = o_b.reshape(B, T, GLA_VDIM) @ w_proj_b

    merged = jax.nn.sigmoid(mg_a) * y_a + jax.nn.sigmoid(mg_b) * y_b
    return merged @ w_out


def moe_block(hn, router_w, router_b, w_gate_up, b_gate_up, w_down, b_down):
    B, T, D = hn.shape
    n_tok = B * T
    n_pairs = n_tok * TOP_K
    xt = hn.reshape(n_tok, D)
    logits = (xt @ router_w).astype(jnp.float32) + router_b.astype(jnp.float32)
    top_val, top_idx = lax.top_k(logits, TOP_K)
    probs = jax.nn.softmax(top_val, axis=-1)

    pair_e = top_idx.reshape(-1).astype(jnp.int32)
    pair_tok = jnp.repeat(jnp.arange(n_tok, dtype=jnp.int32), TOP_K)
    pair_w = probs.reshape(-1)
    order = jnp.argsort(pair_e)
    se, st, sw = pair_e[order], pair_tok[order], pair_w[order]

    counts = jnp.bincount(pair_e, length=N_EXPERTS)
    padded = (counts + MOE_BLOCK - 1) // MOE_BLOCK * MOE_BLOCK
    pad_end = jnp.cumsum(padded)
    pad_start = pad_end - padded
    grp_start = jnp.cumsum(counts) - counts
    dest = pad_start[se] + jnp.arange(n_pairs, dtype=jnp.int32) - grp_start[se]
    n_blocks = -(-n_pairs // MOE_BLOCK) + N_EXPERTS
    n_rows = n_blocks * MOE_BLOCK
    row_tok = jnp.zeros((n_rows,), jnp.int32).at[dest].set(st)
    row_w = jnp.zeros((n_rows,), jnp.float32).at[dest].set(sw)
    block_start = jnp.arange(n_blocks, dtype=jnp.int32) * MOE_BLOCK
    block_e = jnp.minimum(jnp.searchsorted(pad_end, block_start, side='right'), N_EXPERTS - 1)

    def run_block(args):
        e, tok = args
        xb = xt[tok]
        gu = xb @ w_gate_up[e] + b_gate_up[e]
        gate = jnp.minimum(gu[:, ::2], SWIGLU_LIMIT)
        up = jnp.clip(gu[:, 1::2], -SWIGLU_LIMIT, SWIGLU_LIMIT)
        act = (up + 1.0) * (gate * jax.nn.sigmoid(SWIGLU_ALPHA * gate))
        return act @ w_down[e] + b_down[e]

    outs = lax.map(run_block, (block_e, row_tok.reshape(n_blocks, MOE_BLOCK)))
    outs = outs.reshape(n_rows, D) * row_w[:, None].astype(outs.dtype)
    y = jax.ops.segment_sum(outs, row_tok, num_segments=n_tok)
    return y.reshape(B, T, D)


def setup_inputs(seed: int = 0) -> dict:
    key = jax.random.key(seed)
    ks = jax.random.split(key, 20)
    f32 = jnp.float32
    L = DEPTH

    def nrm(k, shape, scale):
        return jax.random.normal(k, shape, f32) * scale

    return {
        'x': nrm(ks[0], (BATCH, SEQ, D_MODEL), 1.0),
        'norm1_g': 1.0 + nrm(ks[1], (L, D_MODEL), 0.02),
        'w_in': nrm(ks[2], (L, D_MODEL, D_IN), D_MODEL ** -0.5),
        'hg_lb_logits': nrm(ks[3], (2, L + 1, HG_FDIM), 0.5),
        'hg_norm_g': 1.0 + nrm(ks[4], (L, HG_DV), 0.02),
        'gla_gate_w2': nrm(ks[5], (L, 2, GLA_GATE_RANK, GLA_KDIM), GLA_GATE_RANK ** -0.5),
        'gla_gate_b': nrm(ks[6], (L, 2, GLA_KDIM), 0.1),
        'gla_norm_g': 1.0 + nrm(ks[7], (L, GLA_DV), 0.02),
        'w_proj_a': nrm(ks[8], (L, HG_VDIM, D_MODEL), HG_VDIM ** -0.5),
        'w_proj_b': nrm(ks[9], (L, GLA_VDIM, D_MODEL), GLA_VDIM ** -0.5),
        'w_out': nrm(ks[10], (L, D_MODEL, D_MODEL), D_MODEL ** -0.5),
        'norm2_g': 1.0 + nrm(ks[11], (L, D_MODEL), 0.02),
        'router_w': nrm(ks[12], (L, D_MODEL, N_EXPERTS), D_MODEL ** -0.5),
        'router_b': nrm(ks[13], (L, N_EXPERTS), 0.01),
        'w_gate_up': nrm(ks[14], (L, N_EXPERTS, D_MODEL, 2 * D_FF), D_MODEL ** -0.5),
        'b_gate_up': nrm(ks[15], (L, N_EXPERTS, 2 * D_FF), 0.02),
        'w_down': nrm(ks[16], (L, N_EXPERTS, D_FF, D_MODEL), D_FF ** -0.5),
        'b_down': nrm(ks[17], (L, N_EXPERTS, D_MODEL), 0.02),
        'final_norm_g': 1.0 + nrm(ks[18], (D_MODEL,), 0.02),
    }


def reference(x, norm1_g, w_in, hg_lb_logits, hg_norm_g, gla_gate_w2, gla_gate_b, gla_norm_g,
              w_proj_a, w_proj_b, w_out, norm2_g, router_w, router_b, w_gate_up, b_gate_up,
              w_down, b_down, final_norm_g):
    h = x
    for layer in range(DEPTH):
        lb = hgrn_lower_bound(hg_lb_logits, layer)
        h = h + mixer_block(rms_norm(h, norm1_g[layer]), w_in[layer], lb, hg_norm_g[layer],
                            gla_gate_w2[layer], gla_gate_b[layer], gla_norm_g[layer],
                            w_proj_a[layer], w_proj_b[layer], w_out[layer])
        h = h + moe_block(rms_norm(h, norm2_g[layer]), router_w[layer], router_b[layer],
                          w_gate_up[layer], b_gate_up[layer], w_down[layer], b_down[layer])
    return rms_norm(h, final_norm_g)
```

```python
import functools

import jax
import jax.numpy as jnp
from jax import lax
from jax.experimental import pallas as pl
from jax.experimental.pallas import tpu as pltpu

F32 = jnp.float32
BF16 = jnp.bfloat16
U32 = jnp.uint32
I32 = jnp.int32

LANES = 128
SUBLANES = 8
VMEM_LIMIT_BYTES = 56 * 1024 * 1024

HEAD_DK = 128
GLA_DV = 256
HG_DV = 128
GATE_NORM = 16.0
TOP_K = 4
SWIGLU_LIMIT = 7.0
SWIGLU_ALPHA = 1.702
RMS_EPS = 1e-5
CHUNK = 64
NEG_BIG = -1e30


def _divisor_tile(n, pref):
    t = min(n, pref)
    while n % t:
        t //= 2
    return t


def _params(sem, vmem=VMEM_LIMIT_BYTES):
    return pltpu.CompilerParams(dimension_semantics=sem, vmem_limit_bytes=vmem)


def _pack_bf16_pairs(a):
    n = a.shape[-1] // 2
    hi = lax.bitcast_convert_type(a[:, :n].astype(BF16).astype(F32), U32)
    lo = lax.bitcast_convert_type(a[:, n:].astype(BF16).astype(F32), U32)
    return hi | (lo >> 16)


def _unpack_bf16_pairs(p):
    hi = lax.bitcast_convert_type(p & jnp.uint32(0xFFFF0000), F32)
    lo = lax.bitcast_convert_type(p << 16, F32)
    return hi, lo


def _rms(x, g):
    ms = jnp.mean(x * x, axis=-1, keepdims=True)
    return x * lax.rsqrt(ms + RMS_EPS) * g


def _log_sigmoid(x):
    return jnp.minimum(x, 0.0) - jnp.log1p(jnp.exp(-jnp.abs(x)))


def _in_proj_kernel(bounds, x_ref, g1_ref, w_ref, lb_ref, w2_ref, b2_ref,
                    qa_ref, kf_ref, gf_ref, kb_ref, gb_ref, va_ref, oga_ref,
                    qb_ref, kg_ref, vb_ref, ogb_ref, sga_ref, sgb_ref, gg_ref,
                    xn_ref):
    j = pl.program_id(1)

    @pl.when(j == 0)
    def _():
        xn_ref[...] = _rms(x_ref[...], g1_ref[...]).astype(BF16)

    z = jnp.dot(xn_ref[...], w_ref[...], preferred_element_type=F32)
    scale = HEAD_DK ** -0.5

    def seg(k):
        return (j >= bounds[k]) & (j < bounds[k + 1])

    @pl.when(seg(0))
    def _():
        qa_ref[...] = (z * jax.nn.sigmoid(z) * scale).astype(BF16)

    def forget(k_ref, g_ref):
        lb = lb_ref[0:1, :]
        f = lb + (1.0 - lb) * jax.nn.sigmoid(z)
        k_ref[...] = (1.0 - f).astype(BF16)
        g_ref[...] = jnp.log(f)

    @pl.when(seg(1))
    def _():
        forget(kf_ref, gf_ref)

    @pl.when(seg(2))
    def _():
        forget(kb_ref, gb_ref)

    @pl.when(seg(3))
    def _():
        va_ref[...] = z.astype(BF16)

    @pl.when(seg(4))
    def _():
        oga_ref[...] = (z * jax.nn.sigmoid(z)).astype(BF16)

    @pl.when(seg(5))
    def _():
        qb_ref[...] = (z * scale).astype(BF16)

    @pl.when(seg(6))
    def _():
        kg_ref[...] = z.astype(BF16)

    @pl.when(seg(7))
    def _():
        vb_ref[...] = z.astype(BF16)

    @pl.when(seg(8))
    def _():
        ogb_ref[...] = (z * jax.nn.sigmoid(z)).astype(BF16)

    @pl.when(seg(9))
    def _():
        sga_ref[...] = jax.nn.sigmoid(z).astype(BF16)

    @pl.when(seg(10))
    def _():
        sgb_ref[...] = jax.nn.sigmoid(z).astype(BF16)

    @pl.when(seg(11))
    def _():
        lr = z[:, :LANES].astype(BF16)
        logits = jnp.dot(lr, w2_ref[...], preferred_element_type=F32) + b2_ref[...]
        gg_ref[...] = _log_sigmoid(logits) * (1.0 / GATE_NORM)


def _in_proj(x2, norm1_g, w_in, lb, gate_w2, gate_b, dims):
    n_tok, d = x2.shape
    hf, hv, gk, gv, rank = dims
    widths = [hf, hf, hf, hv, hv, gk, gk, gv, gv]
    off_lr = sum(widths)
    tn = _divisor_tile(gk, 512)
    tm = _divisor_tile(n_tok, 512)
    widths_all = widths + [d, d, tn]
    w_lr = jnp.pad(w_in[:, off_lr:off_lr + 2 * rank], ((0, 0), (0, tn - 2 * rank)))
    w_cat = jnp.concatenate([w_in[:, :off_lr], w_in[:, off_lr + 2 * rank:], w_lr], axis=1).astype(BF16)
    n_col = w_cat.shape[1]
    lb_tab = jnp.zeros((SUBLANES, n_col), F32)
    lb_tab = lb_tab.at[0, hf:2 * hf].set(lb[0]).at[0, 2 * hf:3 * hf].set(lb[1])
    w2 = jnp.zeros((LANES, 2 * gk), F32)
    w2 = w2.at[:rank, :gk].set(gate_w2[0]).at[rank:2 * rank, gk:].set(gate_w2[1]).astype(BF16)
    b2 = jnp.concatenate([gate_b[0], gate_b[1]])[None, :].astype(F32)

    bounds = [0]
    for w in widths_all:
        bounds.append(bounds[-1] + w // tn)
    n_j = bounds[-1]

    def out_spec(k):
        s, n = bounds[k], bounds[k + 1] - bounds[k]
        return pl.BlockSpec((tm, tn), lambda i, j, s=s, n=n: (i, jnp.clip(j - s, 0, n - 1)))

    seg_of_out = [0, 1, 1, 2, 2, 3, 4, 5, 6, 7, 8, 9, 10]
    out_dtypes = [BF16, BF16, F32, BF16, F32, BF16, BF16, BF16, BF16, BF16, BF16, BF16, BF16]
    out_shape = [jax.ShapeDtypeStruct((n_tok, widths_all[s]), dt) for s, dt in zip(seg_of_out, out_dtypes)]
    out_specs = [out_spec(s) for s in seg_of_out]
    out_shape.append(jax.ShapeDtypeStruct((n_tok, 2 * gk), F32))
    out_specs.append(pl.BlockSpec((tm, 2 * gk), lambda i, j: (i, 0)))

    return pl.pallas_call(
        functools.partial(_in_proj_kernel, tuple(bounds)),
        grid=(n_tok // tm, n_j),
        in_specs=[
            pl.BlockSpec((tm, d), lambda i, j: (i, 0)),
            pl.BlockSpec((1, d), lambda i, j: (0, 0)),
            pl.BlockSpec((d, tn), lambda i, j: (0, j)),
            pl.BlockSpec((SUBLANES, tn), lambda i, j: (0, j)),
            pl.BlockSpec((LANES, 2 * gk), lambda i, j: (0, 0)),
            pl.BlockSpec((1, 2 * gk), lambda i, j: (0, 0)),
        ],
        out_specs=out_specs,
        out_shape=out_shape,
        scratch_shapes=[pltpu.VMEM((tm, d), BF16)],
        compiler_params=_params(("arbitrary", "arbitrary")),
        name="in_proj",
    )(x2, norm1_g.reshape(1, d), w_cat, lb_tab, w2, b2)


def _chunk_step(q, k, v, g, st_ref, rev):
    nt = CHUNK // SUBLANES
    row = lax.broadcasted_iota(I32, (SUBLANES, LANES), 0)
    lrow = (SUBLANES - 1 - row) if rev else row

    def phys(jl):
        return nt - 1 - jl if rev else jl

    def prow(x, rl):
        r = SUBLANES - 1 - rl if rev else rl
        return x[r:r + 1, :]

    def tile(x, jl):
        p = phys(jl)
        return x[SUBLANES * p:SUBLANES * (p + 1), :]

    def scan(x):
        for s in (1, 2, 4):
            if rev:
                x = x + jnp.where(row < SUBLANES - s, pltpu.roll(x, SUBLANES - s, 0), 0.0)
            else:
                x = x + jnp.where(row >= s, pltpu.roll(x, s, 0), 0.0)
        return x

    gt = [tile(g, jl) for jl in range(nt)]
    qt = [tile(q, jl) for jl in range(nt)]
    kt = [tile(k, jl) for jl in range(nt)]
    ct = [scan(x) for x in gt]
    tot = [prow(c, SUBLANES - 1) for c in ct]
    suf = [t - c for t, c in zip(tot, ct)]

    def assemble(tiles):
        out = [None] * nt
        for jl, x in enumerate(tiles):
            out[phys(jl)] = jnp.zeros((SUBLANES, LANES), F32) if x is None else x
        return jnp.concatenate(out, axis=0).astype(BF16)

    def plus(a, b):
        return a if b is None else a + b

    levels = []

    for nb in (8, 4, 2):
        half = nb // 2
        qs, ks = [None] * nt, [None] * nt
        for base in range(0, nt, nb):
            acc = None
            for jl in range(base + half, base + nb):
                qs[jl] = qt[jl] * jnp.exp(plus(ct[jl], acc))
                acc = plus(tot[jl], acc)
            acc = None
            for jl in range(base + half - 1, base - 1, -1):
                ks[jl] = kt[jl] * jnp.exp(plus(suf[jl], acc))
                acc = plus(tot[jl], acc)
        levels.append((SUBLANES * nb, assemble(qs), assemble(ks)))

    for size in (8, 4):
        half = size // 2
        is_q = (lrow & (size - 1)) >= half
        qs, ks = [], []
        for jl in range(nt):
            if size == 8:
                ref = prow(ct[jl], half - 1)
            else:
                ref = jnp.where(lrow >= size, prow(ct[jl], size + half - 1), prow(ct[jl], half - 1))
            d = ct[jl] - ref
            e = jnp.exp(jnp.where(is_q, d, -d))
            qs.append(jnp.where(is_q, qt[jl] * e, 0.0))
            ks.append(jnp.where(is_q, 0.0, kt[jl] * e))
        levels.append((size, assemble(qs), assemble(ks)))

    odd = (lrow & 1) == 1
    qs = [jnp.where(odd, qt[jl] * jnp.exp(gt[jl]), 0.0) for jl in range(nt)]
    ks = [jnp.where(odd, 0.0, kt[jl]) for jl in range(nt)]
    levels.append((2, assemble(qs), assemble(ks)))
    levels.append((1, q.astype(BF16), k.astype(BF16)))

    ti = lax.broadcasted_iota(I32, (CHUNK, CHUNK), 0)
    si = lax.broadcasted_iota(I32, (CHUNK, CHUNK), 1)
    blk = ti ^ si
    nt_dims = (((1,), (1,)), ((), ()))
    scores = None
    for size, qm, km in levels:
        s_m = lax.dot_general(qm, km, nt_dims, preferred_element_type=F32)
        if size < CHUNK:
            s_m = jnp.where(blk < size, s_m, 0.0)
        scores = s_m if scores is None else scores + s_m
    out = jnp.dot(scores.astype(BF16), v, preferred_element_type=F32)

    q_in, k_out = [None] * nt, [None] * nt
    acc = None
    for jl in range(nt):
        q_in[jl] = qt[jl] * jnp.exp(plus(ct[jl], acc))
        acc = plus(tot[jl], acc)
    total = acc
    acc = None
    for jl in range(nt - 1, -1, -1):
        k_out[jl] = kt[jl] * jnp.exp(plus(suf[jl], acc))
        acc = plus(tot[jl], acc)
    st = st_ref[...]
    out = out + lax.dot_general(assemble(q_in), st.astype(BF16), nt_dims, preferred_element_type=F32)
    tn_dims = (((0,), (0,)), ((), ()))
    st_ref[...] = st * jnp.exp(total) + lax.dot_general(v, assemble(k_out), tn_dims,
                                                        preferred_element_type=F32)
    return out


def _recurrence_kernel(n_chunks, q_ref, kf_ref, kb_ref, v_ref, gf_ref, gb_ref, og_ref, ng_ref,
                       o_ref, acc_ref, stf_ref, stb_ref):
    acc_ref[...] = jnp.zeros_like(acc_ref)
    stf_ref[...] = jnp.zeros_like(stf_ref)
    stb_ref[...] = jnp.zeros_like(stb_ref)

    def body(c, carry):
        for rev, k_ref, g_ref, st_ref in ((False, kf_ref, gf_ref, stf_ref), (True, kb_ref, gb_ref, stb_ref)):
            cc = (n_chunks - 1 - c) if rev else c
            rows = pl.ds(pl.multiple_of(cc * CHUNK, CHUNK), CHUNK)
            o = _chunk_step(q_ref[rows, :].astype(F32), k_ref[rows, :].astype(F32), v_ref[rows, :],
                            g_ref[rows, :], st_ref, rev)
            acc_ref[rows, :] += o
        return carry

    lax.fori_loop(0, n_chunks, body, 0)
    o_ref[...] = (_rms(acc_ref[...], ng_ref[...]) * og_ref[...].astype(F32)).astype(BF16)


def _recurrence(q, kf, kb, v, gf, gb, og, norm_g, batch, seq, heads, dv, kb_off=0, gb_off=0, name=""):
    n_tok = batch * seq
    n_chunks = seq // CHUNK

    def spec(width, off=0):
        return pl.BlockSpec((seq, width), lambda b, h, off=off: (b, h + off))

    return pl.pallas_call(
        functools.partial(_recurrence_kernel, n_chunks),
        grid=(batch, heads),
        in_specs=[spec(HEAD_DK), spec(HEAD_DK), spec(HEAD_DK, kb_off), spec(dv),
                  spec(HEAD_DK), spec(HEAD_DK, gb_off), spec(dv),
                  pl.BlockSpec((1, dv), lambda b, h: (0, 0))],
        out_specs=spec(dv),
        out_shape=jax.ShapeDtypeStruct((n_tok, heads * dv), BF16),
        scratch_shapes=[pltpu.VMEM((seq, dv), F32), pltpu.VMEM((dv, HEAD_DK), F32),
                        pltpu.VMEM((dv, HEAD_DK), F32)],
        compiler_params=_params(("arbitrary", "arbitrary")),
        name=name,
    )(q, kf, kb, v, gf, gb, og, norm_g.reshape(1, dv).astype(F32))


def _mix_out_kernel(oa_ref, ob_ref, sga_ref, sgb_ref, x_ref, wpa_ref, wpb_ref, wo_ref, g2_ref, rw_ref, rb_ref,
                    h_ref, hn_ref, w4_ref, idx_ref, sel_ref):
    ya = jnp.dot(oa_ref[...], wpa_ref[...], preferred_element_type=F32)
    yb = jnp.dot(ob_ref[...], wpb_ref[...], preferred_element_type=F32)
    merged = sga_ref[...].astype(F32) * ya + sgb_ref[...].astype(F32) * yb
    h = x_ref[...] + jnp.dot(merged.astype(BF16), wo_ref[...], preferred_element_type=F32)
    h_ref[...] = h
    hn = _rms(h, g2_ref[...])
    hn_ref[...] = _pack_bf16_pairs(hn)

    logits = jnp.dot(hn, rw_ref[...], preferred_element_type=F32,
                     precision=lax.Precision.HIGHEST) + rb_ref[...]
    lane = lax.broadcasted_iota(I32, logits.shape, 1)
    lane_f = lane.astype(F32)
    work = logits
    vals, idxs = [], []
    for _ in range(TOP_K):
        m = jnp.max(work, axis=-1, keepdims=True)
        idx = jnp.min(jnp.where(work == m, lane_f, float(LANES)), axis=-1, keepdims=True)
        vals.append(m)
        idxs.append(idx)
        work = jnp.where(lane_f == idx, -jnp.inf, work)
    es = [jnp.exp(v - vals[0]) for v in vals]
    denom = es[0]
    for e in es[1:]:
        denom = denom + e
    w4 = jnp.zeros(logits.shape, F32)
    i4 = jnp.zeros(logits.shape, F32)
    sel = jnp.zeros(logits.shape, F32)
    for k in range(TOP_K):
        w4 = jnp.where(lane == k, es[k] / denom, w4)
        i4 = jnp.where(lane == k, idxs[k], i4)
        sel = jnp.where(lane_f == idxs[k], 1.0, sel)
    w4_ref[...] = w4
    idx_ref[...] = i4.astype(I32)
    sel_ref[...] = sel.astype(BF16)


def _mix_out(oa, ob, sga, sgb, x2, w_proj_a, w_proj_b, w_out, norm2_g, router_w, router_b):
    n_tok, d = x2.shape
    hv, gv = oa.shape[1], ob.shape[1]
    n_exp = router_w.shape[1]
    tm = _divisor_tile(n_tok, 256)
    rw = jnp.pad(router_w.astype(F32), ((0, 0), (0, LANES - n_exp)))
    rb = jnp.concatenate([router_b.astype(F32), jnp.full((LANES - n_exp,), NEG_BIG, F32)])[None, :]

    def rows(width):
        return pl.BlockSpec((tm, width), lambda i: (i, 0))

    def whole(r, c):
        return pl.BlockSpec((r, c), lambda i: (0, 0))

    return pl.pallas_call(
        _mix_out_kernel,
        grid=(n_tok // tm,),
        in_specs=[rows(hv), rows(gv), rows(d), rows(d), rows(d),
                  whole(hv, d), whole(gv, d), whole(d, d), whole(1, d), whole(d, LANES), whole(1, LANES)],
        out_specs=[rows(d), rows(d // 2), rows(LANES), rows(LANES), rows(LANES)],
        out_shape=[jax.ShapeDtypeStruct((n_tok, d), F32),
                   jax.ShapeDtypeStruct((n_tok, d // 2), U32),
                   jax.ShapeDtypeStruct((n_tok, LANES), F32),
                   jax.ShapeDtypeStruct((n_tok, LANES), I32),
                   jax.ShapeDtypeStruct((n_tok, LANES), BF16)],
        compiler_params=_params(("arbitrary",)),
        name="mix_out",
    )(oa, ob, sga, sgb, x2, w_proj_a.astype(BF16), w_proj_b.astype(BF16), w_out.astype(BF16),
      norm2_g.reshape(1, d).astype(F32), rw, rb)


def _route_rank_kernel(row_block, sel_ref, idx_ref, tri_ref, dest_ref, cnt_ref, rank_ref, carry_ref, start_ref):
    p = pl.program_id(0)
    i = pl.program_id(1)
    tb = sel_ref.shape[0]
    rows = pl.ds(pl.multiple_of(i * tb, tb), tb)

    @pl.when((p == 0) & (i == 0))
    def _():
        carry_ref[...] = jnp.zeros_like(carry_ref)

    @pl.when(p == 0)
    def _():
        sel = sel_ref[...]
        before = jnp.dot(tri_ref[...], sel, preferred_element_type=F32)
        rank_ref[rows, :] = before + carry_ref[0:1, :]
        carry_ref[...] = carry_ref[...] + jnp.sum(sel.astype(F32), axis=0, keepdims=True)

    @pl.when((p == 1) & (i == 0))
    def _():
        counts = carry_ref[...]
        padded = jnp.ceil(counts * (1.0 / row_block)) * row_block
        lane = lax.broadcasted_iota(I32, padded.shape, 1)
        run = padded
        s = 1
        while s < LANES:
            run = run + jnp.where(lane >= s, pltpu.roll(run, s, 1), 0.0)
            s *= 2
        start_ref[...] = run - padded

    @pl.when(p == 1)
    def _():
        pos = rank_ref[rows, :] + start_ref[0:1, :]
        idx = idx_ref[...]
        lane = lax.broadcasted_iota(I32, pos.shape, 1)
        dest = jnp.zeros(pos.shape, F32)
        for k in range(TOP_K):
            hit = lane == idx[:, k:k + 1]
            val = jnp.sum(jnp.where(hit, pos, 0.0), axis=-1, keepdims=True)
            dest = jnp.where(lane == k, val, dest)
        dest_ref[...] = dest.astype(I32)
        cnt_ref[...] = carry_ref[...]


def _route_rank(sel, idx4, row_block):
    n_tok = sel.shape[0]
    tb = _divisor_tile(n_tok, 256)
    tri = (lax.broadcasted_iota(I32, (tb, tb), 0) > lax.broadcasted_iota(I32, (tb, tb), 1)).astype(BF16)
    return pl.pallas_call(
        functools.partial(_route_rank_kernel, row_block),
        grid=(2, n_tok // tb),
        in_specs=[pl.BlockSpec((tb, LANES), lambda p, i: (i, 0)),
                  pl.BlockSpec((tb, LANES), lambda p, i: (i, 0)),
                  pl.BlockSpec((tb, tb), lambda p, i: (0, 0))],
        out_specs=[pl.BlockSpec((tb, LANES), lambda p, i: (i * p, 0)),
                   pl.BlockSpec((SUBLANES, LANES), lambda p, i: (0, 0))],
        out_shape=[jax.ShapeDtypeStruct((n_tok, LANES), I32),
                   jax.ShapeDtypeStruct((SUBLANES, LANES), F32)],
        scratch_shapes=[pltpu.VMEM((n_tok, LANES), F32), pltpu.VMEM((SUBLANES, LANES), F32),
                        pltpu.VMEM((SUBLANES, LANES), F32)],
        compiler_params=_params(("arbitrary", "arbitrary")),
        name="route_rank",
    )(sel, idx4, tri)


def _dispatch_kernel(dest_ref, hn_ref, xs_in_ref, xs_ref, sem):
    del xs_in_ref
    tt = hn_ref.shape[0]

    def copy(t, d):
        return pltpu.make_async_copy(hn_ref.at[pl.ds(t, 1), :], xs_ref.at[pl.ds(d, 1), :], sem)

    def issue(t, carry):
        for k in range(TOP_K):
            copy(t, dest_ref[TOP_K * t + k]).start()
        return carry

    lax.fori_loop(0, tt, issue, 0)

    def drain(t, carry):
        for k in range(TOP_K):
            copy(t, dest_ref[TOP_K * t + k]).wait()
        return carry

    lax.fori_loop(0, tt, drain, 0)


def _dispatch(dest_flat, hn_packed, n_rows):
    n_tok, half = hn_packed.shape
    tt = _divisor_tile(n_tok, 512)
    xs0 = jnp.zeros((n_rows, half), U32)
    return pl.pallas_call(
        _dispatch_kernel,
        grid=(n_tok // tt,),
        in_specs=[pl.BlockSpec((tt * TOP_K,), lambda i: (i,), memory_space=pltpu.SMEM),
                  pl.BlockSpec((tt, half), lambda i: (i, 0)),
                  pl.BlockSpec(memory_space=pl.ANY)],
        out_specs=pl.BlockSpec(memory_space=pl.ANY),
        out_shape=jax.ShapeDtypeStruct((n_rows, half), U32),
        scratch_shapes=[pltpu.SemaphoreType.DMA(())],
        input_output_aliases={2: 0},
        compiler_params=_params(("arbitrary",)),
        name="dispatch",
    )(dest_flat, hn_packed, xs0)


def _expert_ffn_kernel(be_ref, bv_ref, xs_ref, wg_ref, wu_ref, wd_ref, bg_ref, bu_ref, bd_ref,
                       out_ref, x_ref, acc_ref):
    del be_ref
    b = pl.program_id(0)
    f = pl.program_id(1)
    n_f = pl.num_programs(1)
    live = bv_ref[b] > 0

    @pl.when(live & (f == 0))
    def _():
        hi, lo = _unpack_bf16_pairs(xs_ref[...])
        x_ref[...] = jnp.concatenate([hi.astype(BF16), lo.astype(BF16)], axis=1)
        acc_ref[...] = jnp.zeros_like(acc_ref)

    @pl.when(live)
    def _():
        x = x_ref[...]
        gate = jnp.dot(x, wg_ref[...], preferred_element_type=F32) + bg_ref[...]
        up = jnp.dot(x, wu_ref[...], preferred_element_type=F32) + bu_ref[...]
        gate = jnp.minimum(gate, SWIGLU_LIMIT)
        up = jnp.clip(up, -SWIGLU_LIMIT, SWIGLU_LIMIT)
        act = (up + 1.0) * (gate * jax.nn.sigmoid(SWIGLU_ALPHA * gate))
        acc_ref[...] += jnp.dot(act.astype(BF16), wd_ref[...], preferred_element_type=F32)

    @pl.when(live & (f == n_f - 1))
    def _():
        out_ref[...] = _pack_bf16_pairs(acc_ref[...] + bd_ref[...])

    @pl.when(jnp.logical_not(live) & (f == n_f - 1))
    def _():
        out_ref[...] = jnp.zeros_like(out_ref)


def _expert_ffn(block_e, block_valid, xs, wg, wu, wd, bg, bu, bd, row_block):
    n_rows, half = xs.shape
    d = 2 * half
    d_ff = wg.shape[2]
    tf = _divisor_tile(d_ff, 512)
    n_blocks = n_rows // row_block
    grid_spec = pltpu.PrefetchScalarGridSpec(
        num_scalar_prefetch=2,
        grid=(n_blocks, d_ff // tf),
        in_specs=[
            pl.BlockSpec((row_block, half), lambda b, f, be, bv: (b, 0)),
            pl.BlockSpec((None, d, tf), lambda b, f, be, bv: (be[b], 0, f)),
            pl.BlockSpec((None, d, tf), lambda b, f, be, bv: (be[b], 0, f)),
            pl.BlockSpec((None, tf, d), lambda b, f, be, bv: (be[b], f, 0)),
            pl.BlockSpec((None, 1, tf), lambda b, f, be, bv: (be[b], 0, f)),
            pl.BlockSpec((None, 1, tf), lambda b, f, be, bv: (be[b], 0, f)),
            pl.BlockSpec((None, 1, d), lambda b, f, be, bv: (be[b], 0, 0)),
        ],
        out_specs=pl.BlockSpec((row_block, half), lambda b, f, be, bv: (b, 0)),
        scratch_shapes=[pltpu.VMEM((row_block, d), BF16), pltpu.VMEM((row_block, d), F32)],
    )
    return pl.pallas_call(
        _expert_ffn_kernel,
        grid_spec=grid_spec,
        out_shape=jax.ShapeDtypeStruct((n_rows, half), U32),
        compiler_params=_params(("arbitrary", "arbitrary")),
        name="expert_ffn",
    )(block_e, block_valid, xs, wg, wu, wd, bg, bu, bd)


def _combine_kernel(dest_ref, h_ref, w4_ref, fg_ref, rows_hbm_ref, o_ref, buf_ref, sem):
    tt = h_ref.shape[0]

    def copy(t, k, d):
        return pltpu.make_async_copy(rows_hbm_ref.at[pl.ds(d, 1), :], buf_ref.at[k, pl.ds(t, 1), :], sem)

    def issue(t, carry):
        for k in range(TOP_K):
            copy(t, k, dest_ref[TOP_K * t + k]).start()
        return carry

    lax.fori_loop(0, tt, issue, 0)

    def drain(t, carry):
        for k in range(TOP_K):
            copy(t, k, dest_ref[TOP_K * t + k]).wait()
        return carry

    lax.fori_loop(0, tt, drain, 0)

    w4 = w4_ref[...]
    half = buf_ref.shape[2]
    y_hi = jnp.zeros((tt, half), F32)
    y_lo = jnp.zeros((tt, half), F32)
    for k in range(TOP_K):
        hi, lo = _unpack_bf16_pairs(buf_ref[k])
        wk = w4[:, k:k + 1]
        y_hi = y_hi + wk * hi
        y_lo = y_lo + wk * lo
    h = h_ref[...] + jnp.concatenate([y_hi, y_lo], axis=1)
    o_ref[...] = _rms(h, fg_ref[...])


def _combine(dest_flat, h, w4, final_g, rows_packed):
    n_tok, d = h.shape
    half = d // 2
    tt = _divisor_tile(n_tok, 256)
    return pl.pallas_call(
        _combine_kernel,
        grid=(n_tok // tt,),
        in_specs=[pl.BlockSpec((tt * TOP_K,), lambda i: (i,), memory_space=pltpu.SMEM),
                  pl.BlockSpec((tt, d), lambda i: (i, 0)),
                  pl.BlockSpec((tt, LANES), lambda i: (i, 0)),
                  pl.BlockSpec((1, d), lambda i: (0, 0)),
                  pl.BlockSpec(memory_space=pl.ANY)],
        out_specs=pl.BlockSpec((tt, d), lambda i: (i, 0)),
        out_shape=jax.ShapeDtypeStruct((n_tok, d), F32),
        scratch_shapes=[pltpu.VMEM((TOP_K, tt, half), U32), pltpu.SemaphoreType.DMA(())],
        compiler_params=_params(("arbitrary",)),
        name="combine",
    )(dest_flat, h, w4, final_g.reshape(1, d).astype(F32), rows_packed)


def _layer(h2, batch, seq, lb, norm1_g, w_in, hg_norm_g, gate_w2, gate_b, gla_norm_g, w_proj_a, w_proj_b,
           w_out, norm2_g, router_w, router_b, w_gate_up, b_gate_up, w_down, b_down, out_norm_g):
    n_tok, d = h2.shape
    hf = lb.shape[1]
    hv = w_proj_a.shape[0]
    gk = gate_w2.shape[2]
    gv = w_proj_b.shape[0]
    rank = gate_w2.shape[1]
    hg_heads, gla_heads = hf // HEAD_DK, gk // HEAD_DK
    n_exp = router_w.shape[1]

    (qa, kf, gf, kb, gb, va, oga, qb, kg, vb, ogb, sga, sgb, gg) = _in_proj(
        h2, norm1_g, w_in, lb, gate_w2, gate_b, (hf, hv, gk, gv, rank))
    oa = _recurrence(qa, kf, kb, va, gf, gb, oga, hg_norm_g, batch, seq, hg_heads, hv // hg_heads,
                     name="hgrn_recurrence")
    ob = _recurrence(qb, kg, kg, vb, gg, gg, ogb, gla_norm_g, batch, seq, gla_heads, gv // gla_heads,
                     gb_off=gla_heads, name="gla_recurrence")
    h, hn_packed, w4, idx4, sel = _mix_out(oa, ob, sga, sgb, h2, w_proj_a, w_proj_b, w_out, norm2_g,
                                           router_w, router_b)

    n_pairs = n_tok * TOP_K
    row_block = _divisor_tile(n_pairs, 512)
    n_blocks = n_pairs // row_block + n_exp
    dest4, counts = _route_rank(sel, idx4, row_block)
    dest_flat = dest4[:, :TOP_K].reshape(-1)
    cnt = counts[0, :n_exp].astype(I32)
    blocks_e = (cnt + row_block - 1) // row_block
    end_blk = jnp.cumsum(blocks_e)
    start_blk = end_blk - blocks_e
    bidx = jnp.arange(n_blocks, dtype=I32)
    block_e = jnp.minimum(jnp.searchsorted(end_blk, bidx, side="right"), n_exp - 1).astype(I32)
    block_valid = jnp.clip(cnt[block_e] - (bidx - start_blk[block_e]) * row_block, 0, row_block).astype(I32)

    xs = _dispatch(dest_flat, hn_packed, n_blocks * row_block)
    wg = w_gate_up[:, :, 0::2].astype(BF16)
    wu = w_gate_up[:, :, 1::2].astype(BF16)
    bg = b_gate_up[:, None, 0::2].astype(F32)
    bu = b_gate_up[:, None, 1::2].astype(F32)
    rows = _expert_ffn(block_e, block_valid, xs, wg, wu, w_down.astype(BF16), bg, bu,
                       b_down[:, None, :].astype(F32), row_block)
    return _combine(dest_flat, h, w4, out_norm_g, rows)


def kernel(x, norm1_g, w_in, hg_lb_logits, hg_norm_g, gla_gate_w2, gla_gate_b, gla_norm_g, w_proj_a, w_proj_b,
           w_out, norm2_g, router_w, router_b, w_gate_up, b_gate_up, w_down, b_down, final_norm_g):
    batch, seq, d = x.shape
    depth = w_in.shape[0]
    assert depth == 1, "the final RMSNorm is fused into the last layer's combine kernel"
    lb_all = jnp.cumsum(jax.nn.softmax(hg_lb_logits.astype(F32), axis=1), axis=1)
    h2 = x.reshape(batch * seq, d)
    out = _layer(h2, batch, seq, lb_all[:, 0], norm1_g[0], w_in[0], hg_norm_g[0], gla_gate_w2[0], gla_gate_b[0],
                 gla_norm_g[0], w_proj_a[0], w_proj_b[0], w_out[0], norm2_g[0], router_w[0], router_b[0],
                 w_gate_up[0], b_gate_up[0], w_down[0], b_down[0], final_norm_g)
    return out.reshape(batch, seq, d)
```

```python
import functools

import jax
import jax.numpy as jnp
from jax import lax
from jax.experimental import pallas as pl
from jax.experimental.pallas import tpu as pltpu

F32 = jnp.float32
BF16 = jnp.bfloat16
U32 = jnp.uint32
I32 = jnp.int32

LANES = 128
SUBLANES = 8
VMEM_LIMIT_BYTES = 56 * 1024 * 1024

HEAD_DK = 128
GLA_DV = 256
HG_DV = 128
GATE_NORM = 16.0
TOP_K = 4
SWIGLU_LIMIT = 7.0
SWIGLU_ALPHA = 1.702
RMS_EPS = 1e-5
CHUNK = 64
NEG_BIG = -1e30


def _divisor_tile(n, pref):
    t = min(n, pref)
    while n % t:
        t //= 2
    return t


def _params(sem, vmem=VMEM_LIMIT_BYTES):
    return pltpu.CompilerParams(dimension_semantics=sem, vmem_limit_bytes=vmem)


def _pack_bf16_pairs(a):
    n = a.shape[-1] // 2
    hi = lax.bitcast_convert_type(a[:, :n].astype(BF16).astype(F32), U32)
    lo = lax.bitcast_convert_type(a[:, n:].astype(BF16).astype(F32), U32)
    return hi | (lo >> 16)


def _unpack_bf16_pairs(p):
    hi = lax.bitcast_convert_type(p & jnp.uint32(0xFFFF0000), F32)
    lo = lax.bitcast_convert_type(p << 16, F32)
    return hi, lo


def _rms(x, g):
    ms = jnp.mean(x * x, axis=-1, keepdims=True)
    return x * lax.rsqrt(ms + RMS_EPS) * g


def _log_sigmoid(x):
    return jnp.minimum(x, 0.0) - jnp.log1p(jnp.exp(-jnp.abs(x)))


def _in_proj_kernel(bounds, x_ref, g1_ref, w_ref, lb_ref, w2_ref, b2_ref,
                    qa_ref, kf_ref, gf_ref, kb_ref, gb_ref, va_ref, oga_ref,
                    qb_ref, kg_ref, vb_ref, ogb_ref, sga_ref, sgb_ref, gg_ref,
                    xn_ref):
    j = pl.program_id(1)

    @pl.when(j == 0)
    def _():
        xn_ref[...] = _rms(x_ref[...], g1_ref[...]).astype(BF16)

    z = jnp.dot(xn_ref[...], w_ref[...], preferred_element_type=F32)
    scale = HEAD_DK ** -0.5

    def seg(k):
        return (j >= bounds[k]) & (j < bounds[k + 1])

    @pl.when(seg(0))
    def _():
        qa_ref[...] = (z * jax.nn.sigmoid(z) * scale).astype(BF16)

    def forget(k_ref, g_ref):
        lb = lb_ref[0:1, :]
        f = lb + (1.0 - lb) * jax.nn.sigmoid(z)
        k_ref[...] = (1.0 - f).astype(BF16)
        g_ref[...] = jnp.log(f)

    @pl.when(seg(1))
    def _():
        forget(kf_ref, gf_ref)

    @pl.when(seg(2))
    def _():
        forget(kb_ref, gb_ref)

    @pl.when(seg(3))
    def _():
        va_ref[...] = z.astype(BF16)

    @pl.when(seg(4))
    def _():
        oga_ref[...] = (z * jax.nn.sigmoid(z)).astype(BF16)

    @pl.when(seg(5))
    def _():
        qb_ref[...] = (z * scale).astype(BF16)

    @pl.when(seg(6))
    def _():
        kg_ref[...] = z.astype(BF16)

    @pl.when(seg(7))
    def _():
        vb_ref[...] = z.astype(BF16)

    @pl.when(seg(8))
    def _():
        ogb_ref[...] = (z * jax.nn.sigmoid(z)).astype(BF16)

    @pl.when(seg(9))
    def _():
        sga_ref[...] = jax.nn.sigmoid(z).astype(BF16)

    @pl.when(seg(10))
    def _():
        sgb_ref[...] = jax.nn.sigmoid(z).astype(BF16)

    @pl.when(seg(11))
    def _():
        lr = z[:, :LANES].astype(BF16)
        logits = jnp.dot(lr, w2_ref[...], preferred_element_type=F32) + b2_ref[...]
        gg_ref[...] = _log_sigmoid(logits) * (1.0 / GATE_NORM)


def _in_proj(x2, norm1_g, w_in, lb, gate_w2, gate_b, dims):
    n_tok, d = x2.shape
    hf, hv, gk, gv, rank = dims
    widths = [hf, hf, hf, hv, hv, gk, gk, gv, gv]
    off_lr = sum(widths)
    tn = _divisor_tile(gk, 512)
    tm = _divisor_tile(n_tok, 512)
    widths_all = widths + [d, d, tn]
    w_lr = jnp.pad(w_in[:, off_lr:off_lr + 2 * rank], ((0, 0), (0, tn - 2 * rank)))
    w_cat = jnp.concatenate([w_in[:, :off_lr], w_in[:, off_lr + 2 * rank:], w_lr], axis=1).astype(BF16)
    n_col = w_cat.shape[1]
    lb_tab = jnp.zeros((SUBLANES, n_col), F32)
    lb_tab = lb_tab.at[0, hf:2 * hf].set(lb[0]).at[0, 2 * hf:3 * hf].set(lb[1])
    w2 = jnp.zeros((LANES, 2 * gk), F32)
    w2 = w2.at[:rank, :gk].set(gate_w2[0]).at[rank:2 * rank, gk:].set(gate_w2[1]).astype(BF16)
    b2 = jnp.concatenate([gate_b[0], gate_b[1]])[None, :].astype(F32)

    bounds = [0]
    for w in widths_all:
        bounds.append(bounds[-1] + w // tn)
    n_j = bounds[-1]

    def out_spec(k):
        s, n = bounds[k], bounds[k + 1] - bounds[k]
        return pl.BlockSpec((tm, tn), lambda i, j, s=s, n=n: (i, jnp.clip(j - s, 0, n - 1)))

    seg_of_out = [0, 1, 1, 2, 2, 3, 4, 5, 6, 7, 8, 9, 10]
    out_dtypes = [BF16, BF16, F32, BF16, F32, BF16, BF16, BF16, BF16, BF16, BF16, BF16, BF16]
    out_shape = [jax.ShapeDtypeStruct((n_tok, widths_all[s]), dt) for s, dt in zip(seg_of_out, out_dtypes)]
    out_specs = [out_spec(s) for s in seg_of_out]
    out_shape.append(jax.ShapeDtypeStruct((n_tok, 2 * gk), F32))
    out_specs.append(pl.BlockSpec((tm, 2 * gk), lambda i, j: (i, 0)))

    return pl.pallas_call(
        functools.partial(_in_proj_kernel, tuple(bounds)),
        grid=(n_tok // tm, n_j),
        in_specs=[
            pl.BlockSpec((tm, d), lambda i, j: (i, 0)),
            pl.BlockSpec((1, d), lambda i, j: (0, 0)),
            pl.BlockSpec((d, tn), lambda i, j: (0, j)),
            pl.BlockSpec((SUBLANES, tn), lambda i, j: (0, j)),
            pl.BlockSpec((LANES, 2 * gk), lambda i, j: (0, 0)),
            pl.BlockSpec((1, 2 * gk), lambda i, j: (0, 0)),
        ],
        out_specs=out_specs,
        out_shape=out_shape,
        scratch_shapes=[pltpu.VMEM((tm, d), BF16)],
        compiler_params=_params(("arbitrary", "arbitrary")),
        name="in_proj",
    )(x2, norm1_g.reshape(1, d), w_cat, lb_tab, w2, b2)


def _chunk_step(q, k, v, g, st_ref, rev):
    nt = CHUNK // SUBLANES
    row = lax.broadcasted_iota(I32, (SUBLANES, LANES), 0)
    lrow = (SUBLANES - 1 - row) if rev else row

    def phys(jl):
        return nt - 1 - jl if rev else jl

    def prow(x, rl):
        r = SUBLANES - 1 - rl if rev else rl
        return x[r:r + 1, :]

    def tile(x, jl):
        p = phys(jl)
        return x[SUBLANES * p:SUBLANES * (p + 1), :]

    def scan(x):
        for s in (1, 2, 4):
            if rev:
                x = x + jnp.where(row < SUBLANES - s, pltpu.roll(x, SUBLANES - s, 0), 0.0)
            else:
                x = x + jnp.where(row >= s, pltpu.roll(x, s, 0), 0.0)
        return x

    gt = [tile(g, jl) for jl in range(nt)]
    qt = [tile(q, jl) for jl in range(nt)]
    kt = [tile(k, jl) for jl in range(nt)]
    ct = [scan(x) for x in gt]
    tot = [prow(c, SUBLANES - 1) for c in ct]
    suf = [t - c for t, c in zip(tot, ct)]

    def assemble(tiles):
        out = [None] * nt
        for jl, x in enumerate(tiles):
            out[phys(jl)] = jnp.zeros((SUBLANES, LANES), F32) if x is None else x
        return jnp.concatenate(out, axis=0).astype(BF16)

    def plus(a, b):
        return a if b is None else a + b

    levels = []

    for nb in (8, 4, 2):
        half = nb // 2
        qs, ks = [None] * nt, [None] * nt
        for base in range(0, nt, nb):
            acc = None
            for jl in range(base + half, base + nb):
                qs[jl] = qt[jl] * jnp.exp(plus(ct[jl], acc))
                acc = plus(tot[jl], acc)
            acc = None
            for jl in range(base + half - 1, base - 1, -1):
                ks[jl] = kt[jl] * jnp.exp(plus(suf[jl], acc))
                acc = plus(tot[jl], acc)
        levels.append((SUBLANES * nb, assemble(qs), assemble(ks)))

    for size in (8, 4):
        half = size // 2
        is_q = (lrow & (size - 1)) >= half
        qs, ks = [], []
        for jl in range(nt):
            if size == 8:
                ref = prow(ct[jl], half - 1)
            else:
                ref = jnp.where(lrow >= size, prow(ct[jl], size + half - 1), prow(ct[jl], half - 1))
            d = ct[jl] - ref
            e = jnp.exp(jnp.where(is_q, d, -d))
            qs.append(jnp.where(is_q, qt[jl] * e, 0.0))
            ks.append(jnp.where(is_q, 0.0, kt[jl] * e))
        levels.append((size, assemble(qs), assemble(ks)))

    odd = (lrow & 1) == 1
    qs = [jnp.where(odd, qt[jl] * jnp.exp(gt[jl]), 0.0) for jl in range(nt)]
    ks = [jnp.where(odd, 0.0, kt[jl]) for jl in range(nt)]
    levels.append((2, assemble(qs), assemble(ks)))
    levels.append((1, q.astype(BF16), k.astype(BF16)))

    ti = lax.broadcasted_iota(I32, (CHUNK, CHUNK), 0)
    si = lax.broadcasted_iota(I32, (CHUNK, CHUNK), 1)
    blk = ti ^ si
    nt_dims = (((1,), (1,)), ((), ()))
    scores = None
    for size, qm, km in levels:
        s_m = lax.dot_general(qm, km, nt_dims, preferred_element_type=F32)
        if size < CHUNK:
            s_m = jnp.where(blk < size, s_m, 0.0)
        scores = s_m if scores is None else scores + s_m
    out = jnp.dot(scores.astype(BF16), v, preferred_element_type=F32)

    q_in, k_out = [None] * nt, [None] * nt
    acc = None
    for jl in range(nt):
        q_in[jl] = qt[jl] * jnp.exp(plus(ct[jl], acc))
        acc = plus(tot[jl], acc)
    total = acc
    acc = None
    for jl in range(nt - 1, -1, -1):
        k_out[jl] = kt[jl] * jnp.exp(plus(suf[jl], acc))
        acc = plus(tot[jl], acc)
    st = st_ref[...]
    out = out + lax.dot_general(assemble(q_in), st.astype(BF16), nt_dims, preferred_element_type=F32)
    tn_dims = (((0,), (0,)), ((), ()))
    st_ref[...] = st * jnp.exp(total) + lax.dot_general(v, assemble(k_out), tn_dims,
                                                        preferred_element_type=F32)
    return out


def _recurrence_kernel(n_chunks, q_ref, kf_ref, kb_ref, v_ref, gf_ref, gb_ref, og_ref, ng_ref,
                       o_ref, acc_ref, stf_ref, stb_ref):
    acc_ref[...] = jnp.zeros_like(acc_ref)
    stf_ref[...] = jnp.zeros_like(stf_ref)
    stb_ref[...] = jnp.zeros_like(stb_ref)

    def body(c, carry):
        for rev, k_ref, g_ref, st_ref in ((False, kf_ref, gf_ref, stf_ref), (True, kb_ref, gb_ref, stb_ref)):
            cc = (n_chunks - 1 - c) if rev else c
            rows = pl.ds(pl.multiple_of(cc * CHUNK, CHUNK), CHUNK)
            o = _chunk_step(q_ref[rows, :].astype(F32), k_ref[rows, :].astype(F32), v_ref[rows, :],
                            g_ref[rows, :], st_ref, rev)
            acc_ref[rows, :] += o
        return carry

    lax.fori_loop(0, n_chunks, body, 0)
    o_ref[...] = (_rms(acc_ref[...], ng_ref[...]) * og_ref[...].astype(F32)).astype(BF16)


def _recurrence(q, kf, kb, v, gf, gb, og, norm_g, batch, seq, heads, dv, kb_off=0, gb_off=0, name=""):
    n_tok = batch * seq
    n_chunks = seq // CHUNK

    def spec(width, off=0):
        return pl.BlockSpec((seq, width), lambda b, h, off=off: (b, h + off))

    return pl.pallas_call(
        functools.partial(_recurrence_kernel, n_chunks),
        grid=(batch, heads),
        in_specs=[spec(HEAD_DK), spec(HEAD_DK), spec(HEAD_DK, kb_off), spec(dv),
                  spec(HEAD_DK), spec(HEAD_DK, gb_off), spec(dv),
                  pl.BlockSpec((1, dv), lambda b, h: (0, 0))],
        out_specs=spec(dv),
        out_shape=jax.ShapeDtypeStruct((n_tok, heads * dv), BF16),
        scratch_shapes=[pltpu.VMEM((seq, dv), F32), pltpu.VMEM((dv, HEAD_DK), F32),
                        pltpu.VMEM((dv, HEAD_DK), F32)],
        compiler_params=_params(("arbitrary", "arbitrary")),
        name=name,
    )(q, kf, kb, v, gf, gb, og, norm_g.reshape(1, dv).astype(F32))


def _mix_out_kernel(oa_ref, ob_ref, sga_ref, sgb_ref, x_ref, wpa_ref, wpb_ref, wo_ref, g2_ref, rw_ref, rb_ref,
                    h_ref, hn_ref, w4_ref, idx_ref, sel_ref):
    ya = jnp.dot(oa_ref[...], wpa_ref[...], preferred_element_type=F32)
    yb = jnp.dot(ob_ref[...], wpb_ref[...], preferred_element_type=F32)
    merged = sga_ref[...].astype(F32) * ya + sgb_ref[...].astype(F32) * yb
    h = x_ref[...] + jnp.dot(merged.astype(BF16), wo_ref[...], preferred_element_type=F32)
    h_ref[...] = h
    hn = _rms(h, g2_ref[...])
    hn_ref[...] = _pack_bf16_pairs(hn)

    logits = jnp.dot(hn, rw_ref[...], preferred_element_type=F32,
                     precision=lax.Precision.HIGHEST) + rb_ref[...]
    lane = lax.broadcasted_iota(I32, logits.shape, 1)
    lane_f = lane.astype(F32)
    work = logits
    vals, idxs = [], []
    for _ in range(TOP_K):
        m = jnp.max(work, axis=-1, keepdims=True)
        idx = jnp.min(jnp.where(work == m, lane_f, float(LANES)), axis=-1, keepdims=True)
        vals.append(m)
        idxs.append(idx)
        work = jnp.where(lane_f == idx, -jnp.inf, work)
    es = [jnp.exp(v - vals[0]) for v in vals]
    denom = es[0]
    for e in es[1:]:
        denom = denom + e
    w4 = jnp.zeros(logits.shape, F32)
    i4 = jnp.zeros(logits.shape, F32)
    sel = jnp.zeros(logits.shape, F32)
    for k in range(TOP_K):
        w4 = jnp.where(lane == k, es[k] / denom, w4)
        i4 = jnp.where(lane == k, idxs[k], i4)
        sel = jnp.where(lane_f == idxs[k], 1.0, sel)
    w4_ref[...] = w4
    idx_ref[...] = i4.astype(I32)
    sel_ref[...] = sel.astype(BF16)


def _mix_out(oa, ob, sga, sgb, x2, w_proj_a, w_proj_b, w_out, norm2_g, router_w, router_b):
    n_tok, d = x2.shape
    hv, gv = oa.shape[1], ob.shape[1]
    n_exp = router_w.shape[1]
    tm = _divisor_tile(n_tok, 256)
    rw = jnp.pad(router_w.astype(F32), ((0, 0), (0, LANES - n_exp)))
    rb = jnp.concatenate([router_b.astype(F32), jnp.full((LANES - n_exp,), NEG_BIG, F32)])[None, :]

    def rows(width):
        return pl.BlockSpec((tm, width), lambda i: (i, 0))

    def whole(r, c):
        return pl.BlockSpec((r, c), lambda i: (0, 0))

    return pl.pallas_call(
        _mix_out_kernel,
        grid=(n_tok // tm,),
        in_specs=[rows(hv), rows(gv), rows(d), rows(d), rows(d),
                  whole(hv, d), whole(gv, d), whole(d, d), whole(1, d), whole(d, LANES), whole(1, LANES)],
        out_specs=[rows(d), rows(d // 2), rows(LANES), rows(LANES), rows(LANES)],
        out_shape=[jax.ShapeDtypeStruct((n_tok, d), F32),
                   jax.ShapeDtypeStruct((n_tok, d // 2), U32),
                   jax.ShapeDtypeStruct((n_tok, LANES), F32),
                   jax.ShapeDtypeStruct((n_tok, LANES), I32),
                   jax.ShapeDtypeStruct((n_tok, LANES), BF16)],
        compiler_params=_params(("arbitrary",)),
        name="mix_out",
    )(oa, ob, sga, sgb, x2, w_proj_a.astype(BF16), w_proj_b.astype(BF16), w_out.astype(BF16),
      norm2_g.reshape(1, d).astype(F32), rw, rb)


def _route_rank_kernel(row_block, sel_ref, idx_ref, tri_ref, dest_ref, cnt_ref, rank_ref, carry_ref, start_ref):
    p = pl.program_id(0)
    i = pl.program_id(1)
    tb = sel_ref.shape[0]
    rows = pl.ds(pl.multiple_of(i * tb, tb), tb)

    @pl.when((p == 0) & (i == 0))
    def _():
        carry_ref[...] = jnp.zeros_like(carry_ref)

    @pl.when(p == 0)
    def _():
        sel = sel_ref[...]
        before = jnp.dot(tri_ref[...], sel, preferred_element_type=F32)
        rank_ref[rows, :] = before + carry_ref[0:1, :]
        carry_ref[...] = carry_ref[...] + jnp.sum(sel.astype(F32), axis=0, keepdims=True)

    @pl.when((p == 1) & (i == 0))
    def _():
        counts = carry_ref[...]
        padded = jnp.ceil(counts * (1.0 / row_block)) * row_block
        lane = lax.broadcasted_iota(I32, padded.shape, 1)
        run = padded
        s = 1
        while s < LANES:
            run = run + jnp.where(lane >= s, pltpu.roll(run, s, 1), 0.0)
            s *= 2
        start_ref[...] = run - padded

    @pl.when(p == 1)
    def _():
        pos = rank_ref[rows, :] + start_ref[0:1, :]
        idx = idx_ref[...]
        lane = lax.broadcasted_iota(I32, pos.shape, 1)
        dest = jnp.zeros(pos.shape, F32)
        for k in range(TOP_K):
            hit = lane == idx[:, k:k + 1]
            val = jnp.sum(jnp.where(hit, pos, 0.0), axis=-1, keepdims=True)
            dest = jnp.where(lane == k, val, dest)
        dest_ref[...] = dest.astype(I32)
        cnt_ref[...] = carry_ref[...]


def _route_rank(sel, idx4, row_block):
    n_tok = sel.shape[0]
    tb = _divisor_tile(n_tok, 256)
    tri = (lax.broadcasted_iota(I32, (tb, tb), 0) > lax.broadcasted_iota(I32, (tb, tb), 1)).astype(BF16)
    return pl.pallas_call(
        functools.partial(_route_rank_kernel, row_block),
        grid=(2, n_tok // tb),
        in_specs=[pl.BlockSpec((tb, LANES), lambda p, i: (i, 0)),
                  pl.BlockSpec((tb, LANES), lambda p, i: (i, 0)),
                  pl.BlockSpec((tb, tb), lambda p, i: (0, 0))],
        out_specs=[pl.BlockSpec((tb, LANES), lambda p, i: (i * p, 0)),
                   pl.BlockSpec((SUBLANES, LANES), lambda p, i: (0, 0))],
        out_shape=[jax.ShapeDtypeStruct((n_tok, LANES), I32),
                   jax.ShapeDtypeStruct((SUBLANES, LANES), F32)],
        scratch_shapes=[pltpu.VMEM((n_tok, LANES), F32), pltpu.VMEM((SUBLANES, LANES), F32),
                        pltpu.VMEM((SUBLANES, LANES), F32)],
        compiler_params=_params(("arbitrary", "arbitrary")),
        name="route_rank",
    )(sel, idx4, tri)


def _dispatch_kernel(dest_ref, hn_ref, xs_in_ref, xs_ref, sem):
    del xs_in_ref
    tt = hn_ref.shape[0]

    def copy(t, d):
        return pltpu.make_async_copy(hn_ref.at[pl.ds(t, 1), :], xs_ref.at[pl.ds(d, 1), :], sem)

    def issue(t, carry):
        for k in range(TOP_K):
            copy(t, dest_ref[TOP_K * t + k]).start()
        return carry

    lax.fori_loop(0, tt, issue, 0)

    def drain(t, carry):
        for k in range(TOP_K):
            copy(t, dest_ref[TOP_K * t + k]).wait()
        return carry

    lax.fori_loop(0, tt, drain, 0)


def _dispatch(dest_flat, hn_packed, n_rows):
    n_tok, half = hn_packed.shape
    tt = _divisor_tile(n_tok, 512)
    xs0 = jnp.zeros((n_rows, half), U32)
    return pl.pallas_call(
        _dispatch_kernel,
        grid=(n_tok // tt,),
        in_specs=[pl.BlockSpec((tt * TOP_K,), lambda i: (i,), memory_space=pltpu.SMEM),
                  pl.BlockSpec((tt, half), lambda i: (i, 0)),
                  pl.BlockSpec(memory_space=pl.ANY)],
        out_specs=pl.BlockSpec(memory_space=pl.ANY),
        out_shape=jax.ShapeDtypeStruct((n_rows, half), U32),
        scratch_shapes=[pltpu.SemaphoreType.DMA(())],
        input_output_aliases={2: 0},
        compiler_params=_params(("arbitrary",)),
        name="dispatch",
    )(dest_flat, hn_packed, xs0)


GROUP = 2 * LANES


def _regroup_kernel(w_ref, p_ref, o_ref):
    p = p_ref[...]
    for g in range(w_ref.shape[1] // GROUP):
        cols = slice(GROUP * g, GROUP * (g + 1))
        o_ref[:, cols] = jnp.dot(w_ref[:, cols].astype(BF16), p, preferred_element_type=F32).astype(BF16)


def _regroup_gate_up(w_gate_up):
    n_exp, d, two_f = w_gate_up.shape
    tr = _divisor_tile(d, 512)
    tc = _divisor_tile(two_f, 1024)
    src = lax.broadcasted_iota(I32, (GROUP, GROUP), 0)
    dst = lax.broadcasted_iota(I32, (GROUP, GROUP), 1)
    perm = (dst == (src // 2) + LANES * (src % 2)).astype(BF16)
    return pl.pallas_call(
        _regroup_kernel,
        grid=(n_exp, d // tr, two_f // tc),
        in_specs=[pl.BlockSpec((None, tr, tc), lambda e, i, j: (e, i, j)),
                  pl.BlockSpec((GROUP, GROUP), lambda e, i, j: (0, 0))],
        out_specs=pl.BlockSpec((None, tr, tc), lambda e, i, j: (e, i, j)),
        out_shape=jax.ShapeDtypeStruct((n_exp, d, two_f), BF16),
        compiler_params=_params(("arbitrary", "arbitrary", "arbitrary")),
        name="regroup_gate_up",
    )(w_gate_up, perm)


def _regroup_bias(b_gate_up):
    n_exp, two_f = b_gate_up.shape
    b = b_gate_up.reshape(n_exp, two_f // GROUP, LANES, 2)
    return jnp.swapaxes(b, 2, 3).reshape(n_exp, 1, two_f).astype(F32)


def _expert_ffn_kernel(be_ref, bv_ref, xs_ref, wgu_ref, wd_ref, bgu_ref, bd_ref,
                       out_ref, x_ref, acc_ref):
    del be_ref
    b = pl.program_id(0)
    f = pl.program_id(1)
    n_f = pl.num_programs(1)
    live = bv_ref[b] > 0

    @pl.when(live & (f == 0))
    def _():
        hi, lo = _unpack_bf16_pairs(xs_ref[...])
        x_ref[...] = jnp.concatenate([hi.astype(BF16), lo.astype(BF16)], axis=1)
        acc_ref[...] = jnp.zeros_like(acc_ref)

    @pl.when(live)
    def _():
        gu = jnp.dot(x_ref[...], wgu_ref[...], preferred_element_type=F32) + bgu_ref[...]
        acts = []
        for g in range(gu.shape[1] // GROUP):
            gate = jnp.minimum(gu[:, GROUP * g:GROUP * g + LANES], SWIGLU_LIMIT)
            up = jnp.clip(gu[:, GROUP * g + LANES:GROUP * (g + 1)], -SWIGLU_LIMIT, SWIGLU_LIMIT)
            acts.append(((up + 1.0) * (gate * jax.nn.sigmoid(SWIGLU_ALPHA * gate))).astype(BF16))
        act = jnp.concatenate(acts, axis=1)
        acc_ref[...] += jnp.dot(act, wd_ref[...], preferred_element_type=F32)

    @pl.when(live & (f == n_f - 1))
    def _():
        out_ref[...] = _pack_bf16_pairs(acc_ref[...] + bd_ref[...])

    @pl.when(jnp.logical_not(live) & (f == n_f - 1))
    def _():
        out_ref[...] = jnp.zeros_like(out_ref)


def _expert_ffn(block_e, block_valid, xs, wgu, wd, bgu, bd, row_block):
    n_rows, half = xs.shape
    d = 2 * half
    d_ff = wd.shape[1]
    tf = _divisor_tile(d_ff, 512)
    n_blocks = n_rows // row_block
    grid_spec = pltpu.PrefetchScalarGridSpec(
        num_scalar_prefetch=2,
        grid=(n_blocks, d_ff // tf),
        in_specs=[
            pl.BlockSpec((row_block, half), lambda b, f, be, bv: (b, 0)),
            pl.BlockSpec((None, d, 2 * tf), lambda b, f, be, bv: (be[b], 0, f)),
            pl.BlockSpec((None, tf, d), lambda b, f, be, bv: (be[b], f, 0)),
            pl.BlockSpec((None, 1, 2 * tf), lambda b, f, be, bv: (be[b], 0, f)),
            pl.BlockSpec((None, 1, d), lambda b, f, be, bv: (be[b], 0, 0)),
        ],
        out_specs=pl.BlockSpec((row_block, half), lambda b, f, be, bv: (b, 0)),
        scratch_shapes=[pltpu.VMEM((row_block, d), BF16), pltpu.VMEM((row_block, d), F32)],
    )
    return pl.pallas_call(
        _expert_ffn_kernel,
        grid_spec=grid_spec,
        out_shape=jax.ShapeDtypeStruct((n_rows, half), U32),
        compiler_params=_params(("arbitrary", "arbitrary")),
        name="expert_ffn",
    )(block_e, block_valid, xs, wgu, wd, bgu, bd)


def _combine_kernel(dest_ref, h_ref, w4_ref, fg_ref, rows_hbm_ref, o_ref, buf_ref, sem):
    tt = h_ref.shape[0]

    def copy(t, k, d):
        return pltpu.make_async_copy(rows_hbm_ref.at[pl.ds(d, 1), :], buf_ref.at[k, pl.ds(t, 1), :], sem)

    def issue(t, carry):
        for k in range(TOP_K):
            copy(t, k, dest_ref[TOP_K * t + k]).start()
        return carry

    lax.fori_loop(0, tt, issue, 0)

    def drain(t, carry):
        for k in range(TOP_K):
            copy(t, k, dest_ref[TOP_K * t + k]).wait()
        return carry

    lax.fori_loop(0, tt, drain, 0)

    w4 = w4_ref[...]
    half = buf_ref.shape[2]
    y_hi = jnp.zeros((tt, half), F32)
    y_lo = jnp.zeros((tt, half), F32)
    for k in range(TOP_K):
        hi, lo = _unpack_bf16_pairs(buf_ref[k])
        wk = w4[:, k:k + 1]
        y_hi = y_hi + wk * hi
        y_lo = y_lo + wk * lo
    h = h_ref[...] + jnp.concatenate([y_hi, y_lo], axis=1)
    o_ref[...] = _rms(h, fg_ref[...])


def _combine(dest_flat, h, w4, final_g, rows_packed):
    n_tok, d = h.shape
    half = d // 2
    tt = _divisor_tile(n_tok, 256)
    return pl.pallas_call(
        _combine_kernel,
        grid=(n_tok // tt,),
        in_specs=[pl.BlockSpec((tt * TOP_K,), lambda i: (i,), memory_space=pltpu.SMEM),
                  pl.BlockSpec((tt, d), lambda i: (i, 0)),
                  pl.BlockSpec((tt, LANES), lambda i: (i, 0)),
                  pl.BlockSpec((1, d), lambda i: (0, 0)),
                  pl.BlockSpec(memory_space=pl.ANY)],
        out_specs=pl.BlockSpec((tt, d), lambda i: (i, 0)),
        out_shape=jax.ShapeDtypeStruct((n_tok, d), F32),
        scratch_shapes=[pltpu.VMEM((TOP_K, tt, half), U32), pltpu.SemaphoreType.DMA(())],
        compiler_params=_params(("arbitrary",)),
        name="combine",
    )(dest_flat, h, w4, final_g.reshape(1, d).astype(F32), rows_packed)


def _layer(h2, batch, seq, lb, norm1_g, w_in, hg_norm_g, gate_w2, gate_b, gla_norm_g, w_proj_a, w_proj_b,
           w_out, norm2_g, router_w, router_b, w_gate_up, b_gate_up, w_down, b_down, out_norm_g):
    n_tok, d = h2.shape
    hf = lb.shape[1]
    hv = w_proj_a.shape[0]
    gk = gate_w2.shape[2]
    gv = w_proj_b.shape[0]
    rank = gate_w2.shape[1]
    hg_heads, gla_heads = hf // HEAD_DK, gk // HEAD_DK
    n_exp = router_w.shape[1]

    (qa, kf, gf, kb, gb, va, oga, qb, kg, vb, ogb, sga, sgb, gg) = _in_proj(
        h2, norm1_g, w_in, lb, gate_w2, gate_b, (hf, hv, gk, gv, rank))
    oa = _recurrence(qa, kf, kb, va, gf, gb, oga, hg_norm_g, batch, seq, hg_heads, hv // hg_heads,
                     name="hgrn_recurrence")
    ob = _recurrence(qb, kg, kg, vb, gg, gg, ogb, gla_norm_g, batch, seq, gla_heads, gv // gla_heads,
                     gb_off=gla_heads, name="gla_recurrence")
    h, hn_packed, w4, idx4, sel = _mix_out(oa, ob, sga, sgb, h2, w_proj_a, w_proj_b, w_out, norm2_g,
                                           router_w, router_b)

    n_pairs = n_tok * TOP_K
    row_block = _divisor_tile(n_pairs, 512)
    n_blocks = n_pairs // row_block + n_exp
    dest4, counts = _route_rank(sel, idx4, row_block)
    dest_flat = dest4[:, :TOP_K].reshape(-1)
    cnt = counts[0, :n_exp].astype(I32)
    blocks_e = (cnt + row_block - 1) // row_block
    end_blk = jnp.cumsum(blocks_e)
    start_blk = end_blk - blocks_e
    bidx = jnp.arange(n_blocks, dtype=I32)
    block_e = jnp.minimum(jnp.searchsorted(end_blk, bidx, side="right"), n_exp - 1).astype(I32)
    block_valid = jnp.clip(cnt[block_e] - (bidx - start_blk[block_e]) * row_block, 0, row_block).astype(I32)

    xs = _dispatch(dest_flat, hn_packed, n_blocks * row_block)
    rows = _expert_ffn(block_e, block_valid, xs, _regroup_gate_up(w_gate_up), w_down.astype(BF16),
                       _regroup_bias(b_gate_up), b_down[:, None, :].astype(F32), row_block)
    return _combine(dest_flat, h, w4, out_norm_g, rows)


def kernel(x, norm1_g, w_in, hg_lb_logits, hg_norm_g, gla_gate_w2, gla_gate_b, gla_norm_g, w_proj_a, w_proj_b,
           w_out, norm2_g, router_w, router_b, w_gate_up, b_gate_up, w_down, b_down, final_norm_g):
    batch, seq, d = x.shape
    depth = w_in.shape[0]
    assert depth == 1, "the final RMSNorm is fused into the last layer's combine kernel"
    lb_all = jnp.cumsum(jax.nn.softmax(hg_lb_logits.astype(F32), axis=1), axis=1)
    h2 = x.reshape(batch * seq, d)
    out = _layer(h2, batch, seq, lb_all[:, 0], norm1_g[0], w_in[0], hg_norm_g[0], gla_gate_w2[0], gla_gate_b[0],
                 gla_norm_g[0], w_proj_a[0], w_proj_b[0], w_out[0], norm2_g[0], router_w[0], router_b[0],
                 w_gate_up[0], b_gate_up[0], w_down[0], b_down[0], final_norm_g)
    return out.reshape(batch, seq, d)
```

```python
import functools

import jax
import jax.numpy as jnp
from jax import lax
from jax.experimental import pallas as pl
from jax.experimental.pallas import tpu as pltpu

F32 = jnp.float32
BF16 = jnp.bfloat16
U32 = jnp.uint32
I32 = jnp.int32

LANES = 128
SUBLANES = 8
VMEM_LIMIT_BYTES = 56 * 1024 * 1024

HEAD_DK = 128
GLA_DV = 256
HG_DV = 128
GATE_NORM = 16.0
TOP_K = 4
SWIGLU_LIMIT = 7.0
SWIGLU_ALPHA = 1.702
RMS_EPS = 1e-5
CHUNK = 64
NEG_BIG = -1e30


def _divisor_tile(n, pref):
    t = min(n, pref)
    while n % t:
        t //= 2
    return t


def _params(sem, vmem=VMEM_LIMIT_BYTES):
    return pltpu.CompilerParams(dimension_semantics=sem, vmem_limit_bytes=vmem)


def _pack_bf16_pairs(a):
    n = a.shape[-1] // 2
    hi = lax.bitcast_convert_type(a[:, :n].astype(BF16).astype(F32), U32)
    lo = lax.bitcast_convert_type(a[:, n:].astype(BF16).astype(F32), U32)
    return hi | (lo >> 16)


def _unpack_bf16_pairs(p):
    hi = lax.bitcast_convert_type(p & jnp.uint32(0xFFFF0000), F32)
    lo = lax.bitcast_convert_type(p << 16, F32)
    return hi, lo


def _rms(x, g):
    ms = jnp.mean(x * x, axis=-1, keepdims=True)
    return x * lax.rsqrt(ms + RMS_EPS) * g


def _log_sigmoid(x):
    return jnp.minimum(x, 0.0) - jnp.log1p(jnp.exp(-jnp.abs(x)))


def _in_proj_kernel(bounds, x_ref, g1_ref, w_ref, lb_ref, w2_ref, b2_ref,
                    qa_ref, kf_ref, gf_ref, kb_ref, gb_ref, va_ref, oga_ref,
                    qb_ref, kg_ref, vb_ref, ogb_ref, sga_ref, sgb_ref, gg_ref,
                    xn_ref):
    j = pl.program_id(1)

    @pl.when(j == 0)
    def _():
        xn_ref[...] = _rms(x_ref[...], g1_ref[...]).astype(BF16)

    scale = HEAD_DK ** -0.5

    def project():
        return jnp.dot(xn_ref[...], w_ref[...], preferred_element_type=F32)

    def silu(z):
        return z * jax.nn.sigmoid(z)

    def store(ref, fn):
        def epilogue():
            ref[...] = fn(project()).astype(ref.dtype)
        return epilogue

    def forget(k_ref, g_ref):
        def epilogue():
            lb = lb_ref[0:1, :]
            f = lb + (1.0 - lb) * jax.nn.sigmoid(project())
            k_ref[...] = (1.0 - f).astype(BF16)
            g_ref[...] = jnp.log(f)
        return epilogue

    def low_rank_gate():
        lr = project()[:, :LANES].astype(BF16)
        logits = jnp.dot(lr, w2_ref[...], preferred_element_type=F32) + b2_ref[...]
        gg_ref[...] = _log_sigmoid(logits) * (1.0 / GATE_NORM)

    epilogues = [
        store(qa_ref, lambda z: silu(z) * scale),
        forget(kf_ref, gf_ref),
        forget(kb_ref, gb_ref),
        store(va_ref, lambda z: z),
        store(oga_ref, silu),
        store(qb_ref, lambda z: z * scale),
        store(kg_ref, lambda z: z),
        store(vb_ref, lambda z: z),
        store(ogb_ref, silu),
        store(sga_ref, jax.nn.sigmoid),
        store(sgb_ref, jax.nn.sigmoid),
        low_rank_gate,
    ]
    for k, epilogue in enumerate(epilogues):
        pl.when((j >= bounds[k]) & (j < bounds[k + 1]))(epilogue)


def _in_proj(x2, norm1_g, w_in, lb, gate_w2, gate_b, dims):
    n_tok, d = x2.shape
    hf, hv, gk, gv, rank = dims
    widths = [hf, hf, hf, hv, hv, gk, gk, gv, gv]
    off_lr = sum(widths)
    tn = _divisor_tile(gk, 512)
    tm = _divisor_tile(n_tok, 512)
    widths_all = widths + [d, d, tn]
    w_lr = jnp.pad(w_in[:, off_lr:off_lr + 2 * rank], ((0, 0), (0, tn - 2 * rank)))
    w_cat = jnp.concatenate([w_in[:, :off_lr], w_in[:, off_lr + 2 * rank:], w_lr], axis=1).astype(BF16)
    n_col = w_cat.shape[1]
    lb_tab = jnp.zeros((SUBLANES, n_col), F32)
    lb_tab = lb_tab.at[0, hf:2 * hf].set(lb[0]).at[0, 2 * hf:3 * hf].set(lb[1])
    w2 = jnp.zeros((LANES, 2 * gk), F32)
    w2 = w2.at[:rank, :gk].set(gate_w2[0]).at[rank:2 * rank, gk:].set(gate_w2[1]).astype(BF16)
    b2 = jnp.concatenate([gate_b[0], gate_b[1]])[None, :].astype(F32)

    bounds = [0]
    for w in widths_all:
        bounds.append(bounds[-1] + w // tn)
    n_j = bounds[-1]

    def out_spec(k):
        s, n = bounds[k], bounds[k + 1] - bounds[k]
        return pl.BlockSpec((tm, tn), lambda i, j, s=s, n=n: (i, jnp.clip(j - s, 0, n - 1)))

    seg_of_out = [0, 1, 1, 2, 2, 3, 4, 5, 6, 7, 8, 9, 10]
    out_dtypes = [BF16, BF16, F32, BF16, F32, BF16, BF16, BF16, BF16, BF16, BF16, BF16, BF16]
    out_shape = [jax.ShapeDtypeStruct((n_tok, widths_all[s]), dt) for s, dt in zip(seg_of_out, out_dtypes)]
    out_specs = [out_spec(s) for s in seg_of_out]
    out_shape.append(jax.ShapeDtypeStruct((n_tok, 2 * gk), F32))
    out_specs.append(pl.BlockSpec((tm, 2 * gk), lambda i, j: (i, 0)))

    return pl.pallas_call(
        functools.partial(_in_proj_kernel, tuple(bounds)),
        grid=(n_tok // tm, n_j),
        in_specs=[
            pl.BlockSpec((tm, d), lambda i, j: (i, 0)),
            pl.BlockSpec((1, d), lambda i, j: (0, 0)),
            pl.BlockSpec((d, tn), lambda i, j: (0, j)),
            pl.BlockSpec((SUBLANES, tn), lambda i, j: (0, j)),
            pl.BlockSpec((LANES, 2 * gk), lambda i, j: (0, 0)),
            pl.BlockSpec((1, 2 * gk), lambda i, j: (0, 0)),
        ],
        out_specs=out_specs,
        out_shape=out_shape,
        scratch_shapes=[pltpu.VMEM((tm, d), BF16)],
        compiler_params=_params(("arbitrary", "arbitrary")),
        name="in_proj",
    )(x2, norm1_g.reshape(1, d), w_cat, lb_tab, w2, b2)


_NT_DIMS = (((1,), (1,)), ((), ()))
_TN_DIMS = (((0,), (0,)), ((), ()))


def _chunk_prepare(q, k, g, rev):
    nt = CHUNK // SUBLANES
    row = lax.broadcasted_iota(I32, (SUBLANES, LANES), 0)
    lrow = (SUBLANES - 1 - row) if rev else row

    def phys(jl):
        return nt - 1 - jl if rev else jl

    def prow(x, rl):
        r = SUBLANES - 1 - rl if rev else rl
        return x[r:r + 1, :]

    def tile(x, jl):
        p = phys(jl)
        return x[SUBLANES * p:SUBLANES * (p + 1), :]

    def scan(x):
        for s in (1, 2, 4):
            if rev:
                x = x + jnp.where(row < SUBLANES - s, pltpu.roll(x, SUBLANES - s, 0), 0.0)
            else:
                x = x + jnp.where(row >= s, pltpu.roll(x, s, 0), 0.0)
        return x

    gt = [tile(g, jl) for jl in range(nt)]
    qt = [tile(q, jl) for jl in range(nt)]
    kt = [tile(k, jl) for jl in range(nt)]
    ct = [scan(x) for x in gt]
    tot = [prow(c, SUBLANES - 1) for c in ct]
    suf = [t - c for t, c in zip(tot, ct)]

    def assemble(tiles):
        out = [None] * nt
        for jl, x in enumerate(tiles):
            out[phys(jl)] = jnp.zeros((SUBLANES, LANES), F32) if x is None else x
        return jnp.concatenate(out, axis=0).astype(BF16)

    def plus(a, b):
        return a if b is None else a + b

    levels = []

    for nb in (8, 4, 2):
        half = nb // 2
        qs, ks = [None] * nt, [None] * nt
        for base in range(0, nt, nb):
            acc = None
            for jl in range(base + half, base + nb):
                qs[jl] = qt[jl] * jnp.exp(plus(ct[jl], acc))
                acc = plus(tot[jl], acc)
            acc = None
            for jl in range(base + half - 1, base - 1, -1):
                ks[jl] = kt[jl] * jnp.exp(plus(suf[jl], acc))
                acc = plus(tot[jl], acc)
        levels.append((SUBLANES * nb, assemble(qs), assemble(ks)))

    for size in (8, 4):
        half = size // 2
        is_q = (lrow & (size - 1)) >= half
        qs, ks = [], []
        for jl in range(nt):
            if size == 8:
                ref = prow(ct[jl], half - 1)
            else:
                ref = jnp.where(lrow >= size, prow(ct[jl], size + half - 1), prow(ct[jl], half - 1))
            d = ct[jl] - ref
            e = jnp.exp(jnp.where(is_q, d, -d))
            qs.append(jnp.where(is_q, qt[jl] * e, 0.0))
            ks.append(jnp.where(is_q, 0.0, kt[jl] * e))
        levels.append((size, assemble(qs), assemble(ks)))

    odd = (lrow & 1) == 1
    qs = [jnp.where(odd, qt[jl] * jnp.exp(gt[jl]), 0.0) for jl in range(nt)]
    ks = [jnp.where(odd, 0.0, kt[jl]) for jl in range(nt)]
    levels.append((2, assemble(qs), assemble(ks)))
    levels.append((1, q.astype(BF16), k.astype(BF16)))

    ti = lax.broadcasted_iota(I32, (CHUNK, CHUNK), 0)
    si = lax.broadcasted_iota(I32, (CHUNK, CHUNK), 1)
    blk = ti ^ si
    scores = None
    for size, qm, km in levels:
        s_m = lax.dot_general(qm, km, _NT_DIMS, preferred_element_type=F32)
        if size < CHUNK:
            s_m = jnp.where(blk < size, s_m, 0.0)
        scores = s_m if scores is None else scores + s_m

    q_in, k_out = [None] * nt, [None] * nt
    acc = None
    for jl in range(nt):
        q_in[jl] = qt[jl] * jnp.exp(plus(ct[jl], acc))
        acc = plus(tot[jl], acc)
    total = acc
    acc = None
    for jl in range(nt - 1, -1, -1):
        k_out[jl] = kt[jl] * jnp.exp(plus(suf[jl], acc))
        acc = plus(tot[jl], acc)
    return scores.astype(BF16), assemble(q_in), assemble(k_out), jnp.exp(total)


def _chunk_apply(scores, q_in, k_out, decay, v, st_ref):
    st = st_ref[...]
    out = (jnp.dot(scores, v, preferred_element_type=F32)
           + lax.dot_general(q_in, st.astype(BF16), _NT_DIMS, preferred_element_type=F32))
    st_ref[...] = st * decay + lax.dot_general(v, k_out, _TN_DIMS, preferred_element_type=F32)
    return out


def _recurrence_kernel(n_chunks, q_ref, kf_ref, kb_ref, v_ref, gf_ref, gb_ref, og_ref, ng_ref,
                       o_ref, of_ref, ob_ref, stf_ref, stb_ref, sc_ref, qin_ref, kout_ref, dec_ref):
    dirs = ((False, kf_ref, gf_ref, stf_ref, of_ref), (True, kb_ref, gb_ref, stb_ref, ob_ref))
    stf_ref[...] = jnp.zeros_like(stf_ref)
    stb_ref[...] = jnp.zeros_like(stb_ref)

    def rows_of(c, rev):
        cc = (n_chunks - 1 - c) if rev else c
        return pl.ds(pl.multiple_of(cc * CHUNK, CHUNK), CHUNK)

    def prepare(c):
        res = []
        for rev, k_ref, g_ref, _, _ in dirs:
            rows = rows_of(c, rev)
            res.append(_chunk_prepare(q_ref[rows, :].astype(F32), k_ref[rows, :].astype(F32),
                                      g_ref[rows, :], rev))
        return res

    def stash(res):
        for d, (scores, q_in, k_out, decay) in enumerate(res):
            sc_ref[d] = scores
            qin_ref[d] = q_in
            kout_ref[d] = k_out
            dec_ref[d] = jnp.broadcast_to(decay, (SUBLANES, LANES))

    def unstash():
        return [(sc_ref[d], qin_ref[d], kout_ref[d], dec_ref[d][0:1, :]) for d in range(2)]

    def apply(c, staged):
        for (rev, _, _, st_ref, out_ref), (scores, q_in, k_out, decay) in zip(dirs, staged):
            rows = rows_of(c, rev)
            out_ref[rows, :] = _chunk_apply(scores, q_in, k_out, decay, v_ref[rows, :], st_ref)

    stash(prepare(0))

    def body(c, carry):
        staged = unstash()
        nxt = prepare(c + 1)
        apply(c, staged)
        stash(nxt)
        return carry

    lax.fori_loop(0, n_chunks - 1, body, 0)
    apply(n_chunks - 1, unstash())
    o = of_ref[...] + ob_ref[...]
    o_ref[...] = (_rms(o, ng_ref[...]) * og_ref[...].astype(F32)).astype(BF16)


def _recurrence(q, kf, kb, v, gf, gb, og, norm_g, batch, seq, heads, dv, kb_off=0, gb_off=0, name=""):
    n_tok = batch * seq
    n_chunks = seq // CHUNK

    def spec(width, off=0):
        return pl.BlockSpec((seq, width), lambda b, h, off=off: (b, h + off))

    return pl.pallas_call(
        functools.partial(_recurrence_kernel, n_chunks),
        grid=(batch, heads),
        in_specs=[spec(HEAD_DK), spec(HEAD_DK), spec(HEAD_DK, kb_off), spec(dv),
                  spec(HEAD_DK), spec(HEAD_DK, gb_off), spec(dv),
                  pl.BlockSpec((1, dv), lambda b, h: (0, 0))],
        out_specs=spec(dv),
        out_shape=jax.ShapeDtypeStruct((n_tok, heads * dv), BF16),
        scratch_shapes=[pltpu.VMEM((seq, dv), F32), pltpu.VMEM((seq, dv), F32),
                        pltpu.VMEM((dv, HEAD_DK), F32), pltpu.VMEM((dv, HEAD_DK), F32),
                        pltpu.VMEM((2, CHUNK, CHUNK), BF16), pltpu.VMEM((2, CHUNK, HEAD_DK), BF16),
                        pltpu.VMEM((2, CHUNK, HEAD_DK), BF16), pltpu.VMEM((2, SUBLANES, LANES), F32)],
        compiler_params=_params(("arbitrary", "arbitrary")),
        name=name,
    )(q, kf, kb, v, gf, gb, og, norm_g.reshape(1, dv).astype(F32))


def _mix_out_kernel(oa_ref, ob_ref, sga_ref, sgb_ref, x_ref, wpa_ref, wpb_ref, wo_ref, g2_ref, rw_ref, rb_ref,
                    h_ref, hn_ref, w4_ref, idx_ref, sel_ref):
    ya = jnp.dot(oa_ref[...], wpa_ref[...], preferred_element_type=F32)
    yb = jnp.dot(ob_ref[...], wpb_ref[...], preferred_element_type=F32)
    merged = sga_ref[...].astype(F32) * ya + sgb_ref[...].astype(F32) * yb
    h = x_ref[...] + jnp.dot(merged.astype(BF16), wo_ref[...], preferred_element_type=F32)
    h_ref[...] = h
    hn = _rms(h, g2_ref[...])
    hn_ref[...] = _pack_bf16_pairs(hn)

    logits = jnp.dot(hn, rw_ref[...], preferred_element_type=F32,
                     precision=lax.Precision.HIGHEST) + rb_ref[...]
    lane = lax.broadcasted_iota(I32, logits.shape, 1)
    lane_f = lane.astype(F32)
    work = logits
    vals, idxs = [], []
    for _ in range(TOP_K):
        m = jnp.max(work, axis=-1, keepdims=True)
        idx = jnp.min(jnp.where(work == m, lane_f, float(LANES)), axis=-1, keepdims=True)
        vals.append(m)
        idxs.append(idx)
        work = jnp.where(lane_f == idx, -jnp.inf, work)
    es = [jnp.exp(v - vals[0]) for v in vals]
    denom = es[0]
    for e in es[1:]:
        denom = denom + e
    w4 = jnp.zeros(logits.shape, F32)
    i4 = jnp.zeros(logits.shape, F32)
    sel = jnp.zeros(logits.shape, F32)
    for k in range(TOP_K):
        w4 = jnp.where(lane == k, es[k] / denom, w4)
        i4 = jnp.where(lane == k, idxs[k], i4)
        sel = jnp.where(lane_f == idxs[k], 1.0, sel)
    w4_ref[...] = w4
    idx_ref[...] = i4.astype(I32)
    sel_ref[...] = sel.astype(BF16)


def _mix_out(oa, ob, sga, sgb, x2, w_proj_a, w_proj_b, w_out, norm2_g, router_w, router_b):
    n_tok, d = x2.shape
    hv, gv = oa.shape[1], ob.shape[1]
    n_exp = router_w.shape[1]
    tm = _divisor_tile(n_tok, 256)
    rw = jnp.pad(router_w.astype(F32), ((0, 0), (0, LANES - n_exp)))
    rb = jnp.concatenate([router_b.astype(F32), jnp.full((LANES - n_exp,), NEG_BIG, F32)])[None, :]

    def rows(width):
        return pl.BlockSpec((tm, width), lambda i: (i, 0))

    def whole(r, c):
        return pl.BlockSpec((r, c), lambda i: (0, 0))

    return pl.pallas_call(
        _mix_out_kernel,
        grid=(n_tok // tm,),
        in_specs=[rows(hv), rows(gv), rows(d), rows(d), rows(d),
                  whole(hv, d), whole(gv, d), whole(d, d), whole(1, d), whole(d, LANES), whole(1, LANES)],
        out_specs=[rows(d), rows(d // 2), rows(LANES), rows(LANES), rows(LANES)],
        out_shape=[jax.ShapeDtypeStruct((n_tok, d), F32),
                   jax.ShapeDtypeStruct((n_tok, d // 2), U32),
                   jax.ShapeDtypeStruct((n_tok, LANES), F32),
                   jax.ShapeDtypeStruct((n_tok, LANES), I32),
                   jax.ShapeDtypeStruct((n_tok, LANES), BF16)],
        compiler_params=_params(("arbitrary",)),
        name="mix_out",
    )(oa, ob, sga, sgb, x2, w_proj_a.astype(BF16), w_proj_b.astype(BF16), w_out.astype(BF16),
      norm2_g.reshape(1, d).astype(F32), rw, rb)


def _route_rank_kernel(row_block, sel_ref, idx_ref, tri_ref, dest_ref, cnt_ref, rank_ref, carry_ref, start_ref):
    p = pl.program_id(0)
    i = pl.program_id(1)
    tb = sel_ref.shape[0]
    rows = pl.ds(pl.multiple_of(i * tb, tb), tb)

    @pl.when((p == 0) & (i == 0))
    def _():
        carry_ref[...] = jnp.zeros_like(carry_ref)

    @pl.when(p == 0)
    def _():
        sel = sel_ref[...]
        before = jnp.dot(tri_ref[...], sel, preferred_element_type=F32)
        rank_ref[rows, :] = before + carry_ref[0:1, :]
        carry_ref[...] = carry_ref[...] + jnp.sum(sel.astype(F32), axis=0, keepdims=True)

    @pl.when((p == 1) & (i == 0))
    def _():
        counts = carry_ref[...]
        padded = jnp.ceil(counts * (1.0 / row_block)) * row_block
        lane = lax.broadcasted_iota(I32, padded.shape, 1)
        run = padded
        s = 1
        while s < LANES:
            run = run + jnp.where(lane >= s, pltpu.roll(run, s, 1), 0.0)
            s *= 2
        start_ref[...] = run - padded

    @pl.when(p == 1)
    def _():
        pos = rank_ref[rows, :] + start_ref[0:1, :]
        idx = idx_ref[...]
        lane = lax.broadcasted_iota(I32, pos.shape, 1)
        dest = jnp.zeros(pos.shape, F32)
        for k in range(TOP_K):
            hit = lane == idx[:, k:k + 1]
            val = jnp.sum(jnp.where(hit, pos, 0.0), axis=-1, keepdims=True)
            dest = jnp.where(lane == k, val, dest)
        dest_ref[...] = dest.astype(I32)
        cnt_ref[...] = carry_ref[...]


def _route_rank(sel, idx4, row_block):
    n_tok = sel.shape[0]
    tb = _divisor_tile(n_tok, 256)
    tri = (lax.broadcasted_iota(I32, (tb, tb), 0) > lax.broadcasted_iota(I32, (tb, tb), 1)).astype(BF16)
    return pl.pallas_call(
        functools.partial(_route_rank_kernel, row_block),
        grid=(2, n_tok // tb),
        in_specs=[pl.BlockSpec((tb, LANES), lambda p, i: (i, 0)),
                  pl.BlockSpec((tb, LANES), lambda p, i: (i, 0)),
                  pl.BlockSpec((tb, tb), lambda p, i: (0, 0))],
        out_specs=[pl.BlockSpec((tb, LANES), lambda p, i: (i * p, 0)),
                   pl.BlockSpec((SUBLANES, LANES), lambda p, i: (0, 0))],
        out_shape=[jax.ShapeDtypeStruct((n_tok, LANES), I32),
                   jax.ShapeDtypeStruct((SUBLANES, LANES), F32)],
        scratch_shapes=[pltpu.VMEM((n_tok, LANES), F32), pltpu.VMEM((SUBLANES, LANES), F32),
                        pltpu.VMEM((SUBLANES, LANES), F32)],
        compiler_params=_params(("arbitrary", "arbitrary")),
        name="route_rank",
    )(sel, idx4, tri)


def _dispatch_kernel(dest_ref, hn_ref, xs_in_ref, xs_ref, sem):
    del xs_in_ref
    tt = hn_ref.shape[0]

    def copy(t, d):
        return pltpu.make_async_copy(hn_ref.at[pl.ds(t, 1), :], xs_ref.at[pl.ds(d, 1), :], sem)

    def issue(t, carry):
        for k in range(TOP_K):
            copy(t, dest_ref[TOP_K * t + k]).start()
        return carry

    lax.fori_loop(0, tt, issue, 0)

    def drain(t, carry):
        for k in range(TOP_K):
            copy(t, dest_ref[TOP_K * t + k]).wait()
        return carry

    lax.fori_loop(0, tt, drain, 0)


def _dispatch(dest_flat, hn_packed, n_rows):
    n_tok, half = hn_packed.shape
    tt = _divisor_tile(n_tok, 512)
    xs0 = jnp.zeros((n_rows, half), U32)
    return pl.pallas_call(
        _dispatch_kernel,
        grid=(n_tok // tt,),
        in_specs=[pl.BlockSpec((tt * TOP_K,), lambda i: (i,), memory_space=pltpu.SMEM),
                  pl.BlockSpec((tt, half), lambda i: (i, 0)),
                  pl.BlockSpec(memory_space=pl.ANY)],
        out_specs=pl.BlockSpec(memory_space=pl.ANY),
        out_shape=jax.ShapeDtypeStruct((n_rows, half), U32),
        scratch_shapes=[pltpu.SemaphoreType.DMA(())],
        input_output_aliases={2: 0},
        compiler_params=_params(("arbitrary",)),
        name="dispatch",
    )(dest_flat, hn_packed, xs0)


GROUP = 2 * LANES


def _regroup_kernel(w_ref, p_ref, o_ref):
    p = p_ref[...]
    for g in range(w_ref.shape[1] // GROUP):
        cols = slice(GROUP * g, GROUP * (g + 1))
        o_ref[:, cols] = jnp.dot(w_ref[:, cols].astype(BF16), p, preferred_element_type=F32).astype(BF16)


def _regroup_gate_up(w_gate_up):
    n_exp, d, two_f = w_gate_up.shape
    tr = _divisor_tile(d, 512)
    tc = _divisor_tile(two_f, 1024)
    src = lax.broadcasted_iota(I32, (GROUP, GROUP), 0)
    dst = lax.broadcasted_iota(I32, (GROUP, GROUP), 1)
    perm = (dst == (src // 2) + LANES * (src % 2)).astype(BF16)
    return pl.pallas_call(
        _regroup_kernel,
        grid=(n_exp, d // tr, two_f // tc),
        in_specs=[pl.BlockSpec((None, tr, tc), lambda e, i, j: (e, i, j)),
                  pl.BlockSpec((GROUP, GROUP), lambda e, i, j: (0, 0))],
        out_specs=pl.BlockSpec((None, tr, tc), lambda e, i, j: (e, i, j)),
        out_shape=jax.ShapeDtypeStruct((n_exp, d, two_f), BF16),
        compiler_params=_params(("arbitrary", "arbitrary", "arbitrary")),
        name="regroup_gate_up",
    )(w_gate_up, perm)


def _regroup_bias(b_gate_up):
    n_exp, two_f = b_gate_up.shape
    b = b_gate_up.reshape(n_exp, two_f // GROUP, LANES, 2)
    return jnp.swapaxes(b, 2, 3).reshape(n_exp, 1, two_f).astype(F32)


def _expert_ffn_kernel(be_ref, bv_ref, xs_ref, wgu_ref, wd_ref, bgu_ref, bd_ref,
                       out_ref, x_ref, acc_ref):
    del be_ref
    b = pl.program_id(0)
    f = pl.program_id(1)
    n_f = pl.num_programs(1)
    live = bv_ref[b] > 0

    @pl.when(live & (f == 0))
    def _():
        hi, lo = _unpack_bf16_pairs(xs_ref[...])
        x_ref[...] = jnp.concatenate([hi.astype(BF16), lo.astype(BF16)], axis=1)
        acc_ref[...] = jnp.zeros_like(acc_ref)

    @pl.when(live)
    def _():
        gu = jnp.dot(x_ref[...], wgu_ref[...], preferred_element_type=F32) + bgu_ref[...]
        acts = []
        for g in range(gu.shape[1] // GROUP):
            gate = jnp.minimum(gu[:, GROUP * g:GROUP * g + LANES], SWIGLU_LIMIT)
            up = jnp.clip(gu[:, GROUP * g + LANES:GROUP * (g + 1)], -SWIGLU_LIMIT, SWIGLU_LIMIT)
            acts.append(((up + 1.0) * (gate * jax.nn.sigmoid(SWIGLU_ALPHA * gate))).astype(BF16))
        act = jnp.concatenate(acts, axis=1)
        acc_ref[...] += jnp.dot(act, wd_ref[...], preferred_element_type=F32)

    @pl.when(live & (f == n_f - 1))
    def _():
        out_ref[...] = _pack_bf16_pairs(acc_ref[...] + bd_ref[...])

    @pl.when(jnp.logical_not(live) & (f == n_f - 1))
    def _():
        out_ref[...] = jnp.zeros_like(out_ref)


def _expert_ffn(block_e, block_valid, xs, wgu, wd, bgu, bd, row_block):
    n_rows, half = xs.shape
    d = 2 * half
    d_ff = wd.shape[1]
    tf = _divisor_tile(d_ff, 512)
    n_blocks = n_rows // row_block
    grid_spec = pltpu.PrefetchScalarGridSpec(
        num_scalar_prefetch=2,
        grid=(n_blocks, d_ff // tf),
        in_specs=[
            pl.BlockSpec((row_block, half), lambda b, f, be, bv: (b, 0)),
            pl.BlockSpec((None, d, 2 * tf), lambda b, f, be, bv: (be[b], 0, f)),
            pl.BlockSpec((None, tf, d), lambda b, f, be, bv: (be[b], f, 0)),
            pl.BlockSpec((None, 1, 2 * tf), lambda b, f, be, bv: (be[b], 0, f)),
            pl.BlockSpec((None, 1, d), lambda b, f, be, bv: (be[b], 0, 0)),
        ],
        out_specs=pl.BlockSpec((row_block, half), lambda b, f, be, bv: (b, 0)),
        scratch_shapes=[pltpu.VMEM((row_block, d), BF16), pltpu.VMEM((row_block, d), F32)],
    )
    return pl.pallas_call(
        _expert_ffn_kernel,
        grid_spec=grid_spec,
        out_shape=jax.ShapeDtypeStruct((n_rows, half), U32),
        compiler_params=_params(("arbitrary", "arbitrary")),
        name="expert_ffn",
    )(block_e, block_valid, xs, wgu, wd, bgu, bd)


def _combine_kernel(dest_ref, h_ref, w4_ref, fg_ref, rows_hbm_ref, o_ref, buf_ref, sem):
    tt = h_ref.shape[0]

    def copy(t, k, d):
        return pltpu.make_async_copy(rows_hbm_ref.at[pl.ds(d, 1), :], buf_ref.at[k, pl.ds(t, 1), :], sem)

    def issue(t, carry):
        for k in range(TOP_K):
            copy(t, k, dest_ref[TOP_K * t + k]).start()
        return carry

    lax.fori_loop(0, tt, issue, 0)

    def drain(t, carry):
        for k in range(TOP_K):
            copy(t, k, dest_ref[TOP_K * t + k]).wait()
        return carry

    lax.fori_loop(0, tt, drain, 0)

    w4 = w4_ref[...]
    half = buf_ref.shape[2]
    y_hi = jnp.zeros((tt, half), F32)
    y_lo = jnp.zeros((tt, half), F32)
    for k in range(TOP_K):
        hi, lo = _unpack_bf16_pairs(buf_ref[k])
        wk = w4[:, k:k + 1]
        y_hi = y_hi + wk * hi
        y_lo = y_lo + wk * lo
    h = h_ref[...] + jnp.concatenate([y_hi, y_lo], axis=1)
    o_ref[...] = _rms(h, fg_ref[...])


def _combine(dest_flat, h, w4, final_g, rows_packed):
    n_tok, d = h.shape
    half = d // 2
    tt = _divisor_tile(n_tok, 256)
    return pl.pallas_call(
        _combine_kernel,
        grid=(n_tok // tt,),
        in_specs=[pl.BlockSpec((tt * TOP_K,), lambda i: (i,), memory_space=pltpu.SMEM),
                  pl.BlockSpec((tt, d), lambda i: (i, 0)),
                  pl.BlockSpec((tt, LANES), lambda i: (i, 0)),
                  pl.BlockSpec((1, d), lambda i: (0, 0)),
                  pl.BlockSpec(memory_space=pl.ANY)],
        out_specs=pl.BlockSpec((tt, d), lambda i: (i, 0)),
        out_shape=jax.ShapeDtypeStruct((n_tok, d), F32),
        scratch_shapes=[pltpu.VMEM((TOP_K, tt, half), U32), pltpu.SemaphoreType.DMA(())],
        compiler_params=_params(("arbitrary",)),
        name="combine",
    )(dest_flat, h, w4, final_g.reshape(1, d).astype(F32), rows_packed)


def _layer(h2, batch, seq, lb, norm1_g, w_in, hg_norm_g, gate_w2, gate_b, gla_norm_g, w_proj_a, w_proj_b,
           w_out, norm2_g, router_w, router_b, w_gate_up, b_gate_up, w_down, b_down, out_norm_g):
    n_tok, d = h2.shape
    hf = lb.shape[1]
    hv = w_proj_a.shape[0]
    gk = gate_w2.shape[2]
    gv = w_proj_b.shape[0]
    rank = gate_w2.shape[1]
    hg_heads, gla_heads = hf // HEAD_DK, gk // HEAD_DK
    n_exp = router_w.shape[1]

    (qa, kf, gf, kb, gb, va, oga, qb, kg, vb, ogb, sga, sgb, gg) = _in_proj(
        h2, norm1_g, w_in, lb, gate_w2, gate_b, (hf, hv, gk, gv, rank))
    oa = _recurrence(qa, kf, kb, va, gf, gb, oga, hg_norm_g, batch, seq, hg_heads, hv // hg_heads,
                     name="hgrn_recurrence")
    ob = _recurrence(qb, kg, kg, vb, gg, gg, ogb, gla_norm_g, batch, seq, gla_heads, gv // gla_heads,
                     gb_off=gla_heads, name="gla_recurrence")
    h, hn_packed, w4, idx4, sel = _mix_out(oa, ob, sga, sgb, h2, w_proj_a, w_proj_b, w_out, norm2_g,
                                           router_w, router_b)

    n_pairs = n_tok * TOP_K
    row_block = _divisor_tile(n_pairs, 512)
    n_blocks = n_pairs // row_block + n_exp
    dest4, counts = _route_rank(sel, idx4, row_block)
    dest_flat = dest4[:, :TOP_K].reshape(-1)
    cnt = counts[0, :n_exp].astype(I32)
    blocks_e = (cnt + row_block - 1) // row_block
    end_blk = jnp.cumsum(blocks_e)
    start_blk = end_blk - blocks_e
    bidx = jnp.arange(n_blocks, dtype=I32)[:, None]
    block_e = jnp.minimum(jnp.sum((end_blk[None, :] <= bidx).astype(I32), axis=1), n_exp - 1)
    owned = (bidx >= start_blk[None, :]) & (bidx < end_blk[None, :])
    rows_left = jnp.clip(cnt[None, :] - (bidx - start_blk[None, :]) * row_block, 0, row_block)
    block_valid = jnp.sum(jnp.where(owned, rows_left, 0), axis=1).astype(I32)

    xs = _dispatch(dest_flat, hn_packed, n_blocks * row_block)
    rows = _expert_ffn(block_e, block_valid, xs, _regroup_gate_up(w_gate_up), w_down.astype(BF16),
                       _regroup_bias(b_gate_up), b_down[:, None, :].astype(F32), row_block)
    return _combine(dest_flat, h, w4, out_norm_g, rows)


def kernel(x, norm1_g, w_in, hg_lb_logits, hg_norm_g, gla_gate_w2, gla_gate_b, gla_norm_g, w_proj_a, w_proj_b,
           w_out, norm2_g, router_w, router_b, w_gate_up, b_gate_up, w_down, b_down, final_norm_g):
    batch, seq, d = x.shape
    depth = w_in.shape[0]
    assert depth == 1, "the final RMSNorm is fused into the last layer's combine kernel"
    lb_all = jnp.cumsum(jax.nn.softmax(hg_lb_logits.astype(F32), axis=1), axis=1)
    h2 = x.reshape(batch * seq, d)
    out = _layer(h2, batch, seq, lb_all[:, 0], norm1_g[0], w_in[0], hg_norm_g[0], gla_gate_w2[0], gla_gate_b[0],
                 gla_norm_g[0], w_proj_a[0], w_proj_b[0], w_out[0], norm2_g[0], router_w[0], router_b[0],
                 w_gate_up[0], b_gate_up[0], w_down[0], b_down[0], final_norm_g)
    return out.reshape(batch, seq, d)
```

```python
import functools

import jax
import jax.numpy as jnp
from jax import lax
from jax.experimental import pallas as pl
from jax.experimental.pallas import tpu as pltpu

F32 = jnp.float32
BF16 = jnp.bfloat16
U32 = jnp.uint32
I32 = jnp.int32

LANES = 128
SUBLANES = 8
VMEM_LIMIT_BYTES = 56 * 1024 * 1024

HEAD_DK = 128
GLA_DV = 256
HG_DV = 128
GATE_NORM = 16.0
TOP_K = 4
SWIGLU_LIMIT = 7.0
SWIGLU_ALPHA = 1.702
RMS_EPS = 1e-5
CHUNK = 64
NEG_BIG = -1e30


def _divisor_tile(n, pref):
    t = min(n, pref)
    while n % t:
        t //= 2
    return t


def _params(sem, vmem=VMEM_LIMIT_BYTES):
    return pltpu.CompilerParams(dimension_semantics=sem, vmem_limit_bytes=vmem)


def _pack_bf16_pairs(a):
    n = a.shape[-1] // 2
    hi = lax.bitcast_convert_type(a[:, :n].astype(BF16).astype(F32), U32)
    lo = lax.bitcast_convert_type(a[:, n:].astype(BF16).astype(F32), U32)
    return hi | (lo >> 16)


def _unpack_bf16_pairs(p):
    hi = lax.bitcast_convert_type(p & jnp.uint32(0xFFFF0000), F32)
    lo = lax.bitcast_convert_type(p << 16, F32)
    return hi, lo


def _rms(x, g):
    ms = jnp.mean(x * x, axis=-1, keepdims=True)
    return x * lax.rsqrt(ms + RMS_EPS) * g


def _log_sigmoid(x):
    return jnp.minimum(x, 0.0) - jnp.log1p(jnp.exp(-jnp.abs(x)))


def _in_proj_kernel(bounds, x_ref, g1_ref, w_ref, lb_ref, w2_ref, b2_ref,
                    z_ref, gf_ref, gb_ref, gg_ref, xn_ref):
    j = pl.program_id(1)

    @pl.when(j == 0)
    def _():
        xn_ref[...] = _rms(x_ref[...], g1_ref[...]).astype(BF16)

    scale = HEAD_DK ** -0.5

    def project():
        return jnp.dot(xn_ref[...], w_ref[...], preferred_element_type=F32)

    def silu(z):
        return z * jax.nn.sigmoid(z)

    def store(fn):
        def epilogue():
            z_ref[...] = fn(project()).astype(BF16)
        return epilogue

    def forget(g_ref):
        def epilogue():
            lb = lb_ref[0:1, :]
            f = lb + (1.0 - lb) * jax.nn.sigmoid(project())
            z_ref[...] = (1.0 - f).astype(BF16)
            g_ref[...] = jnp.log(f)
        return epilogue

    def low_rank_gate():
        lr = project()[:, :LANES].astype(BF16)
        logits = jnp.dot(lr, w2_ref[...], preferred_element_type=F32) + b2_ref[...]
        gg_ref[...] = _log_sigmoid(logits) * (1.0 / GATE_NORM)

    epilogues = [
        store(lambda z: silu(z) * scale),
        forget(gf_ref),
        forget(gb_ref),
        store(lambda z: z),
        store(silu),
        store(lambda z: z * scale),
        store(lambda z: z),
        store(lambda z: z),
        store(silu),
        store(jax.nn.sigmoid),
        store(jax.nn.sigmoid),
        low_rank_gate,
    ]
    for k, epilogue in enumerate(epilogues):
        pl.when((j >= bounds[k]) & (j < bounds[k + 1]))(epilogue)


def _in_proj(x2, norm1_g, w_in, lb, gate_w2, gate_b, dims):
    n_tok, d = x2.shape
    hf, hv, gk, gv, rank = dims
    widths = [hf, hf, hf, hv, hv, gk, gk, gv, gv]
    off_lr = sum(widths)
    n_main = off_lr + 2 * d
    tn = _divisor_tile(gk, 256)
    tm = _divisor_tile(n_tok, 1024)
    widths_all = widths + [d, d, tn]
    w_lr = jnp.pad(w_in[:, off_lr:off_lr + 2 * rank], ((0, 0), (0, tn - 2 * rank)))
    w_cat = jnp.concatenate([w_in[:, :off_lr], w_in[:, off_lr + 2 * rank:], w_lr], axis=1).astype(BF16)
    n_col = w_cat.shape[1]
    w_tiles = w_cat.reshape(d, n_col // tn, tn).transpose(1, 0, 2)
    lb_tab = jnp.zeros((SUBLANES, n_col), F32)
    lb_tab = lb_tab.at[0, hf:2 * hf].set(lb[0]).at[0, 2 * hf:3 * hf].set(lb[1])
    w2 = jnp.zeros((LANES, 2 * gk), F32)
    w2 = w2.at[:rank, :gk].set(gate_w2[0]).at[rank:2 * rank, gk:].set(gate_w2[1]).astype(BF16)
    b2 = jnp.concatenate([gate_b[0], gate_b[1]])[None, :].astype(F32)

    bounds = [0]
    for w in widths_all:
        bounds.append(bounds[-1] + w // tn)
    n_j = bounds[-1]

    def seg_spec(k):
        s, n = bounds[k], bounds[k + 1] - bounds[k]
        return pl.BlockSpec((tm, tn), lambda i, j, s=s, n=n: (i, jnp.clip(j - s, 0, n - 1)))

    return pl.pallas_call(
        functools.partial(_in_proj_kernel, tuple(bounds)),
        grid=(n_tok // tm, n_j),
        in_specs=[
            pl.BlockSpec((tm, d), lambda i, j: (i, 0)),
            pl.BlockSpec((1, d), lambda i, j: (0, 0)),
            pl.BlockSpec((None, d, tn), lambda i, j: (j, 0, 0)),
            pl.BlockSpec((SUBLANES, tn), lambda i, j: (0, j)),
            pl.BlockSpec((LANES, 2 * gk), lambda i, j: (0, 0)),
            pl.BlockSpec((1, 2 * gk), lambda i, j: (0, 0)),
        ],
        out_specs=[pl.BlockSpec((tm, tn), lambda i, j: (i, jnp.minimum(j, n_main // tn - 1))),
                   seg_spec(1), seg_spec(2),
                   pl.BlockSpec((tm, 2 * gk), lambda i, j: (i, 0))],
        out_shape=[jax.ShapeDtypeStruct((n_tok, n_main), BF16),
                   jax.ShapeDtypeStruct((n_tok, hf), F32),
                   jax.ShapeDtypeStruct((n_tok, hf), F32),
                   jax.ShapeDtypeStruct((n_tok, 2 * gk), F32)],
        scratch_shapes=[pltpu.VMEM((tm, d), BF16)],
        compiler_params=_params(("arbitrary", "arbitrary")),
        name="in_proj",
    )(x2, norm1_g.reshape(1, d), w_tiles, lb_tab, w2, b2)


_NT_DIMS = (((1,), (1,)), ((), ()))
_TN_DIMS = (((0,), (0,)), ((), ()))


def _chunk_prepare(q, k, g, rev):
    nt = CHUNK // SUBLANES
    row = lax.broadcasted_iota(I32, (SUBLANES, LANES), 0)
    lrow = (SUBLANES - 1 - row) if rev else row

    def phys(jl):
        return nt - 1 - jl if rev else jl

    def prow(x, rl):
        r = SUBLANES - 1 - rl if rev else rl
        return x[r:r + 1, :]

    def tile(x, jl):
        p = phys(jl)
        return x[SUBLANES * p:SUBLANES * (p + 1), :]

    def scan(x):
        for s in (1, 2, 4):
            if rev:
                x = x + jnp.where(row < SUBLANES - s, pltpu.roll(x, SUBLANES - s, 0), 0.0)
            else:
                x = x + jnp.where(row >= s, pltpu.roll(x, s, 0), 0.0)
        return x

    gt = [tile(g, jl) for jl in range(nt)]
    qt = [tile(q, jl) for jl in range(nt)]
    kt = [tile(k, jl) for jl in range(nt)]
    ct = [scan(x) for x in gt]
    tot = [prow(c, SUBLANES - 1) for c in ct]
    suf = [t - c for t, c in zip(tot, ct)]

    def assemble(tiles):
        out = [None] * nt
        for jl, x in enumerate(tiles):
            out[phys(jl)] = jnp.zeros((SUBLANES, LANES), F32) if x is None else x
        return jnp.concatenate(out, axis=0).astype(BF16)

    def plus(a, b):
        return a if b is None else a + b

    levels = []

    for nb in (8, 4, 2):
        half = nb // 2
        qs, ks = [None] * nt, [None] * nt
        for base in range(0, nt, nb):
            acc = None
            for jl in range(base + half, base + nb):
                qs[jl] = qt[jl] * jnp.exp(plus(ct[jl], acc))
                acc = plus(tot[jl], acc)
            acc = None
            for jl in range(base + half - 1, base - 1, -1):
                ks[jl] = kt[jl] * jnp.exp(plus(suf[jl], acc))
                acc = plus(tot[jl], acc)
        levels.append((SUBLANES * nb, assemble(qs), assemble(ks)))

    for size in (8, 4):
        half = size // 2
        is_q = (lrow & (size - 1)) >= half
        qs, ks = [], []
        for jl in range(nt):
            if size == 8:
                ref = prow(ct[jl], half - 1)
            else:
                ref = jnp.where(lrow >= size, prow(ct[jl], size + half - 1), prow(ct[jl], half - 1))
            d = ct[jl] - ref
            e = jnp.exp(jnp.where(is_q, d, -d))
            qs.append(jnp.where(is_q, qt[jl] * e, 0.0))
            ks.append(jnp.where(is_q, 0.0, kt[jl] * e))
        levels.append((size, assemble(qs), assemble(ks)))

    odd = (lrow & 1) == 1
    qs = [jnp.where(odd, qt[jl] * jnp.exp(gt[jl]), 0.0) for jl in range(nt)]
    ks = [jnp.where(odd, 0.0, kt[jl]) for jl in range(nt)]
    levels.append((2, assemble(qs), assemble(ks)))
    levels.append((1, q.astype(BF16), k.astype(BF16)))

    ti = lax.broadcasted_iota(I32, (CHUNK, CHUNK), 0)
    si = lax.broadcasted_iota(I32, (CHUNK, CHUNK), 1)
    blk = ti ^ si
    scores = None
    for size, qm, km in levels:
        s_m = lax.dot_general(qm, km, _NT_DIMS, preferred_element_type=F32)
        if size < CHUNK:
            s_m = jnp.where(blk < size, s_m, 0.0)
        scores = s_m if scores is None else scores + s_m

    q_in, k_out = [None] * nt, [None] * nt
    acc = None
    for jl in range(nt):
        q_in[jl] = qt[jl] * jnp.exp(plus(ct[jl], acc))
        acc = plus(tot[jl], acc)
    total = acc
    acc = None
    for jl in range(nt - 1, -1, -1):
        k_out[jl] = kt[jl] * jnp.exp(plus(suf[jl], acc))
        acc = plus(tot[jl], acc)
    return scores.astype(BF16), assemble(q_in), assemble(k_out), jnp.exp(total)


def _chunk_apply(scores, q_in, k_out, decay, v, st_ref):
    st = st_ref[...]
    out = (jnp.dot(scores, v, preferred_element_type=F32)
           + lax.dot_general(q_in, st.astype(BF16), _NT_DIMS, preferred_element_type=F32))
    st_ref[...] = st * decay + lax.dot_general(v, k_out, _TN_DIMS, preferred_element_type=F32)
    return out


def _recurrence_kernel(n_chunks, q_ref, kf_ref, kb_ref, v_ref, gf_ref, gb_ref, og_ref, ng_ref,
                       o_ref, of_ref, ob_ref, stf_ref, stb_ref, sc_ref, qin_ref, kout_ref, dec_ref):
    dirs = ((False, kf_ref, gf_ref, stf_ref, of_ref), (True, kb_ref, gb_ref, stb_ref, ob_ref))
    stf_ref[...] = jnp.zeros_like(stf_ref)
    stb_ref[...] = jnp.zeros_like(stb_ref)

    def rows_of(c, rev):
        cc = (n_chunks - 1 - c) if rev else c
        return pl.ds(pl.multiple_of(cc * CHUNK, CHUNK), CHUNK)

    def prepare(c):
        res = []
        for rev, k_ref, g_ref, _, _ in dirs:
            rows = rows_of(c, rev)
            res.append(_chunk_prepare(q_ref[rows, :].astype(F32), k_ref[rows, :].astype(F32),
                                      g_ref[rows, :], rev))
        return res

    def stash(res):
        for d, (scores, q_in, k_out, decay) in enumerate(res):
            sc_ref[d] = scores
            qin_ref[d] = q_in
            kout_ref[d] = k_out
            dec_ref[d] = jnp.broadcast_to(decay, (SUBLANES, LANES))

    def unstash():
        return [(sc_ref[d], qin_ref[d], kout_ref[d], dec_ref[d][0:1, :]) for d in range(2)]

    def apply(c, staged):
        for (rev, _, _, st_ref, out_ref), (scores, q_in, k_out, decay) in zip(dirs, staged):
            rows = rows_of(c, rev)
            out_ref[rows, :] = _chunk_apply(scores, q_in, k_out, decay, v_ref[rows, :], st_ref)

    stash(prepare(0))

    def body(c, carry):
        staged = unstash()
        nxt = prepare(c + 1)
        apply(c, staged)
        stash(nxt)
        return carry

    lax.fori_loop(0, n_chunks - 1, body, 0)
    apply(n_chunks - 1, unstash())
    o = of_ref[...] + ob_ref[...]
    o_ref[...] = (_rms(o, ng_ref[...]) * og_ref[...].astype(F32)).astype(BF16)


def _recurrence(z, gf, gb, norm_g, batch, seq, heads, dv, cols, gb_off=0, name=""):
    n_tok = batch * seq
    n_chunks = seq // CHUNK
    q_col, kf_col, kb_col, v_col, og_col = cols

    def spec(width, col=0):
        assert col % width == 0
        return pl.BlockSpec((seq, width), lambda b, h, off=col // width: (b, h + off))

    return pl.pallas_call(
        functools.partial(_recurrence_kernel, n_chunks),
        grid=(batch, heads),
        in_specs=[spec(HEAD_DK, q_col), spec(HEAD_DK, kf_col), spec(HEAD_DK, kb_col), spec(dv, v_col),
                  spec(HEAD_DK), spec(HEAD_DK, gb_off * HEAD_DK), spec(dv, og_col),
                  pl.BlockSpec((1, dv), lambda b, h: (0, 0))],
        out_specs=spec(dv),
        out_shape=jax.ShapeDtypeStruct((n_tok, heads * dv), BF16),
        scratch_shapes=[pltpu.VMEM((seq, dv), F32), pltpu.VMEM((seq, dv), F32),
                        pltpu.VMEM((dv, HEAD_DK), F32), pltpu.VMEM((dv, HEAD_DK), F32),
                        pltpu.VMEM((2, CHUNK, CHUNK), BF16), pltpu.VMEM((2, CHUNK, HEAD_DK), BF16),
                        pltpu.VMEM((2, CHUNK, HEAD_DK), BF16), pltpu.VMEM((2, SUBLANES, LANES), F32)],
        compiler_params=_params(("arbitrary", "arbitrary")),
        name=name,
    )(z, z, z, z, gf, gb, z, norm_g.reshape(1, dv).astype(F32))


MIX_SUB_ROWS = 256


def _mix_out_kernel(oa_ref, ob_ref, sga_ref, sgb_ref, x_ref, wpa_ref, wpb_ref, wo_ref, g2_ref, rw2_ref, rw1_ref,
                    rb_ref, h_ref, hn_ref, w4_ref, idx_ref, sel_ref):
    tm = x_ref.shape[0]
    sub = min(tm, MIX_SUB_ROWS)
    for r in range(tm // sub):
        _mix_out_rows(pl.ds(r * sub, sub), oa_ref, ob_ref, sga_ref, sgb_ref, x_ref, wpa_ref, wpb_ref, wo_ref,
                      g2_ref, rw2_ref, rw1_ref, rb_ref, h_ref, hn_ref, w4_ref, idx_ref, sel_ref)


def _mix_out_rows(rows, oa_ref, ob_ref, sga_ref, sgb_ref, x_ref, wpa_ref, wpb_ref, wo_ref, g2_ref, rw2_ref,
                  rw1_ref, rb_ref, h_ref, hn_ref, w4_ref, idx_ref, sel_ref):
    ya = jnp.dot(oa_ref[rows, :], wpa_ref[...], preferred_element_type=F32)
    yb = jnp.dot(ob_ref[rows, :], wpb_ref[...], preferred_element_type=F32)
    merged = sga_ref[rows, :].astype(F32) * ya + sgb_ref[rows, :].astype(F32) * yb
    h = x_ref[rows, :] + jnp.dot(merged.astype(BF16), wo_ref[...], preferred_element_type=F32)
    h_ref[rows, :] = h
    hn = _rms(h, g2_ref[...])
    hn_ref[rows, :] = _pack_bf16_pairs(hn)

    hn_hi = hn.astype(BF16)
    hn_lo = (hn - hn_hi.astype(F32)).astype(BF16)
    two = jnp.dot(hn_hi, rw2_ref[...], preferred_element_type=F32)
    logits = (two[:, :LANES] + two[:, LANES:] + jnp.dot(hn_lo, rw1_ref[...], preferred_element_type=F32)
              + rb_ref[...])
    lane = lax.broadcasted_iota(I32, logits.shape, 1)
    lane_f = lane.astype(F32)
    work = logits
    vals, idxs = [], []
    for _ in range(TOP_K):
        m = jnp.max(work, axis=-1, keepdims=True)
        idx = jnp.min(jnp.where(work == m, lane_f, float(LANES)), axis=-1, keepdims=True)
        vals.append(m)
        idxs.append(idx)
        work = jnp.where(lane_f == idx, -jnp.inf, work)
    es = [jnp.exp(v - vals[0]) for v in vals]
    denom = es[0]
    for e in es[1:]:
        denom = denom + e
    w4 = jnp.zeros(logits.shape, F32)
    i4 = jnp.zeros(logits.shape, F32)
    sel = jnp.zeros(logits.shape, F32)
    for k in range(TOP_K):
        w4 = jnp.where(lane == k, es[k] / denom, w4)
        i4 = jnp.where(lane == k, idxs[k], i4)
        sel = jnp.where(lane_f == idxs[k], 1.0, sel)
    w4_ref[rows, :] = w4
    idx_ref[rows, :] = i4.astype(I32)
    sel_ref[rows, :] = sel.astype(BF16)


def _mix_out(oa, ob, z, gate_col, x2, w_proj_a, w_proj_b, w_out, norm2_g, router_w, router_b):
    n_tok, d = x2.shape
    hv, gv = oa.shape[1], ob.shape[1]
    n_exp = router_w.shape[1]
    tm = _divisor_tile(n_tok, 2 * MIX_SUB_ROWS)
    rw = jnp.pad(router_w.astype(F32), ((0, 0), (0, LANES - n_exp)))
    rw_hi = rw.astype(BF16)
    rw_lo = (rw - rw_hi.astype(F32)).astype(BF16)
    rb = jnp.concatenate([router_b.astype(F32), jnp.full((LANES - n_exp,), NEG_BIG, F32)])[None, :]
    assert gate_col % d == 0
    gate_blk = gate_col // d

    def rows(width, off=0):
        return pl.BlockSpec((tm, width), lambda i, off=off: (i, off))

    def whole(r, c):
        return pl.BlockSpec((r, c), lambda i: (0, 0), pipeline_mode=pl.Buffered(1))

    return pl.pallas_call(
        _mix_out_kernel,
        grid=(n_tok // tm,),
        in_specs=[rows(hv), rows(gv), rows(d, gate_blk), rows(d, gate_blk + 1), rows(d),
                  whole(hv, d), whole(gv, d), whole(d, d), whole(1, d), whole(d, 2 * LANES), whole(d, LANES),
                  whole(1, LANES)],
        out_specs=[rows(d), rows(d // 2), rows(LANES), rows(LANES), rows(LANES)],
        out_shape=[jax.ShapeDtypeStruct((n_tok, d), F32),
                   jax.ShapeDtypeStruct((n_tok, d // 2), U32),
                   jax.ShapeDtypeStruct((n_tok, LANES), F32),
                   jax.ShapeDtypeStruct((n_tok, LANES), I32),
                   jax.ShapeDtypeStruct((n_tok, LANES), BF16)],
        compiler_params=_params(("arbitrary",)),
        name="mix_out",
    )(oa, ob, z, z, x2, w_proj_a.astype(BF16), w_proj_b.astype(BF16), w_out.astype(BF16),
      norm2_g.reshape(1, d).astype(F32), jnp.concatenate([rw_hi, rw_lo], axis=1), rw_hi, rb)


def _route_rank_kernel(row_block, sel_ref, idx_ref, tri_ref, dest_ref, cnt_ref, rank_ref, carry_ref, start_ref):
    p = pl.program_id(0)
    i = pl.program_id(1)
    tb = sel_ref.shape[0]
    rows = pl.ds(pl.multiple_of(i * tb, tb), tb)

    @pl.when((p == 0) & (i == 0))
    def _():
        carry_ref[...] = jnp.zeros_like(carry_ref)

    @pl.when(p == 0)
    def _():
        sel = sel_ref[...]
        before = jnp.dot(tri_ref[...], sel, preferred_element_type=F32)
        rank_ref[rows, :] = before + carry_ref[0:1, :]
        carry_ref[...] = carry_ref[...] + jnp.sum(sel.astype(F32), axis=0, keepdims=True)

    @pl.when((p == 1) & (i == 0))
    def _():
        counts = carry_ref[...]
        padded = jnp.ceil(counts * (1.0 / row_block)) * row_block
        lane = lax.broadcasted_iota(I32, padded.shape, 1)
        run = padded
        s = 1
        while s < LANES:
            run = run + jnp.where(lane >= s, pltpu.roll(run, s, 1), 0.0)
            s *= 2
        start_ref[...] = run - padded

    @pl.when(p == 1)
    def _():
        pos = rank_ref[rows, :] + start_ref[0:1, :]
        idx = idx_ref[...]
        lane = lax.broadcasted_iota(I32, pos.shape, 1)
        dest = jnp.zeros(pos.shape, F32)
        for k in range(TOP_K):
            hit = lane == idx[:, k:k + 1]
            val = jnp.sum(jnp.where(hit, pos, 0.0), axis=-1, keepdims=True)
            dest = jnp.where(lane == k, val, dest)
        dest_ref[...] = dest.astype(I32)
        cnt_ref[...] = carry_ref[...]


def _route_rank(sel, idx4, row_block):
    n_tok = sel.shape[0]
    tb = _divisor_tile(n_tok, 256)
    tri = (lax.broadcasted_iota(I32, (tb, tb), 0) > lax.broadcasted_iota(I32, (tb, tb), 1)).astype(BF16)
    return pl.pallas_call(
        functools.partial(_route_rank_kernel, row_block),
        grid=(2, n_tok // tb),
        in_specs=[pl.BlockSpec((tb, LANES), lambda p, i: (i, 0)),
                  pl.BlockSpec((tb, LANES), lambda p, i: (i, 0)),
                  pl.BlockSpec((tb, tb), lambda p, i: (0, 0))],
        out_specs=[pl.BlockSpec((tb, LANES), lambda p, i: (i * p, 0)),
                   pl.BlockSpec((SUBLANES, LANES), lambda p, i: (0, 0))],
        out_shape=[jax.ShapeDtypeStruct((n_tok, LANES), I32),
                   jax.ShapeDtypeStruct((SUBLANES, LANES), F32)],
        scratch_shapes=[pltpu.VMEM((n_tok, LANES), F32), pltpu.VMEM((SUBLANES, LANES), F32),
                        pltpu.VMEM((SUBLANES, LANES), F32)],
        compiler_params=_params(("arbitrary", "arbitrary")),
        name="route_rank",
    )(sel, idx4, tri)


def _dispatch_kernel(dest_ref, hn_ref, xs_in_ref, xs_ref, sem):
    del xs_in_ref
    tt = hn_ref.shape[0]

    def copy(t, d):
        return pltpu.make_async_copy(hn_ref.at[pl.ds(t, 1), :], xs_ref.at[pl.ds(d, 1), :], sem)

    def issue(t, carry):
        for k in range(TOP_K):
            copy(t, dest_ref[TOP_K * t + k]).start()
        return carry

    lax.fori_loop(0, tt, issue, 0)

    def drain(t, carry):
        for k in range(TOP_K):
            copy(t, dest_ref[TOP_K * t + k]).wait()
        return carry

    lax.fori_loop(0, tt, drain, 0)


def _dispatch(dest_flat, hn_packed, n_rows):
    n_tok, half = hn_packed.shape
    tt = _divisor_tile(n_tok, 512)
    xs0 = jnp.zeros((n_rows, half), U32)
    return pl.pallas_call(
        _dispatch_kernel,
        grid=(n_tok // tt,),
        in_specs=[pl.BlockSpec((tt * TOP_K,), lambda i: (i,), memory_space=pltpu.SMEM),
                  pl.BlockSpec((tt, half), lambda i: (i, 0)),
                  pl.BlockSpec(memory_space=pl.ANY)],
        out_specs=pl.BlockSpec(memory_space=pl.ANY),
        out_shape=jax.ShapeDtypeStruct((n_rows, half), U32),
        scratch_shapes=[pltpu.SemaphoreType.DMA(())],
        input_output_aliases={2: 0},
        compiler_params=_params(("arbitrary",)),
        name="dispatch",
    )(dest_flat, hn_packed, xs0)


GROUP = 2 * LANES


def _regroup_kernel(w_ref, p_ref, o_ref):
    p = p_ref[...]
    for g in range(w_ref.shape[1] // GROUP):
        cols = slice(GROUP * g, GROUP * (g + 1))
        o_ref[:, cols] = jnp.dot(w_ref[:, cols].astype(BF16), p, preferred_element_type=F32).astype(BF16)


def _regroup_gate_up(w_gate_up):
    n_exp, d, two_f = w_gate_up.shape
    tr = _divisor_tile(d, 1024)
    tc = _divisor_tile(two_f, 2048)
    src = lax.broadcasted_iota(I32, (GROUP, GROUP), 0)
    dst = lax.broadcasted_iota(I32, (GROUP, GROUP), 1)
    perm = (dst == (src // 2) + LANES * (src % 2)).astype(BF16)
    return pl.pallas_call(
        _regroup_kernel,
        grid=(n_exp, d // tr, two_f // tc),
        in_specs=[pl.BlockSpec((None, tr, tc), lambda e, i, j: (e, i, j)),
                  pl.BlockSpec((GROUP, GROUP), lambda e, i, j: (0, 0))],
        out_specs=pl.BlockSpec((None, tr, tc), lambda e, i, j: (e, i, j)),
        out_shape=jax.ShapeDtypeStruct((n_exp, d, two_f), BF16),
        compiler_params=_params(("arbitrary", "arbitrary", "arbitrary")),
        name="regroup_gate_up",
    )(w_gate_up, perm)


def _regroup_bias(b_gate_up):
    n_exp, two_f = b_gate_up.shape
    b = b_gate_up.reshape(n_exp, two_f // GROUP, LANES, 2)
    return jnp.swapaxes(b, 2, 3).reshape(n_exp, 1, two_f).astype(F32)


FFN_ROW_BLOCK = 1024
FFN_SUB_ROWS = 256


def _expert_ffn_kernel(be_ref, bv_ref, xs_ref, wgu_ref, wd_ref, bgu_ref, bd_ref,
                       out_ref, x_ref, acc_ref):
    del be_ref
    b = pl.program_id(0)
    f = pl.program_id(1)
    n_f = pl.num_programs(1)
    n_rows = xs_ref.shape[0]
    sub = min(n_rows, FFN_SUB_ROWS)
    valid = bv_ref[b]
    n_sub = (valid + sub - 1) // sub

    @pl.when((valid > 0) & (f == 0))
    def _():
        hi, lo = _unpack_bf16_pairs(xs_ref[...])
        x_ref[...] = jnp.concatenate([hi.astype(BF16), lo.astype(BF16)], axis=1)
        acc_ref[...] = jnp.zeros_like(acc_ref)

    def ffn(rows):
        gu = jnp.dot(x_ref[rows, :], wgu_ref[...], preferred_element_type=F32) + bgu_ref[...]
        acts = []
        for g in range(gu.shape[1] // GROUP):
            gate = jnp.minimum(gu[:, GROUP * g:GROUP * g + LANES], SWIGLU_LIMIT)
            up = jnp.clip(gu[:, GROUP * g + LANES:GROUP * (g + 1)], -SWIGLU_LIMIT, SWIGLU_LIMIT)
            acts.append(((up + 1.0) * (gate * jax.nn.sigmoid(SWIGLU_ALPHA * gate))).astype(BF16))
        act = jnp.concatenate(acts, axis=1)
        acc_ref[rows, :] += jnp.dot(act, wd_ref[...], preferred_element_type=F32)

    @pl.when(n_sub == n_rows // sub)
    def _():
        ffn(pl.ds(0, n_rows))

    @pl.when(n_sub < n_rows // sub)
    def _():
        def body(s, carry):
            ffn(pl.ds(pl.multiple_of(s * sub, sub), sub))
            return carry
        lax.fori_loop(0, n_sub, body, 0)

    @pl.when((valid > 0) & (f == n_f - 1))
    def _():
        out_ref[...] = _pack_bf16_pairs(acc_ref[...] + bd_ref[...])

    @pl.when((valid == 0) & (f == n_f - 1))
    def _():
        out_ref[...] = jnp.zeros_like(out_ref)


def _expert_ffn(block_e, block_valid, xs, wgu, wd, bgu, bd, row_block):
    n_rows, half = xs.shape
    d = 2 * half
    d_ff = wd.shape[1]
    tf = _divisor_tile(d_ff, 512)
    n_blocks = n_rows // row_block
    grid_spec = pltpu.PrefetchScalarGridSpec(
        num_scalar_prefetch=2,
        grid=(n_blocks, d_ff // tf),
        in_specs=[
            pl.BlockSpec((row_block, half), lambda b, f, be, bv: (b, 0)),
            pl.BlockSpec((None, d, 2 * tf), lambda b, f, be, bv: (be[b], 0, f)),
            pl.BlockSpec((None, tf, d), lambda b, f, be, bv: (be[b], f, 0)),
            pl.BlockSpec((None, 1, 2 * tf), lambda b, f, be, bv: (be[b], 0, f)),
            pl.BlockSpec((None, 1, d), lambda b, f, be, bv: (be[b], 0, 0)),
        ],
        out_specs=pl.BlockSpec((row_block, half), lambda b, f, be, bv: (b, 0)),
        scratch_shapes=[pltpu.VMEM((row_block, d), BF16), pltpu.VMEM((row_block, d), F32)],
    )
    return pl.pallas_call(
        _expert_ffn_kernel,
        grid_spec=grid_spec,
        out_shape=jax.ShapeDtypeStruct((n_rows, half), U32),
        compiler_params=_params(("arbitrary", "arbitrary")),
        name="expert_ffn",
    )(block_e, block_valid, xs, wgu, wd, bgu, bd)


def _combine_kernel(dest_ref, h_ref, w4_ref, fg_ref, rows_hbm_ref, o_ref, buf_ref, sem):
    tt = h_ref.shape[0]

    def copy(t, k, d):
        return pltpu.make_async_copy(rows_hbm_ref.at[pl.ds(d, 1), :], buf_ref.at[k, pl.ds(t, 1), :], sem)

    def issue(t, carry):
        for k in range(TOP_K):
            copy(t, k, dest_ref[TOP_K * t + k]).start()
        return carry

    lax.fori_loop(0, tt, issue, 0)

    def drain(t, carry):
        for k in range(TOP_K):
            copy(t, k, dest_ref[TOP_K * t + k]).wait()
        return carry

    lax.fori_loop(0, tt, drain, 0)

    w4 = w4_ref[...]
    half = buf_ref.shape[2]
    y_hi = jnp.zeros((tt, half), F32)
    y_lo = jnp.zeros((tt, half), F32)
    for k in range(TOP_K):
        hi, lo = _unpack_bf16_pairs(buf_ref[k])
        wk = w4[:, k:k + 1]
        y_hi = y_hi + wk * hi
        y_lo = y_lo + wk * lo
    h = h_ref[...] + jnp.concatenate([y_hi, y_lo], axis=1)
    o_ref[...] = _rms(h, fg_ref[...])


def _combine(dest_flat, h, w4, final_g, rows_packed):
    n_tok, d = h.shape
    half = d // 2
    tt = _divisor_tile(n_tok, 256)
    return pl.pallas_call(
        _combine_kernel,
        grid=(n_tok // tt,),
        in_specs=[pl.BlockSpec((tt * TOP_K,), lambda i: (i,), memory_space=pltpu.SMEM),
                  pl.BlockSpec((tt, d), lambda i: (i, 0)),
                  pl.BlockSpec((tt, LANES), lambda i: (i, 0)),
                  pl.BlockSpec((1, d), lambda i: (0, 0)),
                  pl.BlockSpec(memory_space=pl.ANY)],
        out_specs=pl.BlockSpec((tt, d), lambda i: (i, 0)),
        out_shape=jax.ShapeDtypeStruct((n_tok, d), F32),
        scratch_shapes=[pltpu.VMEM((TOP_K, tt, half), U32), pltpu.SemaphoreType.DMA(())],
        compiler_params=_params(("arbitrary",)),
        name="combine",
    )(dest_flat, h, w4, final_g.reshape(1, d).astype(F32), rows_packed)


def _layer(h2, batch, seq, lb, norm1_g, w_in, hg_norm_g, gate_w2, gate_b, gla_norm_g, w_proj_a, w_proj_b,
           w_out, norm2_g, router_w, router_b, w_gate_up, b_gate_up, w_down, b_down, out_norm_g):
    n_tok, d = h2.shape
    hf = lb.shape[1]
    hv = w_proj_a.shape[0]
    gk = gate_w2.shape[2]
    gv = w_proj_b.shape[0]
    rank = gate_w2.shape[1]
    hg_heads, gla_heads = hf // HEAD_DK, gk // HEAD_DK
    n_exp = router_w.shape[1]

    z, gf, gb, gg = _in_proj(h2, norm1_g, w_in, lb, gate_w2, gate_b, (hf, hv, gk, gv, rank))
    a_cols = (0, hf, 2 * hf, 3 * hf, 3 * hf + hv)
    b0 = 3 * hf + 2 * hv
    b_cols = (b0, b0 + gk, b0 + gk, b0 + 2 * gk, b0 + 2 * gk + gv)
    oa = _recurrence(z, gf, gb, hg_norm_g, batch, seq, hg_heads, hv // hg_heads, a_cols,
                     name="hgrn_recurrence")
    ob = _recurrence(z, gg, gg, gla_norm_g, batch, seq, gla_heads, gv // gla_heads, b_cols,
                     gb_off=gla_heads, name="gla_recurrence")
    h, hn_packed, w4, idx4, sel = _mix_out(oa, ob, z, b0 + 2 * gk + 2 * gv, h2, w_proj_a, w_proj_b, w_out,
                                           norm2_g, router_w, router_b)

    n_pairs = n_tok * TOP_K
    row_block = _divisor_tile(n_pairs, FFN_ROW_BLOCK)
    n_blocks = n_pairs // row_block + n_exp
    dest4, counts = _route_rank(sel, idx4, row_block)
    dest_flat = dest4[:, :TOP_K].reshape(-1)
    cnt = counts[0, :n_exp].astype(I32)
    blocks_e = (cnt + row_block - 1) // row_block
    end_blk = jnp.cumsum(blocks_e)
    start_blk = end_blk - blocks_e
    bidx = jnp.arange(n_blocks, dtype=I32)[:, None]
    block_e = jnp.minimum(jnp.sum((end_blk[None, :] <= bidx).astype(I32), axis=1), n_exp - 1)
    owned = (bidx >= start_blk[None, :]) & (bidx < end_blk[None, :])
    rows_left = jnp.clip(cnt[None, :] - (bidx - start_blk[None, :]) * row_block, 0, row_block)
    block_valid = jnp.sum(jnp.where(owned, rows_left, 0), axis=1).astype(I32)

    xs = _dispatch(dest_flat, hn_packed, n_blocks * row_block)
    rows = _expert_ffn(block_e, block_valid, xs, _regroup_gate_up(w_gate_up), w_down.astype(BF16),
                       _regroup_bias(b_gate_up), b_down[:, None, :].astype(F32), row_block)
    return _combine(dest_flat, h, w4, out_norm_g, rows)


def kernel(x, norm1_g, w_in, hg_lb_logits, hg_norm_g, gla_gate_w2, gla_gate_b, gla_norm_g, w_proj_a, w_proj_b,
           w_out, norm2_g, router_w, router_b, w_gate_up, b_gate_up, w_down, b_down, final_norm_g):
    batch, seq, d = x.shape
    depth = w_in.shape[0]
    assert depth == 1, "the final RMSNorm is fused into the last layer's combine kernel"
    lb_all = jnp.cumsum(jax.nn.softmax(hg_lb_logits.astype(F32), axis=1), axis=1)
    h2 = x.reshape(batch * seq, d)
    out = _layer(h2, batch, seq, lb_all[:, 0], norm1_g[0], w_in[0], hg_norm_g[0], gla_gate_w2[0], gla_gate_b[0],
                 gla_norm_g[0], w_proj_a[0], w_proj_b[0], w_out[0], norm2_g[0], router_w[0], router_b[0],
                 w_gate_up[0], b_gate_up[0], w_down[0], b_down[0], final_norm_g)
    return out.reshape(batch, seq, d)
```

```python
import functools

import jax
import jax.numpy as jnp
from jax import lax
from jax.experimental import pallas as pl
from jax.experimental.pallas import tpu as pltpu

F32 = jnp.float32
BF16 = jnp.bfloat16
U32 = jnp.uint32
I32 = jnp.int32

LANES = 128
SUBLANES = 8
VMEM_LIMIT_BYTES = 56 * 1024 * 1024

HEAD_DK = 128
GLA_DV = 256
HG_DV = 128
GATE_NORM = 16.0
TOP_K = 4
SWIGLU_LIMIT = 7.0
SWIGLU_ALPHA = 1.702
RMS_EPS = 1e-5
CHUNK = 64
NEG_BIG = -1e30


def _divisor_tile(n, pref):
    t = min(n, pref)
    while n % t:
        t //= 2
    return t


def _params(sem, vmem=VMEM_LIMIT_BYTES):
    return pltpu.CompilerParams(dimension_semantics=sem, vmem_limit_bytes=vmem)


def _pack_bf16_pairs(a):
    n = a.shape[-1] // 2
    hi = lax.bitcast_convert_type(a[:, :n].astype(BF16).astype(F32), U32)
    lo = lax.bitcast_convert_type(a[:, n:].astype(BF16).astype(F32), U32)
    return hi | (lo >> 16)


def _unpack_bf16_pairs(p):
    hi = lax.bitcast_convert_type(p & jnp.uint32(0xFFFF0000), F32)
    lo = lax.bitcast_convert_type(p << 16, F32)
    return hi, lo


def _rms(x, g):
    ms = jnp.mean(x * x, axis=-1, keepdims=True)
    return x * lax.rsqrt(ms + RMS_EPS) * g


def _log_sigmoid(x):
    return jnp.minimum(x, 0.0) - jnp.log1p(jnp.exp(-jnp.abs(x)))


def _in_proj_kernel(bounds, x_ref, g1_ref, w_ref, lb_ref, w2_ref, b2_ref,
                    z_ref, gf_ref, gb_ref, gg_ref, xn_ref):
    j = pl.program_id(1)

    @pl.when(j == 0)
    def _():
        xn_ref[...] = _rms(x_ref[...], g1_ref[...]).astype(BF16)

    scale = HEAD_DK ** -0.5

    def project():
        return jnp.dot(xn_ref[...], w_ref[...], preferred_element_type=F32)

    def silu(z):
        return z * jax.nn.sigmoid(z)

    def store(fn):
        def epilogue():
            z_ref[...] = fn(project()).astype(BF16)
        return epilogue

    def forget(g_ref):
        def epilogue():
            lb = lb_ref[0:1, :]
            f = lb + (1.0 - lb) * jax.nn.sigmoid(project())
            z_ref[...] = (1.0 - f).astype(BF16)
            g_ref[...] = jnp.log(f)
        return epilogue

    def low_rank_gate():
        lr = project()[:, :LANES].astype(BF16)
        logits = jnp.dot(lr, w2_ref[...], preferred_element_type=F32) + b2_ref[...]
        gg_ref[...] = _log_sigmoid(logits) * (1.0 / GATE_NORM)

    epilogues = [
        store(lambda z: silu(z) * scale),
        forget(gf_ref),
        forget(gb_ref),
        store(lambda z: z),
        store(silu),
        store(lambda z: z * scale),
        store(lambda z: z),
        store(lambda z: z),
        store(silu),
        store(jax.nn.sigmoid),
        store(jax.nn.sigmoid),
        low_rank_gate,
    ]
    for k, epilogue in enumerate(epilogues):
        pl.when((j >= bounds[k]) & (j < bounds[k + 1]))(epilogue)


def _in_proj(x2, norm1_g, w_in, lb, gate_w2, gate_b, dims):
    n_tok, d = x2.shape
    hf, hv, gk, gv, rank = dims
    widths = [hf, hf, hf, hv, hv, gk, gk, gv, gv]
    off_lr = sum(widths)
    n_main = off_lr + 2 * d
    tn = _divisor_tile(gk, 256)
    tm = _divisor_tile(n_tok, 1024)
    widths_all = widths + [d, d, tn]
    w_lr = jnp.pad(w_in[:, off_lr:off_lr + 2 * rank], ((0, 0), (0, tn - 2 * rank)))
    w_cat = jnp.concatenate([w_in[:, :off_lr], w_in[:, off_lr + 2 * rank:], w_lr], axis=1).astype(BF16)
    n_col = w_cat.shape[1]
    w_tiles = w_cat.reshape(d, n_col // tn, tn).transpose(1, 0, 2)
    lb_tab = jnp.zeros((SUBLANES, n_col), F32)
    lb_tab = lb_tab.at[0, hf:2 * hf].set(lb[0]).at[0, 2 * hf:3 * hf].set(lb[1])
    w2 = jnp.zeros((LANES, 2 * gk), F32)
    w2 = w2.at[:rank, :gk].set(gate_w2[0]).at[rank:2 * rank, gk:].set(gate_w2[1]).astype(BF16)
    b2 = jnp.concatenate([gate_b[0], gate_b[1]])[None, :].astype(F32)

    bounds = [0]
    for w in widths_all:
        bounds.append(bounds[-1] + w // tn)
    n_j = bounds[-1]

    def seg_spec(k):
        s, n = bounds[k], bounds[k + 1] - bounds[k]
        return pl.BlockSpec((tm, tn), lambda i, j, s=s, n=n: (i, jnp.clip(j - s, 0, n - 1)))

    return pl.pallas_call(
        functools.partial(_in_proj_kernel, tuple(bounds)),
        grid=(n_tok // tm, n_j),
        in_specs=[
            pl.BlockSpec((tm, d), lambda i, j: (i, 0)),
            pl.BlockSpec((1, d), lambda i, j: (0, 0)),
            pl.BlockSpec((None, d, tn), lambda i, j: (j, 0, 0)),
            pl.BlockSpec((SUBLANES, tn), lambda i, j: (0, j)),
            pl.BlockSpec((LANES, 2 * gk), lambda i, j: (0, 0)),
            pl.BlockSpec((1, 2 * gk), lambda i, j: (0, 0)),
        ],
        out_specs=[pl.BlockSpec((tm, tn), lambda i, j: (i, jnp.minimum(j, n_main // tn - 1))),
                   seg_spec(1), seg_spec(2),
                   pl.BlockSpec((tm, 2 * gk), lambda i, j: (i, 0))],
        out_shape=[jax.ShapeDtypeStruct((n_tok, n_main), BF16),
                   jax.ShapeDtypeStruct((n_tok, hf), F32),
                   jax.ShapeDtypeStruct((n_tok, hf), F32),
                   jax.ShapeDtypeStruct((n_tok, 2 * gk), F32)],
        scratch_shapes=[pltpu.VMEM((tm, d), BF16)],
        compiler_params=_params(("arbitrary", "arbitrary")),
        name="in_proj",
    )(x2, norm1_g.reshape(1, d), w_tiles, lb_tab, w2, b2)


_NT_DIMS = (((1,), (1,)), ((), ()))
_TN_DIMS = (((0,), (0,)), ((), ()))


def _chunk_prepare(q, k, g, rev):
    nt = CHUNK // SUBLANES
    row = lax.broadcasted_iota(I32, (SUBLANES, LANES), 0)
    lrow = (SUBLANES - 1 - row) if rev else row

    def phys(jl):
        return nt - 1 - jl if rev else jl

    def prow(x, rl):
        r = SUBLANES - 1 - rl if rev else rl
        return x[r:r + 1, :]

    def tile(x, jl):
        p = phys(jl)
        return x[SUBLANES * p:SUBLANES * (p + 1), :]

    def scan(x):
        for s in (1, 2, 4):
            if rev:
                x = x + jnp.where(row < SUBLANES - s, pltpu.roll(x, SUBLANES - s, 0), 0.0)
            else:
                x = x + jnp.where(row >= s, pltpu.roll(x, s, 0), 0.0)
        return x

    gt = [tile(g, jl) for jl in range(nt)]
    qt = [tile(q, jl) for jl in range(nt)]
    kt = [tile(k, jl) for jl in range(nt)]
    ct = [scan(x) for x in gt]
    tot = [prow(c, SUBLANES - 1) for c in ct]
    suf = [t - c for t, c in zip(tot, ct)]

    def assemble(tiles):
        out = [None] * nt
        for jl, x in enumerate(tiles):
            out[phys(jl)] = jnp.zeros((SUBLANES, LANES), F32) if x is None else x
        return jnp.concatenate(out, axis=0).astype(BF16)

    def plus(a, b):
        return a if b is None else a + b

    levels = []

    for nb in (8, 4, 2):
        half = nb // 2
        qs, ks = [None] * nt, [None] * nt
        for base in range(0, nt, nb):
            acc = None
            for jl in range(base + half, base + nb):
                qs[jl] = qt[jl] * jnp.exp(plus(ct[jl], acc))
                acc = plus(tot[jl], acc)
            acc = None
            for jl in range(base + half - 1, base - 1, -1):
                ks[jl] = kt[jl] * jnp.exp(plus(suf[jl], acc))
                acc = plus(tot[jl], acc)
        levels.append((SUBLANES * nb, assemble(qs), assemble(ks)))

    for size in (8, 4):
        half = size // 2
        is_q = (lrow & (size - 1)) >= half
        qs, ks = [], []
        for jl in range(nt):
            if size == 8:
                ref = prow(ct[jl], half - 1)
            else:
                ref = jnp.where(lrow >= size, prow(ct[jl], size + half - 1), prow(ct[jl], half - 1))
            d = ct[jl] - ref
            e = jnp.exp(jnp.where(is_q, d, -d))
            qs.append(jnp.where(is_q, qt[jl] * e, 0.0))
            ks.append(jnp.where(is_q, 0.0, kt[jl] * e))
        levels.append((size, assemble(qs), assemble(ks)))

    odd = (lrow & 1) == 1
    qs = [jnp.where(odd, qt[jl] * jnp.exp(gt[jl]), 0.0) for jl in range(nt)]
    ks = [jnp.where(odd, 0.0, kt[jl]) for jl in range(nt)]
    levels.append((2, assemble(qs), assemble(ks)))
    levels.append((1, q.astype(BF16), k.astype(BF16)))

    ti = lax.broadcasted_iota(I32, (CHUNK, CHUNK), 0)
    si = lax.broadcasted_iota(I32, (CHUNK, CHUNK), 1)
    blk = ti ^ si
    scores = None
    for size, qm, km in levels:
        s_m = lax.dot_general(qm, km, _NT_DIMS, preferred_element_type=F32)
        if size < CHUNK:
            s_m = jnp.where(blk < size, s_m, 0.0)
        scores = s_m if scores is None else scores + s_m

    q_in, k_out = [None] * nt, [None] * nt
    acc = None
    for jl in range(nt):
        q_in[jl] = qt[jl] * jnp.exp(plus(ct[jl], acc))
        acc = plus(tot[jl], acc)
    total = acc
    acc = None
    for jl in range(nt - 1, -1, -1):
        k_out[jl] = kt[jl] * jnp.exp(plus(suf[jl], acc))
        acc = plus(tot[jl], acc)
    return scores.astype(BF16), assemble(q_in), assemble(k_out), jnp.exp(total)


def _chunk_apply(scores, q_in, k_out, decay, v, st_ref):
    st = st_ref[...]
    out = (jnp.dot(scores, v, preferred_element_type=F32)
           + lax.dot_general(q_in, st.astype(BF16), _NT_DIMS, preferred_element_type=F32))
    st_ref[...] = st * decay + lax.dot_general(v, k_out, _TN_DIMS, preferred_element_type=F32)
    return out


def _recurrence_kernel(n_chunks, q_ref, kf_ref, kb_ref, v_ref, gf_ref, gb_ref, og_ref, ng_ref,
                       o_ref, of_ref, ob_ref, stf_ref, stb_ref, sc_ref, qin_ref, kout_ref, dec_ref):
    dirs = ((False, kf_ref, gf_ref, stf_ref, of_ref), (True, kb_ref, gb_ref, stb_ref, ob_ref))
    stf_ref[...] = jnp.zeros_like(stf_ref)
    stb_ref[...] = jnp.zeros_like(stb_ref)

    def rows_of(c, rev):
        cc = (n_chunks - 1 - c) if rev else c
        return pl.ds(pl.multiple_of(cc * CHUNK, CHUNK), CHUNK)

    def prepare(c):
        res = []
        for rev, k_ref, g_ref, _, _ in dirs:
            rows = rows_of(c, rev)
            res.append(_chunk_prepare(q_ref[rows, :].astype(F32), k_ref[rows, :].astype(F32),
                                      g_ref[rows, :], rev))
        return res

    def stash(res):
        for d, (scores, q_in, k_out, decay) in enumerate(res):
            sc_ref[d] = scores
            qin_ref[d] = q_in
            kout_ref[d] = k_out
            dec_ref[d] = jnp.broadcast_to(decay, (SUBLANES, LANES))

    def unstash():
        return [(sc_ref[d], qin_ref[d], kout_ref[d], dec_ref[d][0:1, :]) for d in range(2)]

    def apply(c, staged):
        for (rev, _, _, st_ref, out_ref), (scores, q_in, k_out, decay) in zip(dirs, staged):
            rows = rows_of(c, rev)
            out_ref[rows, :] = _chunk_apply(scores, q_in, k_out, decay, v_ref[rows, :], st_ref)

    stash(prepare(0))

    def body(c, carry):
        staged = unstash()
        nxt = prepare(c + 1)
        apply(c, staged)
        stash(nxt)
        return carry

    lax.fori_loop(0, n_chunks - 1, body, 0)
    apply(n_chunks - 1, unstash())
    o = of_ref[...] + ob_ref[...]
    o_ref[...] = (_rms(o, ng_ref[...]) * og_ref[...].astype(F32)).astype(BF16)


def _recurrence(z, gf, gb, norm_g, batch, seq, heads, dv, cols, gb_off=0, name=""):
    n_tok = batch * seq
    n_chunks = seq // CHUNK
    q_col, kf_col, kb_col, v_col, og_col = cols

    def spec(width, col=0):
        assert col % width == 0
        return pl.BlockSpec((seq, width), lambda b, h, off=col // width: (b, h + off))

    return pl.pallas_call(
        functools.partial(_recurrence_kernel, n_chunks),
        grid=(batch, heads),
        in_specs=[spec(HEAD_DK, q_col), spec(HEAD_DK, kf_col), spec(HEAD_DK, kb_col), spec(dv, v_col),
                  spec(HEAD_DK), spec(HEAD_DK, gb_off * HEAD_DK), spec(dv, og_col),
                  pl.BlockSpec((1, dv), lambda b, h: (0, 0))],
        out_specs=spec(dv),
        out_shape=jax.ShapeDtypeStruct((n_tok, heads * dv), BF16),
        scratch_shapes=[pltpu.VMEM((seq, dv), F32), pltpu.VMEM((seq, dv), F32),
                        pltpu.VMEM((dv, HEAD_DK), F32), pltpu.VMEM((dv, HEAD_DK), F32),
                        pltpu.VMEM((2, CHUNK, CHUNK), BF16), pltpu.VMEM((2, CHUNK, HEAD_DK), BF16),
                        pltpu.VMEM((2, CHUNK, HEAD_DK), BF16), pltpu.VMEM((2, SUBLANES, LANES), F32)],
        compiler_params=_params(("arbitrary", "arbitrary")),
        name=name,
    )(z, z, z, z, gf, gb, z, norm_g.reshape(1, dv).astype(F32))


MIX_SUB_ROWS = 256


def _mix_out_kernel(oa_ref, ob_ref, sga_ref, sgb_ref, x_ref, wpa_ref, wpb_ref, wo_ref, g2_ref, rw2_ref, rw1_ref,
                    rb_ref, h_ref, hn_ref, w4_ref, idx_ref, sel_ref):
    tm = x_ref.shape[0]
    sub = min(tm, MIX_SUB_ROWS)
    for r in range(tm // sub):
        _mix_out_rows(pl.ds(r * sub, sub), oa_ref, ob_ref, sga_ref, sgb_ref, x_ref, wpa_ref, wpb_ref, wo_ref,
                      g2_ref, rw2_ref, rw1_ref, rb_ref, h_ref, hn_ref, w4_ref, idx_ref, sel_ref)


def _mix_out_rows(rows, oa_ref, ob_ref, sga_ref, sgb_ref, x_ref, wpa_ref, wpb_ref, wo_ref, g2_ref, rw2_ref,
                  rw1_ref, rb_ref, h_ref, hn_ref, w4_ref, idx_ref, sel_ref):
    ya = jnp.dot(oa_ref[rows, :], wpa_ref[...], preferred_element_type=F32)
    yb = jnp.dot(ob_ref[rows, :], wpb_ref[...], preferred_element_type=F32)
    merged = sga_ref[rows, :].astype(F32) * ya + sgb_ref[rows, :].astype(F32) * yb
    h = x_ref[rows, :] + jnp.dot(merged.astype(BF16), wo_ref[...], preferred_element_type=F32)
    h_ref[rows, :] = h
    hn = _rms(h, g2_ref[...])
    hn_ref[rows, :] = _pack_bf16_pairs(hn)

    hn_hi = hn.astype(BF16)
    hn_lo = (hn - hn_hi.astype(F32)).astype(BF16)
    two = jnp.dot(hn_hi, rw2_ref[...], preferred_element_type=F32)
    logits = (two[:, :LANES] + two[:, LANES:] + jnp.dot(hn_lo, rw1_ref[...], preferred_element_type=F32)
              + rb_ref[...])
    lane = lax.broadcasted_iota(I32, logits.shape, 1)
    lane_f = lane.astype(F32)
    work = logits
    vals, idxs = [], []
    for _ in range(TOP_K):
        m = jnp.max(work, axis=-1, keepdims=True)
        idx = jnp.min(jnp.where(work == m, lane_f, float(LANES)), axis=-1, keepdims=True)
        vals.append(m)
        idxs.append(idx)
        work = jnp.where(lane_f == idx, -jnp.inf, work)
    es = [jnp.exp(v - vals[0]) for v in vals]
    denom = es[0]
    for e in es[1:]:
        denom = denom + e
    w4 = jnp.zeros(logits.shape, F32)
    i4 = jnp.zeros(logits.shape, F32)
    sel = jnp.zeros(logits.shape, F32)
    for k in range(TOP_K):
        w4 = jnp.where(lane == k, es[k] / denom, w4)
        i4 = jnp.where(lane == k, idxs[k], i4)
        sel = jnp.where(lane_f == idxs[k], 1.0, sel)
    w4_ref[rows, :] = w4
    idx_ref[rows, :] = i4.astype(I32)
    sel_ref[rows, :] = sel.astype(BF16)


def _mix_out(oa, ob, z, gate_col, x2, w_proj_a, w_proj_b, w_out, norm2_g, router_w, router_b):
    n_tok, d = x2.shape
    hv, gv = oa.shape[1], ob.shape[1]
    n_exp = router_w.shape[1]
    tm = _divisor_tile(n_tok, 2 * MIX_SUB_ROWS)
    rw = jnp.pad(router_w.astype(F32), ((0, 0), (0, LANES - n_exp)))
    rw_hi = rw.astype(BF16)
    rw_lo = (rw - rw_hi.astype(F32)).astype(BF16)
    rb = jnp.concatenate([router_b.astype(F32), jnp.full((LANES - n_exp,), NEG_BIG, F32)])[None, :]
    assert gate_col % d == 0
    gate_blk = gate_col // d

    def rows(width, off=0):
        return pl.BlockSpec((tm, width), lambda i, off=off: (i, off))

    def whole(r, c):
        return pl.BlockSpec((r, c), lambda i: (0, 0), pipeline_mode=pl.Buffered(1))

    return pl.pallas_call(
        _mix_out_kernel,
        grid=(n_tok // tm,),
        in_specs=[rows(hv), rows(gv), rows(d, gate_blk), rows(d, gate_blk + 1), rows(d),
                  whole(hv, d), whole(gv, d), whole(d, d), whole(1, d), whole(d, 2 * LANES), whole(d, LANES),
                  whole(1, LANES)],
        out_specs=[rows(d), rows(d // 2), rows(LANES), rows(LANES), rows(LANES)],
        out_shape=[jax.ShapeDtypeStruct((n_tok, d), F32),
                   jax.ShapeDtypeStruct((n_tok, d // 2), U32),
                   jax.ShapeDtypeStruct((n_tok, LANES), F32),
                   jax.ShapeDtypeStruct((n_tok, LANES), I32),
                   jax.ShapeDtypeStruct((n_tok, LANES), BF16)],
        compiler_params=_params(("arbitrary",)),
        name="mix_out",
    )(oa, ob, z, z, x2, w_proj_a.astype(BF16), w_proj_b.astype(BF16), w_out.astype(BF16),
      norm2_g.reshape(1, d).astype(F32), jnp.concatenate([rw_hi, rw_lo], axis=1), rw_hi, rb)


def _route_rank_kernel(row_block, sel_ref, idx_ref, tri_ref, dest_ref, cnt_ref, rank_ref, carry_ref, start_ref):
    p = pl.program_id(0)
    i = pl.program_id(1)
    tb = sel_ref.shape[0]
    rows = pl.ds(pl.multiple_of(i * tb, tb), tb)

    @pl.when((p == 0) & (i == 0))
    def _():
        carry_ref[...] = jnp.zeros_like(carry_ref)

    @pl.when(p == 0)
    def _():
        sel = sel_ref[...]
        before = jnp.dot(tri_ref[...], sel, preferred_element_type=F32)
        rank_ref[rows, :] = before + carry_ref[0:1, :]
        carry_ref[...] = carry_ref[...] + jnp.sum(sel.astype(F32), axis=0, keepdims=True)

    @pl.when((p == 1) & (i == 0))
    def _():
        counts = carry_ref[...]
        padded = jnp.ceil(counts * (1.0 / row_block)) * row_block
        lane = lax.broadcasted_iota(I32, padded.shape, 1)
        run = padded
        s = 1
        while s < LANES:
            run = run + jnp.where(lane >= s, pltpu.roll(run, s, 1), 0.0)
            s *= 2
        start_ref[...] = run - padded

    @pl.when(p == 1)
    def _():
        pos = rank_ref[rows, :] + start_ref[0:1, :]
        idx = idx_ref[...]
        lane = lax.broadcasted_iota(I32, pos.shape, 1)
        dest = jnp.zeros(pos.shape, F32)
        for k in range(TOP_K):
            hit = lane == idx[:, k:k + 1]
            val = jnp.sum(jnp.where(hit, pos, 0.0), axis=-1, keepdims=True)
            dest = jnp.where(lane == k, val, dest)
        dest_ref[...] = dest.astype(I32)
        cnt_ref[...] = carry_ref[...]


def _route_rank(sel, idx4, row_block):
    n_tok = sel.shape[0]
    tb = _divisor_tile(n_tok, 256)
    tri = (lax.broadcasted_iota(I32, (tb, tb), 0) > lax.broadcasted_iota(I32, (tb, tb), 1)).astype(BF16)
    return pl.pallas_call(
        functools.partial(_route_rank_kernel, row_block),
        grid=(2, n_tok // tb),
        in_specs=[pl.BlockSpec((tb, LANES), lambda p, i: (i, 0)),
                  pl.BlockSpec((tb, LANES), lambda p, i: (i, 0)),
                  pl.BlockSpec((tb, tb), lambda p, i: (0, 0))],
        out_specs=[pl.BlockSpec((tb, LANES), lambda p, i: (i * p, 0)),
                   pl.BlockSpec((SUBLANES, LANES), lambda p, i: (0, 0))],
        out_shape=[jax.ShapeDtypeStruct((n_tok, LANES), I32),
                   jax.ShapeDtypeStruct((SUBLANES, LANES), F32)],
        scratch_shapes=[pltpu.VMEM((n_tok, LANES), F32), pltpu.VMEM((SUBLANES, LANES), F32),
                        pltpu.VMEM((SUBLANES, LANES), F32)],
        compiler_params=_params(("arbitrary", "arbitrary")),
        name="route_rank",
    )(sel, idx4, tri)


def _dispatch_kernel(dest_ref, hn_ref, xs_in_ref, xs_ref, sem):
    del xs_in_ref
    tt = hn_ref.shape[0]
    for t in range(tt):
        for k in range(TOP_K):
            pltpu.make_async_copy(hn_ref.at[pl.ds(t, 1), :], xs_ref.at[pl.ds(dest_ref[TOP_K * t + k], 1), :],
                                  sem).start(priority=(TOP_K * t + k) % 2)
    for k in range(TOP_K):
        pltpu.make_async_copy(hn_ref, xs_ref.at[pl.ds(0, tt), :], sem).wait()


def _dispatch(dest_flat, hn_packed, n_rows):
    n_tok, half = hn_packed.shape
    tt = _divisor_tile(n_tok, 512)
    xs0 = jnp.zeros((n_rows, half), U32)
    return pl.pallas_call(
        _dispatch_kernel,
        grid=(n_tok // tt,),
        in_specs=[pl.BlockSpec((tt * TOP_K,), lambda i: (i,), memory_space=pltpu.SMEM),
                  pl.BlockSpec((tt, half), lambda i: (i, 0)),
                  pl.BlockSpec(memory_space=pl.ANY)],
        out_specs=pl.BlockSpec(memory_space=pl.ANY),
        out_shape=jax.ShapeDtypeStruct((n_rows, half), U32),
        scratch_shapes=[pltpu.SemaphoreType.DMA(())],
        input_output_aliases={2: 0},
        compiler_params=_params(("arbitrary",)),
        name="dispatch",
    )(dest_flat, hn_packed, xs0)


GROUP = 2 * LANES


def _regroup_kernel(w_ref, p_ref, o_ref):
    p = p_ref[...]
    for g in range(w_ref.shape[1] // GROUP):
        cols = slice(GROUP * g, GROUP * (g + 1))
        o_ref[:, cols] = jnp.dot(w_ref[:, cols].astype(BF16), p, preferred_element_type=F32).astype(BF16)


FFN_FF_TILE = 512


def _regroup_gate_up(w_gate_up):
    n_exp, d, two_f = w_gate_up.shape
    tr = _divisor_tile(d, 2048)
    tc = 2 * _divisor_tile(two_f // 2, FFN_FF_TILE)
    src = lax.broadcasted_iota(I32, (GROUP, GROUP), 0)
    dst = lax.broadcasted_iota(I32, (GROUP, GROUP), 1)
    perm = (dst == (src // 2) + LANES * (src % 2)).astype(BF16)
    return pl.pallas_call(
        _regroup_kernel,
        grid=(n_exp, d // tr, two_f // tc),
        in_specs=[pl.BlockSpec((None, tr, tc), lambda e, i, j: (e, i, j)),
                  pl.BlockSpec((GROUP, GROUP), lambda e, i, j: (0, 0))],
        out_specs=pl.BlockSpec((None, None, tr, tc), lambda e, i, j: (e, j, i, 0)),
        out_shape=jax.ShapeDtypeStruct((n_exp, two_f // tc, d, tc), BF16),
        compiler_params=_params(("arbitrary", "arbitrary", "arbitrary")),
        name="regroup_gate_up",
    )(w_gate_up, perm)


def _regroup_bias(b_gate_up):
    n_exp, two_f = b_gate_up.shape
    b = b_gate_up.reshape(n_exp, two_f // GROUP, LANES, 2)
    return jnp.swapaxes(b, 2, 3).reshape(n_exp, 1, two_f).astype(F32)


FFN_ROW_BLOCK = 1024
FFN_SUB_ROWS = 256


def _expert_ffn_kernel(be_ref, bv_ref, xs_ref, wgu_ref, wd_ref, bgu_ref, bd_ref,
                       out_ref, x_ref, acc_ref):
    del be_ref
    b = pl.program_id(0)
    f = pl.program_id(1)
    n_f = pl.num_programs(1)
    n_rows = xs_ref.shape[0]
    sub = min(n_rows, FFN_SUB_ROWS)
    valid = bv_ref[b]
    n_sub = (valid + sub - 1) // sub

    @pl.when((valid > 0) & (f == 0))
    def _():
        hi, lo = _unpack_bf16_pairs(xs_ref[...])
        x_ref[...] = jnp.concatenate([hi.astype(BF16), lo.astype(BF16)], axis=1)
        acc_ref[...] = jnp.zeros_like(acc_ref)

    def ffn(rows):
        gu = jnp.dot(x_ref[rows, :], wgu_ref[...], preferred_element_type=F32) + bgu_ref[...]
        acts = []
        for g in range(gu.shape[1] // GROUP):
            gate = jnp.minimum(gu[:, GROUP * g:GROUP * g + LANES], SWIGLU_LIMIT)
            up = jnp.clip(gu[:, GROUP * g + LANES:GROUP * (g + 1)], -SWIGLU_LIMIT, SWIGLU_LIMIT)
            acts.append(((up + 1.0) * (gate * jax.nn.sigmoid(SWIGLU_ALPHA * gate))).astype(BF16))
        act = jnp.concatenate(acts, axis=1)
        acc_ref[rows, :] += jnp.dot(act, wd_ref[...], preferred_element_type=F32)

    @pl.when(n_sub == n_rows // sub)
    def _():
        ffn(pl.ds(0, n_rows))

    @pl.when(n_sub < n_rows // sub)
    def _():
        def body(s, carry):
            ffn(pl.ds(pl.multiple_of(s * sub, sub), sub))
            return carry
        lax.fori_loop(0, n_sub, body, 0)

    @pl.when((valid > 0) & (f == n_f - 1))
    def _():
        out_ref[...] = _pack_bf16_pairs(acc_ref[...] + bd_ref[...])

    @pl.when((valid == 0) & (f == n_f - 1))
    def _():
        out_ref[...] = jnp.zeros_like(out_ref)


def _expert_ffn(block_e, block_valid, xs, wgu, wd, bgu, bd, row_block):
    n_rows, half = xs.shape
    d = 2 * half
    d_ff = wd.shape[1]
    tf = wgu.shape[3] // 2
    n_blocks = n_rows // row_block
    grid_spec = pltpu.PrefetchScalarGridSpec(
        num_scalar_prefetch=2,
        grid=(n_blocks, d_ff // tf),
        in_specs=[
            pl.BlockSpec((row_block, half), lambda b, f, be, bv: (b, 0)),
            pl.BlockSpec((None, None, d, 2 * tf), lambda b, f, be, bv: (be[b], f, 0, 0)),
            pl.BlockSpec((None, tf, d), lambda b, f, be, bv: (be[b], f, 0)),
            pl.BlockSpec((None, 1, 2 * tf), lambda b, f, be, bv: (be[b], 0, f)),
            pl.BlockSpec((None, 1, d), lambda b, f, be, bv: (be[b], 0, 0)),
        ],
        out_specs=pl.BlockSpec((row_block, half), lambda b, f, be, bv: (b, 0)),
        scratch_shapes=[pltpu.VMEM((row_block, d), BF16), pltpu.VMEM((row_block, d), F32)],
    )
    return pl.pallas_call(
        _expert_ffn_kernel,
        grid_spec=grid_spec,
        out_shape=jax.ShapeDtypeStruct((n_rows, half), U32),
        compiler_params=_params(("arbitrary", "arbitrary")),
        name="expert_ffn",
    )(block_e, block_valid, xs, wgu, wd, bgu, bd)


def _combine_kernel(cur_ref, nxt_ref, h_ref, w4_ref, fg_ref, rows_hbm_ref, o_ref, buf_ref, sem):
    i = pl.program_id(0)
    n = pl.num_programs(0)
    tt = h_ref.shape[0]
    half = buf_ref.shape[3]

    def issue(dest_ref, s):
        for t in range(tt):
            for k in range(TOP_K):
                pltpu.make_async_copy(rows_hbm_ref.at[pl.ds(dest_ref[TOP_K * t + k], 1), :],
                                      buf_ref.at[s, k, pl.ds(t, 1), :],
                                      sem.at[s]).start(priority=(TOP_K * t + k) % 2)

    def drain(s):
        for k in range(TOP_K):
            pltpu.make_async_copy(rows_hbm_ref.at[pl.ds(0, tt), :], buf_ref.at[s, k], sem.at[s]).wait()

    @pl.when(i == 0)
    def _():
        issue(cur_ref, 0)

    def step(slot):
        drain(slot)
        issue(nxt_ref, 1 - slot)
        w4 = w4_ref[...]
        y_hi = jnp.zeros((tt, half), F32)
        y_lo = jnp.zeros((tt, half), F32)
        for k in range(TOP_K):
            hi, lo = _unpack_bf16_pairs(buf_ref[slot, k])
            wk = w4[:, k:k + 1]
            y_hi = y_hi + wk * hi
            y_lo = y_lo + wk * lo
        h = h_ref[...] + jnp.concatenate([y_hi, y_lo], axis=1)
        o_ref[...] = _rms(h, fg_ref[...])

        @pl.when(i == n - 1)
        def _():
            drain(1 - slot)

    for slot in range(2):
        pl.when(i % 2 == slot)(functools.partial(step, slot))


def _combine(dest_flat, h, w4, final_g, rows_packed):
    n_tok, d = h.shape
    half = d // 2
    tt = _divisor_tile(n_tok, 256)
    n_steps = n_tok // tt
    return pl.pallas_call(
        _combine_kernel,
        grid=(n_steps,),
        in_specs=[pl.BlockSpec((tt * TOP_K,), lambda i: (i,), memory_space=pltpu.SMEM),
                  pl.BlockSpec((tt * TOP_K,), lambda i: (jnp.minimum(i + 1, n_steps - 1),),
                               memory_space=pltpu.SMEM),
                  pl.BlockSpec((tt, d), lambda i: (i, 0)),
                  pl.BlockSpec((tt, LANES), lambda i: (i, 0)),
                  pl.BlockSpec((1, d), lambda i: (0, 0)),
                  pl.BlockSpec(memory_space=pl.ANY)],
        out_specs=pl.BlockSpec((tt, d), lambda i: (i, 0)),
        out_shape=jax.ShapeDtypeStruct((n_tok, d), F32),
        scratch_shapes=[pltpu.VMEM((2, TOP_K, tt, half), U32), pltpu.SemaphoreType.DMA((2,))],
        compiler_params=_params(("arbitrary",)),
        name="combine",
    )(dest_flat, dest_flat, h, w4, final_g.reshape(1, d).astype(F32), rows_packed)


def _layer(h2, batch, seq, lb, norm1_g, w_in, hg_norm_g, gate_w2, gate_b, gla_norm_g, w_proj_a, w_proj_b,
           w_out, norm2_g, router_w, router_b, w_gate_up, b_gate_up, w_down, b_down, out_norm_g):
    n_tok, d = h2.shape
    hf = lb.shape[1]
    hv = w_proj_a.shape[0]
    gk = gate_w2.shape[2]
    gv = w_proj_b.shape[0]
    rank = gate_w2.shape[1]
    hg_heads, gla_heads = hf // HEAD_DK, gk // HEAD_DK
    n_exp = router_w.shape[1]

    z, gf, gb, gg = _in_proj(h2, norm1_g, w_in, lb, gate_w2, gate_b, (hf, hv, gk, gv, rank))
    a_cols = (0, hf, 2 * hf, 3 * hf, 3 * hf + hv)
    b0 = 3 * hf + 2 * hv
    b_cols = (b0, b0 + gk, b0 + gk, b0 + 2 * gk, b0 + 2 * gk + gv)
    oa = _recurrence(z, gf, gb, hg_norm_g, batch, seq, hg_heads, hv // hg_heads, a_cols,
                     name="hgrn_recurrence")
    ob = _recurrence(z, gg, gg, gla_norm_g, batch, seq, gla_heads, gv // gla_heads, b_cols,
                     gb_off=gla_heads, name="gla_recurrence")
    h, hn_packed, w4, idx4, sel = _mix_out(oa, ob, z, b0 + 2 * gk + 2 * gv, h2, w_proj_a, w_proj_b, w_out,
                                           norm2_g, router_w, router_b)

    n_pairs = n_tok * TOP_K
    row_block = _divisor_tile(n_pairs, FFN_ROW_BLOCK)
    n_blocks = n_pairs // row_block + n_exp
    dest4, counts = _route_rank(sel, idx4, row_block)
    dest_flat = dest4[:, :TOP_K].reshape(-1)
    cnt = counts[0, :n_exp].astype(I32)
    blocks_e = (cnt + row_block - 1) // row_block
    end_blk = jnp.cumsum(blocks_e)
    start_blk = end_blk - blocks_e
    bidx = jnp.arange(n_blocks, dtype=I32)[:, None]
    block_e = jnp.minimum(jnp.sum((end_blk[None, :] <= bidx).astype(I32), axis=1), n_exp - 1)
    owned = (bidx >= start_blk[None, :]) & (bidx < end_blk[None, :])
    rows_left = jnp.clip(cnt[None, :] - (bidx - start_blk[None, :]) * row_block, 0, row_block)
    block_valid = jnp.sum(jnp.where(owned, rows_left, 0), axis=1).astype(I32)

    xs = _dispatch(dest_flat, hn_packed, n_blocks * row_block)
    rows = _expert_ffn(block_e, block_valid, xs, _regroup_gate_up(w_gate_up), w_down.astype(BF16),
                       _regroup_bias(b_gate_up), b_down[:, None, :].astype(F32), row_block)
    return _combine(dest_flat, h, w4, out_norm_g, rows)


def kernel(x, norm1_g, w_in, hg_lb_logits, hg_norm_g, gla_gate_w2, gla_gate_b, gla_norm_g, w_proj_a, w_proj_b,
           w_out, norm2_g, router_w, router_b, w_gate_up, b_gate_up, w_down, b_down, final_norm_g):
    batch, seq, d = x.shape
    depth = w_in.shape[0]
    assert depth == 1, "the final RMSNorm is fused into the last layer's combine kernel"
    lb_all = jnp.cumsum(jax.nn.softmax(hg_lb_logits.astype(F32), axis=1), axis=1)
    h2 = x.reshape(batch * seq, d)
    out = _layer(h2, batch, seq, lb_all[:, 0], norm1_g[0], w_in[0], hg_norm_g[0], gla_gate_w2[0], gla_gate_b[0],
                 gla_norm_g[0], w_proj_a[0], w_proj_b[0], w_out[0], norm2_g[0], router_w[0], router_b[0],
                 w_gate_up[0], b_gate_up[0], w_down[0], b_down[0], final_norm_g)
    return out.reshape(batch, seq, d)
```

```python
import functools

import jax
import jax.numpy as jnp
from jax import lax
from jax.experimental import pallas as pl
from jax.experimental.pallas import tpu as pltpu

F32 = jnp.float32
BF16 = jnp.bfloat16
U32 = jnp.uint32
I32 = jnp.int32

LANES = 128
SUBLANES = 8
VMEM_LIMIT_BYTES = 56 * 1024 * 1024

HEAD_DK = 128
GLA_DV = 256
HG_DV = 128
GATE_NORM = 16.0
TOP_K = 4
SWIGLU_LIMIT = 7.0
SWIGLU_ALPHA = 1.702
RMS_EPS = 1e-5
CHUNK = 64
NEG_BIG = -1e30
LOG2_E = 1.4426950408889634


def _divisor_tile(n, pref):
    t = min(n, pref)
    while n % t:
        t //= 2
    return t


def _params(sem, vmem=VMEM_LIMIT_BYTES):
    return pltpu.CompilerParams(dimension_semantics=sem, vmem_limit_bytes=vmem)


def _pack_bf16_pairs(a):
    n = a.shape[-1] // 2
    hi = lax.bitcast_convert_type(a[:, :n].astype(BF16).astype(F32), U32)
    lo = lax.bitcast_convert_type(a[:, n:].astype(BF16).astype(F32), U32)
    return hi | (lo >> 16)


def _unpack_bf16_pairs(p):
    hi = lax.bitcast_convert_type(p & jnp.uint32(0xFFFF0000), F32)
    lo = lax.bitcast_convert_type(p << 16, F32)
    return hi, lo


def _rms(x, g):
    ms = jnp.mean(x * x, axis=-1, keepdims=True)
    return x * lax.rsqrt(ms + RMS_EPS) * g


def _log_sigmoid(x):
    return jnp.minimum(x, 0.0) - jnp.log1p(jnp.exp(-jnp.abs(x)))


def _in_proj_kernel(bounds, x_ref, g1_ref, w_ref, lb_ref, w2_ref, b2_ref,
                    z_ref, gf_ref, gb_ref, gg_ref, xn_ref):
    j = pl.program_id(1)

    @pl.when(j == 0)
    def _():
        xn_ref[...] = _rms(x_ref[...], g1_ref[...]).astype(BF16)

    scale = HEAD_DK ** -0.5

    def project():
        return jnp.dot(xn_ref[...], w_ref[...], preferred_element_type=F32)

    def silu(z):
        return z * jax.nn.sigmoid(z)

    def store(fn):
        def epilogue():
            z_ref[...] = fn(project()).astype(BF16)
        return epilogue

    def forget(g_ref):
        def epilogue():
            lb = lb_ref[0:1, :]
            f = lb + (1.0 - lb) * jax.nn.sigmoid(project())
            z_ref[...] = (1.0 - f).astype(BF16)
            g_ref[...] = jnp.log2(f)
        return epilogue

    def low_rank_gate():
        lr = project()[:, :LANES].astype(BF16)
        logits = jnp.dot(lr, w2_ref[...], preferred_element_type=F32) + b2_ref[...]
        gg_ref[...] = _log_sigmoid(logits) * (LOG2_E / GATE_NORM)

    epilogues = [
        store(lambda z: silu(z) * scale),
        forget(gf_ref),
        forget(gb_ref),
        store(lambda z: z),
        store(silu),
        store(lambda z: z * scale),
        store(lambda z: z),
        store(lambda z: z),
        store(silu),
        store(jax.nn.sigmoid),
        store(jax.nn.sigmoid),
        low_rank_gate,
    ]
    for k, epilogue in enumerate(epilogues):
        pl.when((j >= bounds[k]) & (j < bounds[k + 1]))(epilogue)


def _in_proj(x2, norm1_g, w_in, lb, gate_w2, gate_b, dims):
    n_tok, d = x2.shape
    hf, hv, gk, gv, rank = dims
    widths = [hf, hf, hf, hv, hv, gk, gk, gv, gv]
    off_lr = sum(widths)
    n_main = off_lr + 2 * d
    tn = _divisor_tile(gk, 256)
    tm = _divisor_tile(n_tok, 1024)
    widths_all = widths + [d, d, tn]
    w_lr = jnp.pad(w_in[:, off_lr:off_lr + 2 * rank], ((0, 0), (0, tn - 2 * rank)))
    w_cat = jnp.concatenate([w_in[:, :off_lr], w_in[:, off_lr + 2 * rank:], w_lr], axis=1).astype(BF16)
    n_col = w_cat.shape[1]
    w_tiles = w_cat.reshape(d, n_col // tn, tn).transpose(1, 0, 2)
    lb_tab = jnp.zeros((SUBLANES, n_col), F32)
    lb_tab = lb_tab.at[0, hf:2 * hf].set(lb[0]).at[0, 2 * hf:3 * hf].set(lb[1])
    w2 = jnp.zeros((LANES, 2 * gk), F32)
    w2 = w2.at[:rank, :gk].set(gate_w2[0]).at[rank:2 * rank, gk:].set(gate_w2[1]).astype(BF16)
    b2 = jnp.concatenate([gate_b[0], gate_b[1]])[None, :].astype(F32)

    bounds = [0]
    for w in widths_all:
        bounds.append(bounds[-1] + w // tn)
    n_j = bounds[-1]

    def seg_spec(k):
        s, n = bounds[k], bounds[k + 1] - bounds[k]
        return pl.BlockSpec((tm, tn), lambda i, j, s=s, n=n: (i, jnp.clip(j - s, 0, n - 1)))

    return pl.pallas_call(
        functools.partial(_in_proj_kernel, tuple(bounds)),
        grid=(n_tok // tm, n_j),
        in_specs=[
            pl.BlockSpec((tm, d), lambda i, j: (i, 0)),
            pl.BlockSpec((1, d), lambda i, j: (0, 0)),
            pl.BlockSpec((None, d, tn), lambda i, j: (j, 0, 0)),
            pl.BlockSpec((SUBLANES, tn), lambda i, j: (0, j)),
            pl.BlockSpec((LANES, 2 * gk), lambda i, j: (0, 0)),
            pl.BlockSpec((1, 2 * gk), lambda i, j: (0, 0)),
        ],
        out_specs=[pl.BlockSpec((tm, tn), lambda i, j: (i, jnp.minimum(j, n_main // tn - 1))),
                   seg_spec(1), seg_spec(2),
                   pl.BlockSpec((tm, 2 * gk), lambda i, j: (i, 0))],
        out_shape=[jax.ShapeDtypeStruct((n_tok, n_main), BF16),
                   jax.ShapeDtypeStruct((n_tok, hf), F32),
                   jax.ShapeDtypeStruct((n_tok, hf), F32),
                   jax.ShapeDtypeStruct((n_tok, 2 * gk), F32)],
        scratch_shapes=[pltpu.VMEM((tm, d), BF16)],
        compiler_params=_params(("arbitrary", "arbitrary")),
        name="in_proj",
    )(x2, norm1_g.reshape(1, d), w_tiles, lb_tab, w2, b2)


_NT_DIMS = (((1,), (1,)), ((), ()))
_TN_DIMS = (((0,), (0,)), ((), ()))


def _chunk_prepare(q, k, g, rev):
    nt = CHUNK // SUBLANES
    row = lax.broadcasted_iota(I32, (SUBLANES, LANES), 0)
    lrow = (SUBLANES - 1 - row) if rev else row

    def phys(jl):
        return nt - 1 - jl if rev else jl

    def prow(x, rl):
        r = SUBLANES - 1 - rl if rev else rl
        return x[r:r + 1, :]

    def tile(x, jl):
        p = phys(jl)
        return x[SUBLANES * p:SUBLANES * (p + 1), :]

    def scan(x):
        for s in (1, 2, 4):
            if rev:
                x = x + jnp.where(row < SUBLANES - s, pltpu.roll(x, SUBLANES - s, 0), 0.0)
            else:
                x = x + jnp.where(row >= s, pltpu.roll(x, s, 0), 0.0)
        return x

    gt = [tile(g, jl) for jl in range(nt)]
    qt = [tile(q, jl) for jl in range(nt)]
    kt = [tile(k, jl) for jl in range(nt)]
    ct = [scan(x) for x in gt]
    tot = [prow(c, SUBLANES - 1) for c in ct]
    suf = [t - c for t, c in zip(tot, ct)]

    def assemble(tiles):
        out = [None] * nt
        for jl, x in enumerate(tiles):
            out[phys(jl)] = jnp.zeros((SUBLANES, LANES), F32) if x is None else x
        return jnp.concatenate(out, axis=0).astype(BF16)

    def plus(a, b):
        return a if b is None else a + b

    levels = []

    for nb in (8, 4, 2):
        half = nb // 2
        qs, ks = [None] * nt, [None] * nt
        for base in range(0, nt, nb):
            acc = None
            for jl in range(base + half, base + nb):
                qs[jl] = qt[jl] * jnp.exp2(plus(ct[jl], acc))
                acc = plus(tot[jl], acc)
            acc = None
            for jl in range(base + half - 1, base - 1, -1):
                ks[jl] = kt[jl] * jnp.exp2(plus(suf[jl], acc))
                acc = plus(tot[jl], acc)
        levels.append((SUBLANES * nb, assemble(qs), assemble(ks)))

    for size in (8, 4):
        half = size // 2
        is_q = (lrow & (size - 1)) >= half
        qs, ks = [], []
        for jl in range(nt):
            if size == 8:
                ref = prow(ct[jl], half - 1)
            else:
                ref = jnp.where(lrow >= size, prow(ct[jl], size + half - 1), prow(ct[jl], half - 1))
            d = ct[jl] - ref
            e = jnp.exp2(jnp.where(is_q, d, -d))
            qs.append(jnp.where(is_q, qt[jl] * e, 0.0))
            ks.append(jnp.where(is_q, 0.0, kt[jl] * e))
        levels.append((size, assemble(qs), assemble(ks)))

    odd = (lrow & 1) == 1
    qs = [jnp.where(odd, qt[jl] * jnp.exp2(gt[jl]), 0.0) for jl in range(nt)]
    ks = [jnp.where(odd, 0.0, kt[jl]) for jl in range(nt)]
    levels.append((2, assemble(qs), assemble(ks)))
    levels.append((1, q.astype(BF16), k.astype(BF16)))

    ti = lax.broadcasted_iota(I32, (CHUNK, CHUNK), 0)
    si = lax.broadcasted_iota(I32, (CHUNK, CHUNK), 1)
    blk = ti ^ si
    scores = None
    for size, qm, km in levels:
        s_m = lax.dot_general(qm, km, _NT_DIMS, preferred_element_type=F32)
        scores = s_m if scores is None else jnp.where(blk < size, s_m, scores)

    q_in, k_out = [None] * nt, [None] * nt
    acc = None
    for jl in range(nt):
        q_in[jl] = qt[jl] * jnp.exp2(plus(ct[jl], acc))
        acc = plus(tot[jl], acc)
    total = acc
    acc = None
    for jl in range(nt - 1, -1, -1):
        k_out[jl] = kt[jl] * jnp.exp2(plus(suf[jl], acc))
        acc = plus(tot[jl], acc)
    return scores.astype(BF16), assemble(q_in), assemble(k_out), jnp.exp2(total)


def _chunk_apply(scores, q_in, k_out, decay, v, st_ref):
    st = st_ref[...]
    out = (jnp.dot(scores, v, preferred_element_type=F32)
           + lax.dot_general(q_in, st.astype(BF16), _NT_DIMS, preferred_element_type=F32))
    st_ref[...] = st * decay + lax.dot_general(v, k_out, _TN_DIMS, preferred_element_type=F32)
    return out


def _recurrence_kernel(n_chunks, dv, q_ref, kf_ref, kb_ref, v_ref, gf_ref, gb_ref, og_ref, ng_ref,
                       o_ref, of_ref, ob_ref, st_ref, sc_ref, qin_ref, kout_ref, dec_ref):
    n_heads = q_ref.shape[1] // HEAD_DK
    lanes = [(rev, k_ref, g_ref, out_ref, hd)
             for rev, k_ref, g_ref, out_ref in ((False, kf_ref, gf_ref, of_ref), (True, kb_ref, gb_ref, ob_ref))
             for hd in range(n_heads)]
    st_ref[...] = jnp.zeros_like(st_ref)

    def rows_of(c, rev):
        cc = (n_chunks - 1 - c) if rev else c
        return pl.ds(pl.multiple_of(cc * CHUNK, CHUNK), CHUNK)

    def kcols(hd):
        return slice(HEAD_DK * hd, HEAD_DK * (hd + 1))

    def vcols(hd):
        return slice(dv * hd, dv * (hd + 1))

    def prepare(c):
        res = []
        for rev, k_ref, g_ref, _, hd in lanes:
            rows = rows_of(c, rev)
            res.append(_chunk_prepare(q_ref[rows, kcols(hd)].astype(F32), k_ref[rows, kcols(hd)].astype(F32),
                                      g_ref[rows, kcols(hd)], rev))
        return res

    def stash(res):
        for d, (scores, q_in, k_out, decay) in enumerate(res):
            sc_ref[d] = scores
            qin_ref[d] = q_in
            kout_ref[d] = k_out
            dec_ref[d] = jnp.broadcast_to(decay, (SUBLANES, LANES))

    def unstash():
        return [(sc_ref[d], qin_ref[d], kout_ref[d], dec_ref[d][0:1, :]) for d in range(len(lanes))]

    def apply(c, staged):
        for d, ((rev, _, _, out_ref, hd), (scores, q_in, k_out, decay)) in enumerate(zip(lanes, staged)):
            rows = rows_of(c, rev)
            out_ref[rows, vcols(hd)] = _chunk_apply(scores, q_in, k_out, decay, v_ref[rows, vcols(hd)],
                                                    st_ref.at[d])

    stash(prepare(0))

    def body(c, carry):
        staged = unstash()
        nxt = prepare(c + 1)
        apply(c, staged)
        stash(nxt)
        return carry

    lax.fori_loop(0, n_chunks - 1, body, 0)
    apply(n_chunks - 1, unstash())
    for hd in range(n_heads):
        o = of_ref[:, vcols(hd)] + ob_ref[:, vcols(hd)]
        o_ref[:, vcols(hd)] = (_rms(o, ng_ref[...]) * og_ref[:, vcols(hd)].astype(F32)).astype(BF16)


RECURRENCE_HEADS_PER_STEP = 2


def _recurrence(z, gf, gb, norm_g, batch, seq, heads, dv, cols, gb_off=0, name=""):
    n_tok = batch * seq
    n_chunks = seq // CHUNK
    q_col, kf_col, kb_col, v_col, og_col = cols
    hp = RECURRENCE_HEADS_PER_STEP if heads % RECURRENCE_HEADS_PER_STEP == 0 else 1

    def spec(width, col=0):
        width = width * hp
        assert col % width == 0
        return pl.BlockSpec((seq, width), lambda b, h, off=col // width: (b, h + off))

    return pl.pallas_call(
        functools.partial(_recurrence_kernel, n_chunks, dv),
        grid=(batch, heads // hp),
        in_specs=[spec(HEAD_DK, q_col), spec(HEAD_DK, kf_col), spec(HEAD_DK, kb_col), spec(dv, v_col),
                  spec(HEAD_DK), spec(HEAD_DK, gb_off * HEAD_DK), spec(dv, og_col),
                  pl.BlockSpec((1, dv), lambda b, h: (0, 0))],
        out_specs=spec(dv),
        out_shape=jax.ShapeDtypeStruct((n_tok, heads * dv), BF16),
        scratch_shapes=[pltpu.VMEM((seq, hp * dv), F32), pltpu.VMEM((seq, hp * dv), F32),
                        pltpu.VMEM((2 * hp, dv, HEAD_DK), F32),
                        pltpu.VMEM((2 * hp, CHUNK, CHUNK), BF16), pltpu.VMEM((2 * hp, CHUNK, HEAD_DK), BF16),
                        pltpu.VMEM((2 * hp, CHUNK, HEAD_DK), BF16), pltpu.VMEM((2 * hp, SUBLANES, LANES), F32)],
        compiler_params=_params(("arbitrary", "arbitrary")),
        name=name,
    )(z, z, z, z, gf, gb, z, norm_g.reshape(1, dv).astype(F32))


MIX_SUB_ROWS = 256


def _mix_out_kernel(oa_ref, ob_ref, sga_ref, sgb_ref, x_ref, wpa_ref, wpb_ref, wo_ref, g2_ref, rw2_ref, rw1_ref,
                    rb_ref, h_ref, hn_ref, w4_ref, idx_ref, sel_ref):
    tm = x_ref.shape[0]
    sub = min(tm, MIX_SUB_ROWS)
    for r in range(tm // sub):
        _mix_out_rows(pl.ds(r * sub, sub), oa_ref, ob_ref, sga_ref, sgb_ref, x_ref, wpa_ref, wpb_ref, wo_ref,
                      g2_ref, rw2_ref, rw1_ref, rb_ref, h_ref, hn_ref, w4_ref, idx_ref, sel_ref)


def _mix_out_rows(rows, oa_ref, ob_ref, sga_ref, sgb_ref, x_ref, wpa_ref, wpb_ref, wo_ref, g2_ref, rw2_ref,
                  rw1_ref, rb_ref, h_ref, hn_ref, w4_ref, idx_ref, sel_ref):
    ya = jnp.dot(oa_ref[rows, :], wpa_ref[...], preferred_element_type=F32)
    yb = jnp.dot(ob_ref[rows, :], wpb_ref[...], preferred_element_type=F32)
    merged = sga_ref[rows, :].astype(F32) * ya + sgb_ref[rows, :].astype(F32) * yb
    h = x_ref[rows, :] + jnp.dot(merged.astype(BF16), wo_ref[...], preferred_element_type=F32)
    h_ref[rows, :] = h
    hn = _rms(h, g2_ref[...])
    hn_ref[rows, :] = _pack_bf16_pairs(hn)

    hn_hi = hn.astype(BF16)
    hn_lo = (hn - hn_hi.astype(F32)).astype(BF16)
    two = jnp.dot(hn_hi, rw2_ref[...], preferred_element_type=F32)
    logits = (two[:, :LANES] + two[:, LANES:] + jnp.dot(hn_lo, rw1_ref[...], preferred_element_type=F32)
              + rb_ref[...])
    lane = lax.broadcasted_iota(I32, logits.shape, 1)
    lane_f = lane.astype(F32)
    work = logits
    vals, idxs = [], []
    for _ in range(TOP_K):
        m = jnp.max(work, axis=-1, keepdims=True)
        idx = jnp.min(jnp.where(work == m, lane_f, float(LANES)), axis=-1, keepdims=True)
        vals.append(m)
        idxs.append(idx)
        work = jnp.where(lane_f == idx, -jnp.inf, work)
    es = [jnp.exp(v - vals[0]) for v in vals]
    denom = es[0]
    for e in es[1:]:
        denom = denom + e
    w4 = jnp.zeros(logits.shape, F32)
    i4 = jnp.zeros(logits.shape, F32)
    sel = jnp.zeros(logits.shape, F32)
    for k in range(TOP_K):
        w4 = jnp.where(lane == k, es[k] / denom, w4)
        i4 = jnp.where(lane == k, idxs[k], i4)
        sel = jnp.where(lane_f == idxs[k], 1.0, sel)
    w4_ref[rows, :] = w4
    idx_ref[rows, :] = i4.astype(I32)
    sel_ref[rows, :] = sel.astype(BF16)


def _mix_out(oa, ob, z, gate_col, x2, w_proj_a, w_proj_b, w_out, norm2_g, router_w, router_b):
    n_tok, d = x2.shape
    hv, gv = oa.shape[1], ob.shape[1]
    n_exp = router_w.shape[1]
    tm = _divisor_tile(n_tok, 2 * MIX_SUB_ROWS)
    rw = jnp.pad(router_w.astype(F32), ((0, 0), (0, LANES - n_exp)))
    rw_hi = rw.astype(BF16)
    rw_lo = (rw - rw_hi.astype(F32)).astype(BF16)
    rb = jnp.concatenate([router_b.astype(F32), jnp.full((LANES - n_exp,), NEG_BIG, F32)])[None, :]
    assert gate_col % d == 0
    gate_blk = gate_col // d

    def rows(width, off=0):
        return pl.BlockSpec((tm, width), lambda i, off=off: (i, off))

    def whole(r, c):
        return pl.BlockSpec((r, c), lambda i: (0, 0), pipeline_mode=pl.Buffered(1))

    return pl.pallas_call(
        _mix_out_kernel,
        grid=(n_tok // tm,),
        in_specs=[rows(hv), rows(gv), rows(d, gate_blk), rows(d, gate_blk + 1), rows(d),
                  whole(hv, d), whole(gv, d), whole(d, d), whole(1, d), whole(d, 2 * LANES), whole(d, LANES),
                  whole(1, LANES)],
        out_specs=[rows(d), rows(d // 2), rows(LANES), rows(LANES), rows(LANES)],
        out_shape=[jax.ShapeDtypeStruct((n_tok, d), F32),
                   jax.ShapeDtypeStruct((n_tok, d // 2), U32),
                   jax.ShapeDtypeStruct((n_tok, LANES), F32),
                   jax.ShapeDtypeStruct((n_tok, LANES), I32),
                   jax.ShapeDtypeStruct((n_tok, LANES), BF16)],
        compiler_params=_params(("arbitrary",)),
        name="mix_out",
    )(oa, ob, z, z, x2, w_proj_a.astype(BF16), w_proj_b.astype(BF16), w_out.astype(BF16),
      norm2_g.reshape(1, d).astype(F32), jnp.concatenate([rw_hi, rw_lo], axis=1), rw_hi, rb)


def _route_rank_kernel(row_block, sel_ref, idx_ref, tri_ref, dest_ref, cnt_ref, rank_ref, carry_ref, start_ref):
    p = pl.program_id(0)
    i = pl.program_id(1)
    tb = sel_ref.shape[0]
    rows = pl.ds(pl.multiple_of(i * tb, tb), tb)

    @pl.when((p == 0) & (i == 0))
    def _():
        carry_ref[...] = jnp.zeros_like(carry_ref)

    @pl.when(p == 0)
    def _():
        sel = sel_ref[...]
        before = jnp.dot(tri_ref[...], sel, preferred_element_type=F32)
        rank_ref[rows, :] = before + carry_ref[0:1, :]
        carry_ref[...] = carry_ref[...] + jnp.sum(sel.astype(F32), axis=0, keepdims=True)

    @pl.when((p == 1) & (i == 0))
    def _():
        counts = carry_ref[...]
        padded = jnp.ceil(counts * (1.0 / row_block)) * row_block
        lane = lax.broadcasted_iota(I32, padded.shape, 1)
        run = padded
        s = 1
        while s < LANES:
            run = run + jnp.where(lane >= s, pltpu.roll(run, s, 1), 0.0)
            s *= 2
        start_ref[...] = run - padded

    @pl.when(p == 1)
    def _():
        pos = rank_ref[rows, :] + start_ref[0:1, :]
        idx = idx_ref[...]
        lane = lax.broadcasted_iota(I32, pos.shape, 1)
        dest = jnp.zeros(pos.shape, F32)
        for k in range(TOP_K):
            hit = lane == idx[:, k:k + 1]
            val = jnp.sum(jnp.where(hit, pos, 0.0), axis=-1, keepdims=True)
            dest = jnp.where(lane == k, val, dest)
        dest_ref[...] = dest.astype(I32)
        cnt_ref[...] = carry_ref[...]


def _route_rank(sel, idx4, row_block):
    n_tok = sel.shape[0]
    tb = _divisor_tile(n_tok, 256)
    tri = (lax.broadcasted_iota(I32, (tb, tb), 0) > lax.broadcasted_iota(I32, (tb, tb), 1)).astype(BF16)
    return pl.pallas_call(
        functools.partial(_route_rank_kernel, row_block),
        grid=(2, n_tok // tb),
        in_specs=[pl.BlockSpec((tb, LANES), lambda p, i: (i, 0)),
                  pl.BlockSpec((tb, LANES), lambda p, i: (i, 0)),
                  pl.BlockSpec((tb, tb), lambda p, i: (0, 0))],
        out_specs=[pl.BlockSpec((tb, LANES), lambda p, i: (i * p, 0)),
                   pl.BlockSpec((SUBLANES, LANES), lambda p, i: (0, 0))],
        out_shape=[jax.ShapeDtypeStruct((n_tok, LANES), I32),
                   jax.ShapeDtypeStruct((SUBLANES, LANES), F32)],
        scratch_shapes=[pltpu.VMEM((n_tok, LANES), F32), pltpu.VMEM((SUBLANES, LANES), F32),
                        pltpu.VMEM((SUBLANES, LANES), F32)],
        compiler_params=_params(("arbitrary", "arbitrary")),
        name="route_rank",
    )(sel, idx4, tri)


def _dispatch_kernel(dest_ref, hn_ref, xs_in_ref, xs_ref, sem):
    del xs_in_ref
    tt = hn_ref.shape[0]
    for t in range(tt):
        for k in range(TOP_K):
            pltpu.make_async_copy(hn_ref.at[pl.ds(t, 1), :], xs_ref.at[pl.ds(dest_ref[TOP_K * t + k], 1), :],
                                  sem).start(priority=(TOP_K * t + k) % 2)
    for k in range(TOP_K):
        pltpu.make_async_copy(hn_ref, xs_ref.at[pl.ds(0, tt), :], sem).wait()


def _dispatch(dest_flat, hn_packed, n_rows):
    n_tok, half = hn_packed.shape
    tt = _divisor_tile(n_tok, 512)
    xs0 = jnp.zeros((n_rows, half), U32)
    return pl.pallas_call(
        _dispatch_kernel,
        grid=(n_tok // tt,),
        in_specs=[pl.BlockSpec((tt * TOP_K,), lambda i: (i,), memory_space=pltpu.SMEM),
                  pl.BlockSpec((tt, half), lambda i: (i, 0)),
                  pl.BlockSpec(memory_space=pl.ANY)],
        out_specs=pl.BlockSpec(memory_space=pl.ANY),
        out_shape=jax.ShapeDtypeStruct((n_rows, half), U32),
        scratch_shapes=[pltpu.SemaphoreType.DMA(())],
        input_output_aliases={2: 0},
        compiler_params=_params(("arbitrary",)),
        name="dispatch",
    )(dest_flat, hn_packed, xs0)


GROUP = 2 * LANES


def _regroup_kernel(w_ref, p_ref, o_ref):
    p = p_ref[...]
    for g in range(w_ref.shape[1] // GROUP):
        cols = slice(GROUP * g, GROUP * (g + 1))
        o_ref[:, cols] = jnp.dot(w_ref[:, cols].astype(BF16), p, preferred_element_type=F32).astype(BF16)


FFN_FF_TILE = 512


def _regroup_gate_up(w_gate_up):
    n_exp, d, two_f = w_gate_up.shape
    tr = _divisor_tile(d, 2048)
    tc = 2 * _divisor_tile(two_f // 2, FFN_FF_TILE)
    src = lax.broadcasted_iota(I32, (GROUP, GROUP), 0)
    dst = lax.broadcasted_iota(I32, (GROUP, GROUP), 1)
    perm = (dst == (src // 2) + LANES * (src % 2)).astype(BF16)
    return pl.pallas_call(
        _regroup_kernel,
        grid=(n_exp, d // tr, two_f // tc),
        in_specs=[pl.BlockSpec((None, tr, tc), lambda e, i, j: (e, i, j)),
                  pl.BlockSpec((GROUP, GROUP), lambda e, i, j: (0, 0))],
        out_specs=pl.BlockSpec((None, None, tr, tc), lambda e, i, j: (e, j, i, 0)),
        out_shape=jax.ShapeDtypeStruct((n_exp, two_f // tc, d, tc), BF16),
        compiler_params=_params(("arbitrary", "arbitrary", "arbitrary")),
        name="regroup_gate_up",
    )(w_gate_up, perm)


def _regroup_bias(b_gate_up):
    n_exp, two_f = b_gate_up.shape
    b = b_gate_up.reshape(n_exp, two_f // GROUP, LANES, 2)
    return jnp.swapaxes(b, 2, 3).reshape(n_exp, 1, two_f).astype(F32)


FFN_ROW_BLOCK = 1024
FFN_SUB_ROWS = 256


def _expert_ffn_kernel(be_ref, bv_ref, xs_ref, wgu_ref, wd_ref, bgu_ref, bd_ref,
                       out_ref, x_ref, acc_ref, wdb_ref):
    del be_ref
    b = pl.program_id(0)
    f = pl.program_id(1)
    n_f = pl.num_programs(1)
    n_rows = xs_ref.shape[0]
    sub = min(n_rows, FFN_SUB_ROWS)
    valid = bv_ref[b]
    n_sub = (valid + sub - 1) // sub

    @pl.when((valid > 0) & (f == 0))
    def _():
        hi, lo = _unpack_bf16_pairs(xs_ref[...])
        x_ref[...] = jnp.concatenate([hi.astype(BF16), lo.astype(BF16)], axis=1)
        acc_ref[...] = jnp.zeros_like(acc_ref)

    def ffn(rows, wd):
        gu = jnp.dot(x_ref[rows, :], wgu_ref[...], preferred_element_type=F32) + bgu_ref[...]
        acts = []
        for g in range(gu.shape[1] // GROUP):
            gate = jnp.minimum(gu[:, GROUP * g:GROUP * g + LANES], SWIGLU_LIMIT)
            up = jnp.clip(gu[:, GROUP * g + LANES:GROUP * (g + 1)], -SWIGLU_LIMIT, SWIGLU_LIMIT)
            acts.append(((up + 1.0) * (gate * jax.nn.sigmoid(SWIGLU_ALPHA * gate))).astype(BF16))
        act = jnp.concatenate(acts, axis=1)
        acc_ref[rows, :] += jnp.dot(act, wd, preferred_element_type=F32)

    @pl.when(n_sub == n_rows // sub)
    def _():
        ffn(pl.ds(0, n_rows), wd_ref[...].astype(BF16))

    @pl.when((n_sub > 0) & (n_sub < n_rows // sub))
    def _():
        wdb_ref[...] = wd_ref[...].astype(BF16)

        def body(s, carry):
            ffn(pl.ds(pl.multiple_of(s * sub, sub), sub), wdb_ref[...])
            return carry
        lax.fori_loop(0, n_sub, body, 0)

    @pl.when((valid > 0) & (f == n_f - 1))
    def _():
        out_ref[...] = _pack_bf16_pairs(acc_ref[...] + bd_ref[...])

    @pl.when((valid == 0) & (f == n_f - 1))
    def _():
        out_ref[...] = jnp.zeros_like(out_ref)


def _expert_ffn(block_e, block_valid, xs, wgu, wd, bgu, bd, row_block):
    n_rows, half = xs.shape
    d = 2 * half
    d_ff = wd.shape[1]
    tf = wgu.shape[3] // 2
    n_blocks = n_rows // row_block
    grid_spec = pltpu.PrefetchScalarGridSpec(
        num_scalar_prefetch=2,
        grid=(n_blocks, d_ff // tf),
        in_specs=[
            pl.BlockSpec((row_block, half), lambda b, f, be, bv: (b, 0)),
            pl.BlockSpec((None, None, d, 2 * tf), lambda b, f, be, bv: (be[b], f, 0, 0)),
            pl.BlockSpec((None, tf, d), lambda b, f, be, bv: (be[b], f, 0)),
            pl.BlockSpec((None, 1, 2 * tf), lambda b, f, be, bv: (be[b], 0, f)),
            pl.BlockSpec((None, 1, d), lambda b, f, be, bv: (be[b], 0, 0)),
        ],
        out_specs=pl.BlockSpec((row_block, half), lambda b, f, be, bv: (b, 0)),
        scratch_shapes=[pltpu.VMEM((row_block, d), BF16), pltpu.VMEM((row_block, d), F32),
                        pltpu.VMEM((tf, d), BF16)],
    )
    return pl.pallas_call(
        _expert_ffn_kernel,
        grid_spec=grid_spec,
        out_shape=jax.ShapeDtypeStruct((n_rows, half), U32),
        compiler_params=_params(("arbitrary", "arbitrary")),
        name="expert_ffn",
    )(block_e, block_valid, xs, wgu, wd, bgu, bd)


def _combine_kernel(cur_ref, nxt_ref, h_ref, w4_ref, fg_ref, rows_hbm_ref, o_ref, buf_ref, sem):
    i = pl.program_id(0)
    n = pl.num_programs(0)
    tt = h_ref.shape[0]
    half = buf_ref.shape[3]

    def issue(dest_ref, s):
        for t in range(tt):
            for k in range(TOP_K):
                pltpu.make_async_copy(rows_hbm_ref.at[pl.ds(dest_ref[TOP_K * t + k], 1), :],
                                      buf_ref.at[s, k, pl.ds(t, 1), :],
                                      sem.at[s]).start(priority=(TOP_K * t + k) % 2)

    def drain(s):
        for k in range(TOP_K):
            pltpu.make_async_copy(rows_hbm_ref.at[pl.ds(0, tt), :], buf_ref.at[s, k], sem.at[s]).wait()

    @pl.when(i == 0)
    def _():
        issue(cur_ref, 0)

    def step(slot):
        drain(slot)
        issue(nxt_ref, 1 - slot)
        w4 = w4_ref[...]
        y_hi = jnp.zeros((tt, half), F32)
        y_lo = jnp.zeros((tt, half), F32)
        for k in range(TOP_K):
            hi, lo = _unpack_bf16_pairs(buf_ref[slot, k])
            wk = w4[:, k:k + 1]
            y_hi = y_hi + wk * hi
            y_lo = y_lo + wk * lo
        h = h_ref[...] + jnp.concatenate([y_hi, y_lo], axis=1)
        o_ref[...] = _rms(h, fg_ref[...])

        @pl.when(i == n - 1)
        def _():
            drain(1 - slot)

    for slot in range(2):
        pl.when(i % 2 == slot)(functools.partial(step, slot))


def _combine(dest_flat, h, w4, final_g, rows_packed):
    n_tok, d = h.shape
    half = d // 2
    tt = _divisor_tile(n_tok, 256)
    n_steps = n_tok // tt
    return pl.pallas_call(
        _combine_kernel,
        grid=(n_steps,),
        in_specs=[pl.BlockSpec((tt * TOP_K,), lambda i: (i,), memory_space=pltpu.SMEM),
                  pl.BlockSpec((tt * TOP_K,), lambda i: (jnp.minimum(i + 1, n_steps - 1),),
                               memory_space=pltpu.SMEM),
                  pl.BlockSpec((tt, d), lambda i: (i, 0)),
                  pl.BlockSpec((tt, LANES), lambda i: (i, 0)),
                  pl.BlockSpec((1, d), lambda i: (0, 0)),
                  pl.BlockSpec(memory_space=pl.ANY)],
        out_specs=pl.BlockSpec((tt, d), lambda i: (i, 0)),
        out_shape=jax.ShapeDtypeStruct((n_tok, d), F32),
        scratch_shapes=[pltpu.VMEM((2, TOP_K, tt, half), U32), pltpu.SemaphoreType.DMA((2,))],
        compiler_params=_params(("arbitrary",)),
        name="combine",
    )(dest_flat, dest_flat, h, w4, final_g.reshape(1, d).astype(F32), rows_packed)


def _layer(h2, batch, seq, lb, norm1_g, w_in, hg_norm_g, gate_w2, gate_b, gla_norm_g, w_proj_a, w_proj_b,
           w_out, norm2_g, router_w, router_b, w_gate_up, b_gate_up, w_down, b_down, out_norm_g):
    n_tok, d = h2.shape
    hf = lb.shape[1]
    hv = w_proj_a.shape[0]
    gk = gate_w2.shape[2]
    gv = w_proj_b.shape[0]
    rank = gate_w2.shape[1]
    hg_heads, gla_heads = hf // HEAD_DK, gk // HEAD_DK
    n_exp = router_w.shape[1]

    z, gf, gb, gg = _in_proj(h2, norm1_g, w_in, lb, gate_w2, gate_b, (hf, hv, gk, gv, rank))
    a_cols = (0, hf, 2 * hf, 3 * hf, 3 * hf + hv)
    b0 = 3 * hf + 2 * hv
    b_cols = (b0, b0 + gk, b0 + gk, b0 + 2 * gk, b0 + 2 * gk + gv)
    oa = _recurrence(z, gf, gb, hg_norm_g, batch, seq, hg_heads, hv // hg_heads, a_cols,
                     name="hgrn_recurrence")
    ob = _recurrence(z, gg, gg, gla_norm_g, batch, seq, gla_heads, gv // gla_heads, b_cols,
                     gb_off=gla_heads, name="gla_recurrence")
    h, hn_packed, w4, idx4, sel = _mix_out(oa, ob, z, b0 + 2 * gk + 2 * gv, h2, w_proj_a, w_proj_b, w_out,
                                           norm2_g, router_w, router_b)

    n_pairs = n_tok * TOP_K
    row_block = _divisor_tile(n_pairs, FFN_ROW_BLOCK)
    n_blocks = n_pairs // row_block + n_exp
    dest4, counts = _route_rank(sel, idx4, row_block)
    dest_flat = dest4[:, :TOP_K].reshape(-1)
    cnt = counts[0, :n_exp].astype(I32)
    blocks_e = (cnt + row_block - 1) // row_block
    end_blk = jnp.cumsum(blocks_e)
    start_blk = end_blk - blocks_e
    bidx = jnp.arange(n_blocks, dtype=I32)[:, None]
    block_e = jnp.minimum(jnp.sum((end_blk[None, :] <= bidx).astype(I32), axis=1), n_exp - 1)
    owned = (bidx >= start_blk[None, :]) & (bidx < end_blk[None, :])
    rows_left = jnp.clip(cnt[None, :] - (bidx - start_blk[None, :]) * row_block, 0, row_block)
    block_valid = jnp.sum(jnp.where(owned, rows_left, 0), axis=1).astype(I32)

    xs = _dispatch(dest_flat, hn_packed, n_blocks * row_block)
    rows = _expert_ffn(block_e, block_valid, xs, _regroup_gate_up(w_gate_up), w_down.astype(F32),
                       _regroup_bias(b_gate_up), b_down[:, None, :].astype(F32), row_block)
    return _combine(dest_flat, h, w4, out_norm_g, rows)


def kernel(x, norm1_g, w_in, hg_lb_logits, hg_norm_g, gla_gate_w2, gla_gate_b, gla_norm_g, w_proj_a, w_proj_b,
           w_out, norm2_g, router_w, router_b, w_gate_up, b_gate_up, w_down, b_down, final_norm_g):
    batch, seq, d = x.shape
    depth = w_in.shape[0]
    assert depth == 1, "the final RMSNorm is fused into the last layer's combine kernel"
    lb_all = jnp.cumsum(jax.nn.softmax(hg_lb_logits.astype(F32), axis=1), axis=1)
    h2 = x.reshape(batch * seq, d)
    out = _layer(h2, batch, seq, lb_all[:, 0], norm1_g[0], w_in[0], hg_norm_g[0], gla_gate_w2[0], gla_gate_b[0],
                 gla_norm_g[0], w_proj_a[0], w_proj_b[0], w_out[0], norm2_g[0], router_w[0], router_b[0],
                 w_gate_up[0], b_gate_up[0], w_down[0], b_down[0], final_norm_g)
    return out.reshape(batch, seq, d)
```

```python
import functools

import jax
import jax.numpy as jnp
from jax import lax
from jax.experimental import pallas as pl
from jax.experimental.pallas import tpu as pltpu

F32 = jnp.float32
BF16 = jnp.bfloat16
U32 = jnp.uint32
I32 = jnp.int32

LANES = 128
SUBLANES = 8
VMEM_LIMIT_BYTES = 56 * 1024 * 1024

HEAD_DK = 128
GLA_DV = 256
HG_DV = 128
GATE_NORM = 16.0
TOP_K = 4
SWIGLU_LIMIT = 7.0
SWIGLU_ALPHA = 1.702
RMS_EPS = 1e-5
CHUNK = 64
NEG_BIG = -1e30
LOG2_E = 1.4426950408889634


def _divisor_tile(n, pref):
    t = min(n, pref)
    while n % t:
        t //= 2
    return t


def _params(sem, vmem=VMEM_LIMIT_BYTES):
    return pltpu.CompilerParams(dimension_semantics=sem, vmem_limit_bytes=vmem)


def _pack_bf16_pairs(a):
    n = a.shape[-1] // 2
    hi = lax.bitcast_convert_type(a[:, :n].astype(BF16).astype(F32), U32)
    lo = lax.bitcast_convert_type(a[:, n:].astype(BF16).astype(F32), U32)
    return hi | (lo >> 16)


def _unpack_bf16_pairs(p):
    hi = lax.bitcast_convert_type(p & jnp.uint32(0xFFFF0000), F32)
    lo = lax.bitcast_convert_type(p << 16, F32)
    return hi, lo


def _rms(x, g):
    ms = jnp.mean(x * x, axis=-1, keepdims=True)
    return x * lax.rsqrt(ms + RMS_EPS) * g


def _log_sigmoid(x):
    return jnp.minimum(x, 0.0) - jnp.log1p(jnp.exp(-jnp.abs(x)))


def _in_proj_kernel(bounds, x_ref, g1_ref, w_ref, lb_ref, w2_ref, b2_ref,
                    z_ref, gf_ref, gb_ref, gg_ref, xn_ref):
    j = pl.program_id(1)

    @pl.when(j == 0)
    def _():
        xn_ref[...] = _rms(x_ref[...], g1_ref[...]).astype(BF16)

    scale = HEAD_DK ** -0.5

    def project():
        return jnp.dot(xn_ref[...], w_ref[...], preferred_element_type=F32)

    def silu(z):
        return z * jax.nn.sigmoid(z)

    def store(fn):
        def epilogue():
            z_ref[...] = fn(project()).astype(BF16)
        return epilogue

    def forget(g_ref):
        def epilogue():
            lb = lb_ref[0:1, :]
            f = lb + (1.0 - lb) * jax.nn.sigmoid(project())
            z_ref[...] = (1.0 - f).astype(BF16)
            g_ref[...] = jnp.log2(f)
        return epilogue

    def low_rank_gate():
        lr = project()[:, :LANES].astype(BF16)
        logits = jnp.dot(lr, w2_ref[...], preferred_element_type=F32) + b2_ref[...]
        gg_ref[...] = _log_sigmoid(logits) * (LOG2_E / GATE_NORM)

    epilogues = [
        store(lambda z: silu(z) * scale),
        forget(gf_ref),
        forget(gb_ref),
        store(lambda z: z),
        store(silu),
        store(lambda z: z * scale),
        store(lambda z: z),
        store(lambda z: z),
        store(silu),
        store(jax.nn.sigmoid),
        store(jax.nn.sigmoid),
        low_rank_gate,
    ]
    for k, epilogue in enumerate(epilogues):
        pl.when((j >= bounds[k]) & (j < bounds[k + 1]))(epilogue)


def _in_proj(x2, norm1_g, w_in, lb, gate_w2, gate_b, dims):
    n_tok, d = x2.shape
    hf, hv, gk, gv, rank = dims
    widths = [hf, hf, hf, hv, hv, gk, gk, gv, gv]
    off_lr = sum(widths)
    n_main = off_lr + 2 * d
    tn = _divisor_tile(gk, 256)
    tm = _divisor_tile(n_tok, 1024)
    widths_all = widths + [d, d, tn]
    w_lr = jnp.pad(w_in[:, off_lr:off_lr + 2 * rank], ((0, 0), (0, tn - 2 * rank)))
    w_cat = jnp.concatenate([w_in[:, :off_lr], w_in[:, off_lr + 2 * rank:], w_lr], axis=1).astype(BF16)
    n_col = w_cat.shape[1]
    w_tiles = w_cat.reshape(d, n_col // tn, tn).transpose(1, 0, 2)
    lb_tab = jnp.zeros((SUBLANES, n_col), F32)
    lb_tab = lb_tab.at[0, hf:2 * hf].set(lb[0]).at[0, 2 * hf:3 * hf].set(lb[1])
    w2 = jnp.zeros((LANES, 2 * gk), F32)
    w2 = w2.at[:rank, :gk].set(gate_w2[0]).at[rank:2 * rank, gk:].set(gate_w2[1]).astype(BF16)
    b2 = jnp.concatenate([gate_b[0], gate_b[1]])[None, :].astype(F32)

    bounds = [0]
    for w in widths_all:
        bounds.append(bounds[-1] + w // tn)
    n_j = bounds[-1]

    def seg_spec(k):
        s, n = bounds[k], bounds[k + 1] - bounds[k]
        return pl.BlockSpec((tm, tn), lambda i, j, s=s, n=n: (i, jnp.clip(j - s, 0, n - 1)))

    return pl.pallas_call(
        functools.partial(_in_proj_kernel, tuple(bounds)),
        grid=(n_tok // tm, n_j),
        in_specs=[
            pl.BlockSpec((tm, d), lambda i, j: (i, 0)),
            pl.BlockSpec((1, d), lambda i, j: (0, 0)),
            pl.BlockSpec((None, d, tn), lambda i, j: (j, 0, 0)),
            pl.BlockSpec((SUBLANES, tn), lambda i, j: (0, j)),
            pl.BlockSpec((LANES, 2 * gk), lambda i, j: (0, 0)),
            pl.BlockSpec((1, 2 * gk), lambda i, j: (0, 0)),
        ],
        out_specs=[pl.BlockSpec((tm, tn), lambda i, j: (i, jnp.minimum(j, n_main // tn - 1))),
                   seg_spec(1), seg_spec(2),
                   pl.BlockSpec((tm, 2 * gk), lambda i, j: (i, 0))],
        out_shape=[jax.ShapeDtypeStruct((n_tok, n_main), BF16),
                   jax.ShapeDtypeStruct((n_tok, hf), F32),
                   jax.ShapeDtypeStruct((n_tok, hf), F32),
                   jax.ShapeDtypeStruct((n_tok, 2 * gk), F32)],
        scratch_shapes=[pltpu.VMEM((tm, d), BF16)],
        compiler_params=_params(("arbitrary", "arbitrary")),
        name="in_proj",
    )(x2, norm1_g.reshape(1, d), w_tiles, lb_tab, w2, b2)


_NT_DIMS = (((1,), (1,)), ((), ()))
_TN_DIMS = (((0,), (0,)), ((), ()))


def _chunk_prepare(q, k, g, rev):
    nt = CHUNK // SUBLANES
    row = lax.broadcasted_iota(I32, (SUBLANES, LANES), 0)
    lrow = (SUBLANES - 1 - row) if rev else row

    def phys(jl):
        return nt - 1 - jl if rev else jl

    def prow(x, rl):
        r = SUBLANES - 1 - rl if rev else rl
        return x[r:r + 1, :]

    def tile(x, jl):
        p = phys(jl)
        return x[SUBLANES * p:SUBLANES * (p + 1), :]

    def scan(x):
        for s in (1, 2, 4):
            if rev:
                x = x + jnp.where(row < SUBLANES - s, pltpu.roll(x, SUBLANES - s, 0), 0.0)
            else:
                x = x + jnp.where(row >= s, pltpu.roll(x, s, 0), 0.0)
        return x

    gt = [tile(g, jl) for jl in range(nt)]
    qt = [tile(q, jl) for jl in range(nt)]
    kt = [tile(k, jl) for jl in range(nt)]
    ct = [scan(x) for x in gt]
    tot = [prow(c, SUBLANES - 1) for c in ct]
    suf = [t - c for t, c in zip(tot, ct)]

    def assemble(tiles):
        out = [None] * nt
        for jl, x in enumerate(tiles):
            out[phys(jl)] = jnp.zeros((SUBLANES, LANES), F32) if x is None else x
        return jnp.concatenate(out, axis=0).astype(BF16)

    def plus(a, b):
        return a if b is None else a + b

    levels = []

    for nb in (8, 4, 2):
        half = nb // 2
        qs, ks = [None] * nt, [None] * nt
        for base in range(0, nt, nb):
            acc = None
            for jl in range(base + half, base + nb):
                qs[jl] = qt[jl] * jnp.exp2(plus(ct[jl], acc))
                acc = plus(tot[jl], acc)
            acc = None
            for jl in range(base + half - 1, base - 1, -1):
                ks[jl] = kt[jl] * jnp.exp2(plus(suf[jl], acc))
                acc = plus(tot[jl], acc)
        levels.append((SUBLANES * nb, assemble(qs), assemble(ks)))

    for size in (8, 4):
        half = size // 2
        is_q = (lrow & (size - 1)) >= half
        qs, ks = [], []
        for jl in range(nt):
            if size == 8:
                ref = prow(ct[jl], half - 1)
            else:
                ref = jnp.where(lrow >= size, prow(ct[jl], size + half - 1), prow(ct[jl], half - 1))
            d = ct[jl] - ref
            e = jnp.exp2(jnp.where(is_q, d, -d))
            qs.append(jnp.where(is_q, qt[jl] * e, 0.0))
            ks.append(jnp.where(is_q, 0.0, kt[jl] * e))
        levels.append((size, assemble(qs), assemble(ks)))

    odd = (lrow & 1) == 1
    qs = [jnp.where(odd, qt[jl] * jnp.exp2(gt[jl]), 0.0) for jl in range(nt)]
    ks = [jnp.where(odd, 0.0, kt[jl]) for jl in range(nt)]
    levels.append((2, assemble(qs), assemble(ks)))
    levels.append((1, q.astype(BF16), k.astype(BF16)))

    ti = lax.broadcasted_iota(I32, (CHUNK, CHUNK), 0)
    si = lax.broadcasted_iota(I32, (CHUNK, CHUNK), 1)
    blk = ti ^ si
    scores = None
    for size, qm, km in levels:
        s_m = lax.dot_general(qm, km, _NT_DIMS, preferred_element_type=F32)
        scores = s_m if scores is None else jnp.where(blk < size, s_m, scores)

    q_in, k_out = [None] * nt, [None] * nt
    acc = None
    for jl in range(nt):
        q_in[jl] = qt[jl] * jnp.exp2(plus(ct[jl], acc))
        acc = plus(tot[jl], acc)
    total = acc
    acc = None
    for jl in range(nt - 1, -1, -1):
        k_out[jl] = kt[jl] * jnp.exp2(plus(suf[jl], acc))
        acc = plus(tot[jl], acc)
    return scores.astype(BF16), assemble(q_in), assemble(k_out), jnp.exp2(total)


def _chunk_apply(scores, q_in, k_out, decay, v, st_ref):
    st = st_ref[...]
    out = (jnp.dot(scores, v, preferred_element_type=F32)
           + lax.dot_general(q_in, st.astype(BF16), _NT_DIMS, preferred_element_type=F32))
    st_ref[...] = st * decay + lax.dot_general(v, k_out, _TN_DIMS, preferred_element_type=F32)
    return out


def _recurrence_kernel(n_chunks, dv, q_ref, kf_ref, kb_ref, v_ref, gf_ref, gb_ref, og_ref, ng_ref,
                       o_ref, of_ref, ob_ref, st_ref, sc_ref, qin_ref, kout_ref, dec_ref):
    n_heads = q_ref.shape[1] // HEAD_DK
    lanes = [(rev, k_ref, g_ref, out_ref, hd)
             for rev, k_ref, g_ref, out_ref in ((False, kf_ref, gf_ref, of_ref), (True, kb_ref, gb_ref, ob_ref))
             for hd in range(n_heads)]
    st_ref[...] = jnp.zeros_like(st_ref)

    def rows_of(c, rev):
        cc = (n_chunks - 1 - c) if rev else c
        return pl.ds(pl.multiple_of(cc * CHUNK, CHUNK), CHUNK)

    def kcols(hd):
        return slice(HEAD_DK * hd, HEAD_DK * (hd + 1))

    def vcols(hd):
        return slice(dv * hd, dv * (hd + 1))

    def prepare(c):
        res = []
        for rev, k_ref, g_ref, _, hd in lanes:
            rows = rows_of(c, rev)
            res.append(_chunk_prepare(q_ref[rows, kcols(hd)].astype(F32), k_ref[rows, kcols(hd)].astype(F32),
                                      g_ref[rows, kcols(hd)], rev))
        return res

    def stash(res):
        for d, (scores, q_in, k_out, decay) in enumerate(res):
            sc_ref[d] = scores
            qin_ref[d] = q_in
            kout_ref[d] = k_out
            dec_ref[d] = jnp.broadcast_to(decay, (SUBLANES, LANES))

    def unstash():
        return [(sc_ref[d], qin_ref[d], kout_ref[d], dec_ref[d][0:1, :]) for d in range(len(lanes))]

    def apply(c, staged):
        for d, ((rev, _, _, out_ref, hd), (scores, q_in, k_out, decay)) in enumerate(zip(lanes, staged)):
            rows = rows_of(c, rev)
            out_ref[rows, vcols(hd)] = _chunk_apply(scores, q_in, k_out, decay, v_ref[rows, vcols(hd)],
                                                    st_ref.at[d])

    stash(prepare(0))

    def body(c, carry):
        staged = unstash()
        nxt = prepare(c + 1)
        apply(c, staged)
        stash(nxt)
        return carry

    lax.fori_loop(0, n_chunks - 1, body, 0)
    apply(n_chunks - 1, unstash())
    for hd in range(n_heads):
        o = of_ref[:, vcols(hd)] + ob_ref[:, vcols(hd)]
        o_ref[:, vcols(hd)] = (_rms(o, ng_ref[...]) * og_ref[:, vcols(hd)].astype(F32)).astype(BF16)


RECURRENCE_HEADS_PER_STEP = 2


def _recurrence(z, gf, gb, norm_g, batch, seq, heads, dv, cols, gb_off=0, name=""):
    n_tok = batch * seq
    n_chunks = seq // CHUNK
    q_col, kf_col, kb_col, v_col, og_col = cols
    hp = RECURRENCE_HEADS_PER_STEP if heads % RECURRENCE_HEADS_PER_STEP == 0 else 1

    def spec(width, col=0):
        width = width * hp
        assert col % width == 0
        return pl.BlockSpec((seq, width), lambda b, h, off=col // width: (b, h + off))

    return pl.pallas_call(
        functools.partial(_recurrence_kernel, n_chunks, dv),
        grid=(batch, heads // hp),
        in_specs=[spec(HEAD_DK, q_col), spec(HEAD_DK, kf_col), spec(HEAD_DK, kb_col), spec(dv, v_col),
                  spec(HEAD_DK), spec(HEAD_DK, gb_off * HEAD_DK), spec(dv, og_col),
                  pl.BlockSpec((1, dv), lambda b, h: (0, 0))],
        out_specs=spec(dv),
        out_shape=jax.ShapeDtypeStruct((n_tok, heads * dv), BF16),
        scratch_shapes=[pltpu.VMEM((seq, hp * dv), F32), pltpu.VMEM((seq, hp * dv), F32),
                        pltpu.VMEM((2 * hp, dv, HEAD_DK), F32),
                        pltpu.VMEM((2 * hp, CHUNK, CHUNK), BF16), pltpu.VMEM((2 * hp, CHUNK, HEAD_DK), BF16),
                        pltpu.VMEM((2 * hp, CHUNK, HEAD_DK), BF16), pltpu.VMEM((2 * hp, SUBLANES, LANES), F32)],
        compiler_params=_params(("arbitrary", "arbitrary")),
        name=name,
    )(z, z, z, z, gf, gb, z, norm_g.reshape(1, dv).astype(F32))


MIX_SUB_ROWS = 256


def _mix_out_kernel(oa_ref, ob_ref, sga_ref, sgb_ref, x_ref, wpa_ref, wpb_ref, wo_ref, g2_ref, rw2_ref, rw1_ref,
                    rb_ref, h_ref, hn_ref, w4_ref, idx_ref, sel_ref):
    tm = x_ref.shape[0]
    sub = min(tm, MIX_SUB_ROWS)
    for r in range(tm // sub):
        _mix_out_rows(pl.ds(r * sub, sub), oa_ref, ob_ref, sga_ref, sgb_ref, x_ref, wpa_ref, wpb_ref, wo_ref,
                      g2_ref, rw2_ref, rw1_ref, rb_ref, h_ref, hn_ref, w4_ref, idx_ref, sel_ref)


def _mix_out_rows(rows, oa_ref, ob_ref, sga_ref, sgb_ref, x_ref, wpa_ref, wpb_ref, wo_ref, g2_ref, rw2_ref,
                  rw1_ref, rb_ref, h_ref, hn_ref, w4_ref, idx_ref, sel_ref):
    ya = jnp.dot(oa_ref[rows, :], wpa_ref[...], preferred_element_type=F32)
    yb = jnp.dot(ob_ref[rows, :], wpb_ref[...], preferred_element_type=F32)
    merged = sga_ref[rows, :].astype(F32) * ya + sgb_ref[rows, :].astype(F32) * yb
    h = x_ref[rows, :] + jnp.dot(merged.astype(BF16), wo_ref[...], preferred_element_type=F32)
    h_ref[rows, :] = h
    hn = _rms(h, g2_ref[...])
    hn_ref[rows, :] = _pack_bf16_pairs(hn)

    hn_hi = hn.astype(BF16)
    hn_lo = (hn - hn_hi.astype(F32)).astype(BF16)
    two = jnp.dot(hn_hi, rw2_ref[...], preferred_element_type=F32)
    logits = (two[:, :LANES] + two[:, LANES:] + jnp.dot(hn_lo, rw1_ref[...], preferred_element_type=F32)
              + rb_ref[...])
    lane = lax.broadcasted_iota(I32, logits.shape, 1)
    lane_f = lane.astype(F32)
    work = logits
    vals, idxs = [], []
    for _ in range(TOP_K):
        m = jnp.max(work, axis=-1, keepdims=True)
        idx = jnp.min(jnp.where(work == m, lane_f, float(LANES)), axis=-1, keepdims=True)
        vals.append(m)
        idxs.append(idx)
        work = jnp.where(lane_f == idx, -jnp.inf, work)
    es = [jnp.exp(v - vals[0]) for v in vals]
    denom = es[0]
    for e in es[1:]:
        denom = denom + e
    w4 = jnp.zeros(logits.shape, F32)
    i4 = jnp.zeros(logits.shape, F32)
    sel = jnp.zeros(logits.shape, F32)
    for k in range(TOP_K):
        w4 = jnp.where(lane == k, es[k] / denom, w4)
        i4 = jnp.where(lane == k, idxs[k], i4)
        sel = jnp.where(lane_f == idxs[k], 1.0, sel)
    w4_ref[rows, :] = w4
    idx_ref[rows, :] = i4.astype(I32)
    sel_ref[rows, :] = sel.astype(BF16)


def _mix_out(oa, ob, z, gate_col, x2, w_proj_a, w_proj_b, w_out, norm2_g, router_w, router_b):
    n_tok, d = x2.shape
    hv, gv = oa.shape[1], ob.shape[1]
    n_exp = router_w.shape[1]
    tm = _divisor_tile(n_tok, 2 * MIX_SUB_ROWS)
    rw = jnp.pad(router_w.astype(F32), ((0, 0), (0, LANES - n_exp)))
    rw_hi = rw.astype(BF16)
    rw_lo = (rw - rw_hi.astype(F32)).astype(BF16)
    rb = jnp.concatenate([router_b.astype(F32), jnp.full((LANES - n_exp,), NEG_BIG, F32)])[None, :]
    assert gate_col % d == 0
    gate_blk = gate_col // d

    def rows(width, off=0):
        return pl.BlockSpec((tm, width), lambda i, off=off: (i, off))

    def whole(r, c):
        return pl.BlockSpec((r, c), lambda i: (0, 0), pipeline_mode=pl.Buffered(1))

    return pl.pallas_call(
        _mix_out_kernel,
        grid=(n_tok // tm,),
        in_specs=[rows(hv), rows(gv), rows(d, gate_blk), rows(d, gate_blk + 1), rows(d),
                  whole(hv, d), whole(gv, d), whole(d, d), whole(1, d), whole(d, 2 * LANES), whole(d, LANES),
                  whole(1, LANES)],
        out_specs=[rows(d), rows(d // 2), rows(LANES), rows(LANES), rows(LANES)],
        out_shape=[jax.ShapeDtypeStruct((n_tok, d), F32),
                   jax.ShapeDtypeStruct((n_tok, d // 2), U32),
                   jax.ShapeDtypeStruct((n_tok, LANES), F32),
                   jax.ShapeDtypeStruct((n_tok, LANES), I32),
                   jax.ShapeDtypeStruct((n_tok, LANES), BF16)],
        compiler_params=_params(("arbitrary",)),
        name="mix_out",
    )(oa, ob, z, z, x2, w_proj_a.astype(BF16), w_proj_b.astype(BF16), w_out.astype(BF16),
      norm2_g.reshape(1, d).astype(F32), jnp.concatenate([rw_hi, rw_lo], axis=1), rw_hi, rb)


def _route_rank_kernel(row_block, sel_ref, idx_ref, tri_ref, dest_ref, cnt_ref, rank_ref, carry_ref, start_ref):
    p = pl.program_id(0)
    i = pl.program_id(1)
    tb = sel_ref.shape[0]
    rows = pl.ds(pl.multiple_of(i * tb, tb), tb)

    @pl.when((p == 0) & (i == 0))
    def _():
        carry_ref[...] = jnp.zeros_like(carry_ref)

    @pl.when(p == 0)
    def _():
        sel = sel_ref[...]
        before = jnp.dot(tri_ref[...], sel, preferred_element_type=F32)
        rank_ref[rows, :] = before + carry_ref[0:1, :]
        carry_ref[...] = carry_ref[...] + jnp.sum(sel.astype(F32), axis=0, keepdims=True)

    @pl.when((p == 1) & (i == 0))
    def _():
        counts = carry_ref[...]
        padded = jnp.ceil(counts * (1.0 / row_block)) * row_block
        lane = lax.broadcasted_iota(I32, padded.shape, 1)
        run = padded
        s = 1
        while s < LANES:
            run = run + jnp.where(lane >= s, pltpu.roll(run, s, 1), 0.0)
            s *= 2
        start_ref[...] = run - padded

    @pl.when(p == 1)
    def _():
        pos = rank_ref[rows, :] + start_ref[0:1, :]
        idx = idx_ref[...]
        lane = lax.broadcasted_iota(I32, pos.shape, 1)
        dest = jnp.zeros(pos.shape, F32)
        for k in range(TOP_K):
            hit = lane == idx[:, k:k + 1]
            val = jnp.sum(jnp.where(hit, pos, 0.0), axis=-1, keepdims=True)
            dest = jnp.where(lane == k, val, dest)
        dest_ref[...] = dest.astype(I32)
        cnt_ref[...] = carry_ref[...]


def _route_rank(sel, idx4, row_block):
    n_tok = sel.shape[0]
    tb = _divisor_tile(n_tok, 256)
    tri = (lax.broadcasted_iota(I32, (tb, tb), 0) > lax.broadcasted_iota(I32, (tb, tb), 1)).astype(BF16)
    return pl.pallas_call(
        functools.partial(_route_rank_kernel, row_block),
        grid=(2, n_tok // tb),
        in_specs=[pl.BlockSpec((tb, LANES), lambda p, i: (i, 0)),
                  pl.BlockSpec((tb, LANES), lambda p, i: (i, 0)),
                  pl.BlockSpec((tb, tb), lambda p, i: (0, 0))],
        out_specs=[pl.BlockSpec((tb, LANES), lambda p, i: (i * p, 0)),
                   pl.BlockSpec((SUBLANES, LANES), lambda p, i: (0, 0))],
        out_shape=[jax.ShapeDtypeStruct((n_tok, LANES), I32),
                   jax.ShapeDtypeStruct((SUBLANES, LANES), F32)],
        scratch_shapes=[pltpu.VMEM((n_tok, LANES), F32), pltpu.VMEM((SUBLANES, LANES), F32),
                        pltpu.VMEM((SUBLANES, LANES), F32)],
        compiler_params=_params(("arbitrary", "arbitrary")),
        name="route_rank",
    )(sel, idx4, tri)


def _dispatch_kernel(bv_ref, dest_ref, hn_ref, xs_ref, zero_ref, sem):
    tt = hn_ref.shape[0]
    row_block = zero_ref.shape[0]
    n_blocks = bv_ref.shape[0]

    @pl.when(pl.program_id(0) == 0)
    def _():
        zero_ref[...] = jnp.zeros_like(zero_ref)

        def zero_copy(blk):
            return pltpu.make_async_copy(zero_ref, xs_ref.at[pl.ds(blk * row_block, row_block), :], sem)

        for blk in range(n_blocks):
            pl.when(bv_ref[blk] < row_block)(zero_copy(blk).start)
        for blk in range(n_blocks):
            pl.when(bv_ref[blk] < row_block)(zero_copy(blk).wait)

    for t in range(tt):
        for k in range(TOP_K):
            pltpu.make_async_copy(hn_ref.at[pl.ds(t, 1), :], xs_ref.at[pl.ds(dest_ref[TOP_K * t + k], 1), :],
                                  sem).start(priority=(TOP_K * t + k) % 2)
    for k in range(TOP_K):
        pltpu.make_async_copy(hn_ref, xs_ref.at[pl.ds(0, tt), :], sem).wait()


def _dispatch(block_valid, dest_flat, hn_packed, row_block):
    n_tok, half = hn_packed.shape
    tt = _divisor_tile(n_tok, 512)
    n_rows = block_valid.shape[0] * row_block
    grid_spec = pltpu.PrefetchScalarGridSpec(
        num_scalar_prefetch=1,
        grid=(n_tok // tt,),
        in_specs=[pl.BlockSpec((tt * TOP_K,), lambda i, bv: (i,), memory_space=pltpu.SMEM),
                  pl.BlockSpec((tt, half), lambda i, bv: (i, 0))],
        out_specs=pl.BlockSpec(memory_space=pl.ANY),
        scratch_shapes=[pltpu.VMEM((row_block, half), U32), pltpu.SemaphoreType.DMA(())],
    )
    return pl.pallas_call(
        _dispatch_kernel,
        grid_spec=grid_spec,
        out_shape=jax.ShapeDtypeStruct((n_rows, half), U32),
        compiler_params=_params(("arbitrary",)),
        name="dispatch",
    )(block_valid, dest_flat, hn_packed)


GROUP = 2 * LANES


def _regroup_kernel(w_ref, p_ref, o_ref):
    p = p_ref[...]
    for g in range(w_ref.shape[1] // GROUP):
        cols = slice(GROUP * g, GROUP * (g + 1))
        o_ref[:, cols] = jnp.dot(w_ref[:, cols].astype(BF16), p, preferred_element_type=F32).astype(BF16)


FFN_FF_TILE = 512


def _regroup_gate_up(w_gate_up):
    n_exp, d, two_f = w_gate_up.shape
    tr = _divisor_tile(d, 2048)
    tc = 2 * _divisor_tile(two_f // 2, FFN_FF_TILE)
    src = lax.broadcasted_iota(I32, (GROUP, GROUP), 0)
    dst = lax.broadcasted_iota(I32, (GROUP, GROUP), 1)
    perm = (dst == (src // 2) + LANES * (src % 2)).astype(BF16)
    return pl.pallas_call(
        _regroup_kernel,
        grid=(n_exp, d // tr, two_f // tc),
        in_specs=[pl.BlockSpec((None, tr, tc), lambda e, i, j: (e, i, j)),
                  pl.BlockSpec((GROUP, GROUP), lambda e, i, j: (0, 0))],
        out_specs=pl.BlockSpec((None, None, tr, tc), lambda e, i, j: (e, j, i, 0)),
        out_shape=jax.ShapeDtypeStruct((n_exp, two_f // tc, d, tc), BF16),
        compiler_params=_params(("arbitrary", "arbitrary", "arbitrary")),
        name="regroup_gate_up",
    )(w_gate_up, perm)


def _regroup_bias(b_gate_up):
    n_exp, two_f = b_gate_up.shape
    b = b_gate_up.reshape(n_exp, two_f // GROUP, LANES, 2)
    return jnp.swapaxes(b, 2, 3).reshape(n_exp, 1, two_f).astype(F32)


FFN_ROW_BLOCK = 1024
FFN_SUB_ROWS = 256


def _expert_ffn_kernel(n_f, be_ref, bv_ref, xs_ref, wgu_ref, wd_ref, bgu_ref, bd_ref,
                       out_ref, x_ref, acc_ref, wdb_ref):
    del be_ref
    b = pl.program_id(0)
    f = pl.program_id(1)
    n_rows = xs_ref.shape[0]
    sub = min(n_rows, FFN_SUB_ROWS)
    valid = bv_ref[b]
    n_sub = (valid + sub - 1) // sub
    full = n_sub == n_rows // sub
    partial = (n_sub > 0) & jnp.logical_not(full)
    first = f == 0
    last = f == n_f - 1

    def unpack():
        hi, lo = _unpack_bf16_pairs(xs_ref[...])
        return jnp.concatenate([hi.astype(BF16), lo.astype(BF16)], axis=1)

    def down(x, wd):
        gu = jnp.dot(x, wgu_ref[...], preferred_element_type=F32) + bgu_ref[...]
        acts = []
        for g in range(gu.shape[1] // GROUP):
            gate = jnp.minimum(gu[:, GROUP * g:GROUP * g + LANES], SWIGLU_LIMIT)
            up = jnp.clip(gu[:, GROUP * g + LANES:GROUP * (g + 1)], -SWIGLU_LIMIT, SWIGLU_LIMIT)
            acts.append(((up + 1.0) * (gate * jax.nn.sigmoid(SWIGLU_ALPHA * gate))).astype(BF16))
        return jnp.dot(jnp.concatenate(acts, axis=1), wd, preferred_element_type=F32)

    def full_step(is_first, is_last):
        def run():
            wd = wd_ref[...].astype(BF16)
            if is_first:
                x = unpack()
                x_ref[...] = x
                y = down(x, wd)
            else:
                y = acc_ref[...] + down(x_ref[...], wd)
            if is_last:
                out_ref[...] = _pack_bf16_pairs(y + bd_ref[...])
            else:
                acc_ref[...] = y
        return run

    if n_f == 1:
        pl.when(full)(full_step(True, True))
    else:
        pl.when(full & first)(full_step(True, False))
        pl.when(full & last)(full_step(False, True))
        if n_f > 2:
            pl.when(full & jnp.logical_not(first | last))(full_step(False, False))

    @pl.when(partial & first)
    def _():
        x_ref[...] = unpack()
        acc_ref[...] = jnp.zeros_like(acc_ref)

    @pl.when(partial)
    def _():
        wdb_ref[...] = wd_ref[...].astype(BF16)

        def body(s, carry):
            rows = pl.ds(pl.multiple_of(s * sub, sub), sub)
            acc_ref[rows, :] += down(x_ref[rows, :], wdb_ref[...])
            return carry
        lax.fori_loop(0, n_sub, body, 0)

    @pl.when(partial & last)
    def _():
        out_ref[...] = _pack_bf16_pairs(acc_ref[...] + bd_ref[...])

    @pl.when((valid == 0) & last)
    def _():
        out_ref[...] = jnp.zeros_like(out_ref)


def _expert_ffn(block_e, block_valid, xs, wgu, wd, bgu, bd, row_block):
    n_rows, half = xs.shape
    d = 2 * half
    d_ff = wd.shape[1]
    tf = wgu.shape[3] // 2
    n_blocks = n_rows // row_block
    grid_spec = pltpu.PrefetchScalarGridSpec(
        num_scalar_prefetch=2,
        grid=(n_blocks, d_ff // tf),
        in_specs=[
            pl.BlockSpec((row_block, half), lambda b, f, be, bv: (b, 0)),
            pl.BlockSpec((None, None, d, 2 * tf), lambda b, f, be, bv: (be[b], f, 0, 0)),
            pl.BlockSpec((None, tf, d), lambda b, f, be, bv: (be[b], f, 0)),
            pl.BlockSpec((None, 1, 2 * tf), lambda b, f, be, bv: (be[b], 0, f)),
            pl.BlockSpec((None, 1, d), lambda b, f, be, bv: (be[b], 0, 0)),
        ],
        out_specs=pl.BlockSpec((row_block, half), lambda b, f, be, bv: (b, 0)),
        scratch_shapes=[pltpu.VMEM((row_block, d), BF16), pltpu.VMEM((row_block, d), F32),
                        pltpu.VMEM((tf, d), BF16)],
    )
    return pl.pallas_call(
        functools.partial(_expert_ffn_kernel, d_ff // tf),
        grid_spec=grid_spec,
        out_shape=jax.ShapeDtypeStruct((n_rows, half), U32),
        compiler_params=_params(("arbitrary", "arbitrary")),
        name="expert_ffn",
    )(block_e, block_valid, xs, wgu, wd, bgu, bd)


def _combine_kernel(cur_ref, nxt_ref, h_ref, w4_ref, fg_ref, rows_hbm_ref, o_ref, buf_ref, sem):
    i = pl.program_id(0)
    n = pl.num_programs(0)
    tt = h_ref.shape[0]
    half = buf_ref.shape[3]

    def issue(dest_ref, s):
        for t in range(tt):
            for k in range(TOP_K):
                pltpu.make_async_copy(rows_hbm_ref.at[pl.ds(dest_ref[TOP_K * t + k], 1), :],
                                      buf_ref.at[s, k, pl.ds(t, 1), :],
                                      sem.at[s]).start(priority=(TOP_K * t + k) % 2)

    def drain(s):
        for k in range(TOP_K):
            pltpu.make_async_copy(rows_hbm_ref.at[pl.ds(0, tt), :], buf_ref.at[s, k], sem.at[s]).wait()

    @pl.when(i == 0)
    def _():
        issue(cur_ref, 0)

    def step(slot):
        drain(slot)
        issue(nxt_ref, 1 - slot)
        w4 = w4_ref[...]
        y_hi = jnp.zeros((tt, half), F32)
        y_lo = jnp.zeros((tt, half), F32)
        for k in range(TOP_K):
            hi, lo = _unpack_bf16_pairs(buf_ref[slot, k])
            wk = w4[:, k:k + 1]
            y_hi = y_hi + wk * hi
            y_lo = y_lo + wk * lo
        h = h_ref[...] + jnp.concatenate([y_hi, y_lo], axis=1)
        o_ref[...] = _rms(h, fg_ref[...])

        @pl.when(i == n - 1)
        def _():
            drain(1 - slot)

    for slot in range(2):
        pl.when(i % 2 == slot)(functools.partial(step, slot))


def _combine(dest_flat, h, w4, final_g, rows_packed):
    n_tok, d = h.shape
    half = d // 2
    tt = _divisor_tile(n_tok, 256)
    n_steps = n_tok // tt
    return pl.pallas_call(
        _combine_kernel,
        grid=(n_steps,),
        in_specs=[pl.BlockSpec((tt * TOP_K,), lambda i: (i,), memory_space=pltpu.SMEM),
                  pl.BlockSpec((tt * TOP_K,), lambda i: (jnp.minimum(i + 1, n_steps - 1),),
                               memory_space=pltpu.SMEM),
                  pl.BlockSpec((tt, d), lambda i: (i, 0)),
                  pl.BlockSpec((tt, LANES), lambda i: (i, 0)),
                  pl.BlockSpec((1, d), lambda i: (0, 0)),
                  pl.BlockSpec(memory_space=pl.ANY)],
        out_specs=pl.BlockSpec((tt, d), lambda i: (i, 0)),
        out_shape=jax.ShapeDtypeStruct((n_tok, d), F32),
        scratch_shapes=[pltpu.VMEM((2, TOP_K, tt, half), U32), pltpu.SemaphoreType.DMA((2,))],
        compiler_params=_params(("arbitrary",)),
        name="combine",
    )(dest_flat, dest_flat, h, w4, final_g.reshape(1, d).astype(F32), rows_packed)


def _layer(h2, batch, seq, lb, norm1_g, w_in, hg_norm_g, gate_w2, gate_b, gla_norm_g, w_proj_a, w_proj_b,
           w_out, norm2_g, router_w, router_b, w_gate_up, b_gate_up, w_down, b_down, out_norm_g):
    n_tok, d = h2.shape
    hf = lb.shape[1]
    hv = w_proj_a.shape[0]
    gk = gate_w2.shape[2]
    gv = w_proj_b.shape[0]
    rank = gate_w2.shape[1]
    hg_heads, gla_heads = hf // HEAD_DK, gk // HEAD_DK
    n_exp = router_w.shape[1]

    z, gf, gb, gg = _in_proj(h2, norm1_g, w_in, lb, gate_w2, gate_b, (hf, hv, gk, gv, rank))
    a_cols = (0, hf, 2 * hf, 3 * hf, 3 * hf + hv)
    b0 = 3 * hf + 2 * hv
    b_cols = (b0, b0 + gk, b0 + gk, b0 + 2 * gk, b0 + 2 * gk + gv)
    oa = _recurrence(z, gf, gb, hg_norm_g, batch, seq, hg_heads, hv // hg_heads, a_cols,
                     name="hgrn_recurrence")
    ob = _recurrence(z, gg, gg, gla_norm_g, batch, seq, gla_heads, gv // gla_heads, b_cols,
                     gb_off=gla_heads, name="gla_recurrence")
    h, hn_packed, w4, idx4, sel = _mix_out(oa, ob, z, b0 + 2 * gk + 2 * gv, h2, w_proj_a, w_proj_b, w_out,
                                           norm2_g, router_w, router_b)

    n_pairs = n_tok * TOP_K
    row_block = _divisor_tile(n_pairs, FFN_ROW_BLOCK)
    n_blocks = n_pairs // row_block + n_exp
    dest4, counts = _route_rank(sel, idx4, row_block)
    dest_flat = dest4[:, :TOP_K].reshape(-1)
    cnt = counts[0, :n_exp].astype(I32)
    blocks_e = (cnt + row_block - 1) // row_block
    end_blk = jnp.cumsum(blocks_e)
    start_blk = end_blk - blocks_e
    bidx = jnp.arange(n_blocks, dtype=I32)[:, None]
    block_e = jnp.minimum(jnp.sum((end_blk[None, :] <= bidx).astype(I32), axis=1), n_exp - 1)
    owned = (bidx >= start_blk[None, :]) & (bidx < end_blk[None, :])
    rows_left = jnp.clip(cnt[None, :] - (bidx - start_blk[None, :]) * row_block, 0, row_block)
    block_valid = jnp.sum(jnp.where(owned, rows_left, 0), axis=1).astype(I32)

    xs = _dispatch(block_valid, dest_flat, hn_packed, row_block)
    rows = _expert_ffn(block_e, block_valid, xs, _regroup_gate_up(w_gate_up), w_down.astype(F32),
                       _regroup_bias(b_gate_up), b_down[:, None, :].astype(F32), row_block)
    return _combine(dest_flat, h, w4, out_norm_g, rows)


def kernel(x, norm1_g, w_in, hg_lb_logits, hg_norm_g, gla_gate_w2, gla_gate_b, gla_norm_g, w_proj_a, w_proj_b,
           w_out, norm2_g, router_w, router_b, w_gate_up, b_gate_up, w_down, b_down, final_norm_g):
    batch, seq, d = x.shape
    depth = w_in.shape[0]
    assert depth == 1, "the final RMSNorm is fused into the last layer's combine kernel"
    lb_all = jnp.cumsum(jax.nn.softmax(hg_lb_logits.astype(F32), axis=1), axis=1)
    h2 = x.reshape(batch * seq, d)
    out = _layer(h2, batch, seq, lb_all[:, 0], norm1_g[0], w_in[0], hg_norm_g[0], gla_gate_w2[0], gla_gate_b[0],
                 gla_norm_g[0], w_proj_a[0], w_proj_b[0], w_out[0], norm2_g[0], router_w[0], router_b[0],
                 w_gate_up[0], b_gate_up[0], w_down[0], b_down[0], final_norm_g)
    return out.reshape(batch, seq, d)
```

```python
import functools

import jax
import jax.numpy as jnp
from jax import lax
from jax.experimental import pallas as pl
from jax.experimental.pallas import tpu as pltpu

F32 = jnp.float32
BF16 = jnp.bfloat16
U32 = jnp.uint32
I32 = jnp.int32

LANES = 128
SUBLANES = 8
VMEM_LIMIT_BYTES = 56 * 1024 * 1024

HEAD_DK = 128
GLA_DV = 256
HG_DV = 128
GATE_NORM = 16.0
TOP_K = 4
SWIGLU_LIMIT = 7.0
SWIGLU_ALPHA = 1.702
RMS_EPS = 1e-5
CHUNK = 64
NEG_BIG = -1e30
LOG2_E = 1.4426950408889634


def _divisor_tile(n, pref):
    t = min(n, pref)
    while n % t:
        t //= 2
    return t


def _params(sem, vmem=VMEM_LIMIT_BYTES):
    return pltpu.CompilerParams(dimension_semantics=sem, vmem_limit_bytes=vmem)


def _pack_bf16_pairs(a):
    n = a.shape[-1] // 2
    hi = lax.bitcast_convert_type(a[:, :n].astype(BF16).astype(F32), U32)
    lo = lax.bitcast_convert_type(a[:, n:].astype(BF16).astype(F32), U32)
    return hi | (lo >> 16)


def _unpack_bf16_pairs(p):
    hi = lax.bitcast_convert_type(p & jnp.uint32(0xFFFF0000), F32)
    lo = lax.bitcast_convert_type(p << 16, F32)
    return hi, lo


def _rms(x, g):
    ms = jnp.mean(x * x, axis=-1, keepdims=True)
    return x * lax.rsqrt(ms + RMS_EPS) * g


def _log_sigmoid(x):
    return jnp.minimum(x, 0.0) - jnp.log1p(jnp.exp(-jnp.abs(x)))


def _in_proj_kernel(bounds, x_ref, g1_ref, w_ref, lb_ref, w2_ref, b2_ref,
                    z_ref, gf_ref, gb_ref, gg_ref, xn_ref):
    j = pl.program_id(1)

    @pl.when(j == 0)
    def _():
        xn_ref[...] = _rms(x_ref[...], g1_ref[...]).astype(BF16)

    scale = HEAD_DK ** -0.5

    def project():
        return jnp.dot(xn_ref[...], w_ref[...], preferred_element_type=F32)

    def silu(z):
        return z * jax.nn.sigmoid(z)

    def store(fn):
        def epilogue():
            z_ref[...] = fn(project()).astype(BF16)
        return epilogue

    def forget(g_ref):
        def epilogue():
            lb = lb_ref[0:1, :]
            f = lb + (1.0 - lb) * jax.nn.sigmoid(project())
            z_ref[...] = (1.0 - f).astype(BF16)
            g_ref[...] = jnp.log2(f)
        return epilogue

    def low_rank_gate():
        lr = project()[:, :LANES].astype(BF16)
        logits = jnp.dot(lr, w2_ref[...], preferred_element_type=F32) + b2_ref[...]
        gg_ref[...] = _log_sigmoid(logits) * (LOG2_E / GATE_NORM)

    epilogues = [
        store(lambda z: silu(z) * scale),
        forget(gf_ref),
        forget(gb_ref),
        store(lambda z: z),
        store(silu),
        store(lambda z: z * scale),
        store(lambda z: z),
        store(lambda z: z),
        store(silu),
        store(jax.nn.sigmoid),
        store(jax.nn.sigmoid),
        low_rank_gate,
    ]
    for k, epilogue in enumerate(epilogues):
        pl.when((j >= bounds[k]) & (j < bounds[k + 1]))(epilogue)


def _in_proj(x2, norm1_g, w_in, lb, gate_w2, gate_b, dims):
    n_tok, d = x2.shape
    hf, hv, gk, gv, rank = dims
    widths = [hf, hf, hf, hv, hv, gk, gk, gv, gv]
    off_lr = sum(widths)
    n_main = off_lr + 2 * d
    tn = _divisor_tile(gk, 256)
    tm = _divisor_tile(n_tok, 1024)
    widths_all = widths + [d, d, tn]
    w_lr = jnp.pad(w_in[:, off_lr:off_lr + 2 * rank], ((0, 0), (0, tn - 2 * rank)))
    w_cat = jnp.concatenate([w_in[:, :off_lr], w_in[:, off_lr + 2 * rank:], w_lr], axis=1).astype(BF16)
    n_col = w_cat.shape[1]
    w_tiles = w_cat.reshape(d, n_col // tn, tn).transpose(1, 0, 2)
    lb_tab = jnp.zeros((SUBLANES, n_col), F32)
    lb_tab = lb_tab.at[0, hf:2 * hf].set(lb[0]).at[0, 2 * hf:3 * hf].set(lb[1])
    w2 = jnp.zeros((LANES, 2 * gk), F32)
    w2 = w2.at[:rank, :gk].set(gate_w2[0]).at[rank:2 * rank, gk:].set(gate_w2[1]).astype(BF16)
    b2 = jnp.concatenate([gate_b[0], gate_b[1]])[None, :].astype(F32)

    bounds = [0]
    for w in widths_all:
        bounds.append(bounds[-1] + w // tn)
    n_j = bounds[-1]

    def seg_spec(k):
        s, n = bounds[k], bounds[k + 1] - bounds[k]
        return pl.BlockSpec((tm, tn), lambda i, j, s=s, n=n: (i, jnp.clip(j - s, 0, n - 1)))

    return pl.pallas_call(
        functools.partial(_in_proj_kernel, tuple(bounds)),
        grid=(n_tok // tm, n_j),
        in_specs=[
            pl.BlockSpec((tm, d), lambda i, j: (i, 0)),
            pl.BlockSpec((1, d), lambda i, j: (0, 0)),
            pl.BlockSpec((None, d, tn), lambda i, j: (j, 0, 0)),
            pl.BlockSpec((SUBLANES, tn), lambda i, j: (0, j)),
            pl.BlockSpec((LANES, 2 * gk), lambda i, j: (0, 0)),
            pl.BlockSpec((1, 2 * gk), lambda i, j: (0, 0)),
        ],
        out_specs=[pl.BlockSpec((tm, tn), lambda i, j: (i, jnp.minimum(j, n_main // tn - 1))),
                   seg_spec(1), seg_spec(2),
                   pl.BlockSpec((tm, 2 * gk), lambda i, j: (i, 0))],
        out_shape=[jax.ShapeDtypeStruct((n_tok, n_main), BF16),
                   jax.ShapeDtypeStruct((n_tok, hf), F32),
                   jax.ShapeDtypeStruct((n_tok, hf), F32),
                   jax.ShapeDtypeStruct((n_tok, 2 * gk), F32)],
        scratch_shapes=[pltpu.VMEM((tm, d), BF16)],
        compiler_params=_params(("arbitrary", "arbitrary")),
        name="in_proj",
    )(x2, norm1_g.reshape(1, d), w_tiles, lb_tab, w2, b2)


_NT_DIMS = (((1,), (1,)), ((), ()))
_TN_DIMS = (((0,), (0,)), ((), ()))


def _chunk_prepare(q, k, g, rev):
    nt = CHUNK // SUBLANES
    row = lax.broadcasted_iota(I32, (SUBLANES, LANES), 0)
    lrow = (SUBLANES - 1 - row) if rev else row

    def phys(jl):
        return nt - 1 - jl if rev else jl

    def prow(x, rl):
        r = SUBLANES - 1 - rl if rev else rl
        return x[r:r + 1, :]

    def tile(x, jl):
        p = phys(jl)
        return x[SUBLANES * p:SUBLANES * (p + 1), :]

    def scan(x):
        for s in (1, 2, 4):
            if rev:
                x = x + jnp.where(row < SUBLANES - s, pltpu.roll(x, SUBLANES - s, 0), 0.0)
            else:
                x = x + jnp.where(row >= s, pltpu.roll(x, s, 0), 0.0)
        return x

    gt = [tile(g, jl) for jl in range(nt)]
    qt = [tile(q, jl) for jl in range(nt)]
    kt = [tile(k, jl) for jl in range(nt)]
    ct = [scan(x) for x in gt]
    tot = [prow(c, SUBLANES - 1) for c in ct]
    suf = [t - c for t, c in zip(tot, ct)]

    def assemble(tiles):
        out = [None] * nt
        for jl, x in enumerate(tiles):
            out[phys(jl)] = jnp.zeros((SUBLANES, LANES), F32) if x is None else x
        return jnp.concatenate(out, axis=0).astype(BF16)

    def plus(a, b):
        return a if b is None else a + b

    levels = []

    for nb in (8, 4, 2):
        half = nb // 2
        qs, ks = [None] * nt, [None] * nt
        for base in range(0, nt, nb):
            acc = None
            for jl in range(base + half, base + nb):
                qs[jl] = qt[jl] * jnp.exp2(plus(ct[jl], acc))
                acc = plus(tot[jl], acc)
            acc = None
            for jl in range(base + half - 1, base - 1, -1):
                ks[jl] = kt[jl] * jnp.exp2(plus(suf[jl], acc))
                acc = plus(tot[jl], acc)
        levels.append((SUBLANES * nb, assemble(qs), assemble(ks)))

    for size in (8, 4):
        half = size // 2
        is_q = (lrow & (size - 1)) >= half
        qs, ks = [], []
        for jl in range(nt):
            if size == 8:
                ref = prow(ct[jl], half - 1)
            else:
                ref = jnp.where(lrow >= size, prow(ct[jl], size + half - 1), prow(ct[jl], half - 1))
            d = ct[jl] - ref
            e = jnp.exp2(jnp.where(is_q, d, -d))
            qs.append(jnp.where(is_q, qt[jl] * e, 0.0))
            ks.append(jnp.where(is_q, 0.0, kt[jl] * e))
        levels.append((size, assemble(qs), assemble(ks)))

    odd = (lrow & 1) == 1
    qs = [jnp.where(odd, qt[jl] * jnp.exp2(gt[jl]), 0.0) for jl in range(nt)]
    ks = [jnp.where(odd, 0.0, kt[jl]) for jl in range(nt)]
    levels.append((2, assemble(qs), assemble(ks)))
    levels.append((1, q.astype(BF16), k.astype(BF16)))

    ti = lax.broadcasted_iota(I32, (CHUNK, CHUNK), 0)
    si = lax.broadcasted_iota(I32, (CHUNK, CHUNK), 1)
    blk = ti ^ si
    scores = None
    for size, qm, km in levels:
        s_m = lax.dot_general(qm, km, _NT_DIMS, preferred_element_type=F32)
        scores = s_m if scores is None else jnp.where(blk < size, s_m, scores)

    q_in, k_out = [None] * nt, [None] * nt
    acc = None
    for jl in range(nt):
        q_in[jl] = qt[jl] * jnp.exp2(plus(ct[jl], acc))
        acc = plus(tot[jl], acc)
    total = acc
    acc = None
    for jl in range(nt - 1, -1, -1):
        k_out[jl] = kt[jl] * jnp.exp2(plus(suf[jl], acc))
        acc = plus(tot[jl], acc)
    return scores.astype(BF16), assemble(q_in), assemble(k_out), jnp.exp2(total)


def _chunk_apply(scores, q_in, k_out, decay, v, st_ref):
    st = st_ref[...]
    out = (jnp.dot(scores, v, preferred_element_type=F32)
           + lax.dot_general(q_in, st.astype(BF16), _NT_DIMS, preferred_element_type=F32))
    st_ref[...] = st * decay + lax.dot_general(v, k_out, _TN_DIMS, preferred_element_type=F32)
    return out


REGROUP_LOOKAHEAD = 3


def _regroup_side_task(side, perm_ref, w_src, w_dst, inbuf, outbuf, in_sem, out_sem, zero_ref=None, zero_sem=None):
    e0, rows, n_items, fill = side
    d = w_src.shape[1]
    n_g = w_src.shape[2] // GROUP
    n_slab, slab_w = w_dst.shape[1], w_dst.shape[3]
    per_slab = slab_w // GROUP
    n_in = REGROUP_LOOKAHEAD + 1
    last = n_items - 1

    def coords(it):
        row = it * rows
        return e0 + row // d, pl.multiple_of(row % d, rows)

    def in_copies(it, slot):
        e, r = coords(it)
        return [pltpu.make_async_copy(w_src.at[e, pl.ds(r, rows), pl.ds(GROUP * g, GROUP)],
                                      inbuf.at[slot, pl.ds(g * rows, rows), :], in_sem.at[slot])
                for g in range(n_g)]

    def out_copies(it, slot):
        e, r = coords(it)
        return [pltpu.make_async_copy(outbuf.at[slot, j], w_dst.at[e, j, pl.ds(r, rows), :], out_sem.at[slot])
                for j in range(n_slab)]

    def prologue():
        outbuf[...] = jnp.zeros_like(outbuf)
        for it in range(min(REGROUP_LOOKAHEAD, n_items)):
            for cp in in_copies(it, it % n_in):
                cp.start()
        for slot in range(2):
            for cp in out_copies(min(slot, last), slot):
                cp.start()
        if fill:
            zero_ref[...] = jnp.zeros_like(zero_ref)
            for cp in fill_copies():
                cp.start()

    def fill_copies():
        return [pltpu.make_async_copy(zero_ref, w_dst.at[e, j], zero_sem) for e in fill for j in range(n_slab)]

    def item_wait(it):
        for cp in in_copies(it, it % n_in):
            cp.wait()
        for cp in out_copies(it, it % 2):
            cp.wait()

    def item_compute(it):
        res = jnp.dot(inbuf[it % n_in].astype(BF16), perm_ref[...], preferred_element_type=F32).astype(BF16)
        for g in range(n_g):
            outbuf[it % 2, g // per_slab, :, pl.ds(GROUP * (g % per_slab), GROUP)] = res[g * rows:(g + 1) * rows, :]

    def item_start(it):
        ahead = jnp.minimum(it + REGROUP_LOOKAHEAD, last)
        for cp in in_copies(ahead, (it + REGROUP_LOOKAHEAD) % n_in):
            cp.start()
        for cp in out_copies(it, it % 2):
            cp.start()

    def epilogue():
        for slot in range(2):
            for cp in out_copies(last, slot):
                cp.wait()
        for k in range(1, REGROUP_LOOKAHEAD + 1):
            for cp in in_copies(last, (last + k) % n_in):
                cp.wait()
        if fill:
            for cp in fill_copies():
                cp.wait()

    return prologue, (item_wait, item_compute, item_start), epilogue


def _recurrence_kernel(n_chunks, dv, side, *refs):
    q_ref, kf_ref, kb_ref, v_ref, gf_ref, gb_ref, og_ref, ng_ref = refs[:8]
    refs = refs[8:]
    if side is not None:
        perm_ref, w_src = refs[:2]
        n_tail = 13 + (2 if side[3] else 0)
        refs = refs[len(refs) - n_tail:]
        o_ref, w_dst, of_ref, ob_ref, st_ref, sc_ref, qin_ref, kout_ref, dec_ref = refs[:9]
        side_prologue, side_item, side_epilogue = _regroup_side_task(side, perm_ref, w_src, w_dst, *refs[9:])
        step = pl.program_id(0) * pl.num_programs(1) + pl.program_id(1)
        n_steps = pl.num_programs(0) * pl.num_programs(1)
        pl.when(step == 0)(side_prologue)
    else:
        o_ref, of_ref, ob_ref, st_ref, sc_ref, qin_ref, kout_ref, dec_ref = refs
    n_heads = q_ref.shape[1] // HEAD_DK
    lanes = [(rev, k_ref, g_ref, out_ref, hd)
             for rev, k_ref, g_ref, out_ref in ((False, kf_ref, gf_ref, of_ref), (True, kb_ref, gb_ref, ob_ref))
             for hd in range(n_heads)]
    st_ref[...] = jnp.zeros_like(st_ref)

    def rows_of(c, rev):
        cc = (n_chunks - 1 - c) if rev else c
        return pl.ds(pl.multiple_of(cc * CHUNK, CHUNK), CHUNK)

    def kcols(hd):
        return slice(HEAD_DK * hd, HEAD_DK * (hd + 1))

    def vcols(hd):
        return slice(dv * hd, dv * (hd + 1))

    def prepare(c):
        res = []
        for rev, k_ref, g_ref, _, hd in lanes:
            rows = rows_of(c, rev)
            res.append(_chunk_prepare(q_ref[rows, kcols(hd)].astype(F32), k_ref[rows, kcols(hd)].astype(F32),
                                      g_ref[rows, kcols(hd)], rev))
        return res

    def stash(res):
        for d, (scores, q_in, k_out, decay) in enumerate(res):
            sc_ref[d] = scores
            qin_ref[d] = q_in
            kout_ref[d] = k_out
            dec_ref[d] = jnp.broadcast_to(decay, (SUBLANES, LANES))

    def unstash():
        return [(sc_ref[d], qin_ref[d], kout_ref[d], dec_ref[d][0:1, :]) for d in range(len(lanes))]

    def apply(c, staged):
        for d, ((rev, _, _, out_ref, hd), (scores, q_in, k_out, decay)) in enumerate(zip(lanes, staged)):
            rows = rows_of(c, rev)
            out_ref[rows, vcols(hd)] = _chunk_apply(scores, q_in, k_out, decay, v_ref[rows, vcols(hd)],
                                                    st_ref.at[d])

    stash(prepare(0))

    def body(c, carry):
        if side is not None:
            side_item[0](step * n_chunks + c)
        staged = unstash()
        nxt = prepare(c + 1)
        apply(c, staged)
        stash(nxt)
        if side is not None:
            side_item[1](step * n_chunks + c)
            side_item[2](step * n_chunks + c)
        return carry

    lax.fori_loop(0, n_chunks - 1, body, 0)
    if side is not None:
        side_item[0](step * n_chunks + n_chunks - 1)
        side_item[1](step * n_chunks + n_chunks - 1)
        side_item[2](step * n_chunks + n_chunks - 1)
    apply(n_chunks - 1, unstash())
    if side is not None:
        pl.when(step == n_steps - 1)(side_epilogue)
    for hd in range(n_heads):
        o = of_ref[:, vcols(hd)] + ob_ref[:, vcols(hd)]
        o_ref[:, vcols(hd)] = (_rms(o, ng_ref[...]) * og_ref[:, vcols(hd)].astype(F32)).astype(BF16)


RECURRENCE_HEADS_PER_STEP = 2


def _recurrence_steps(batch, seq, heads):
    hp = RECURRENCE_HEADS_PER_STEP if heads % RECURRENCE_HEADS_PER_STEP == 0 else 1
    return hp, batch * (heads // hp) * (seq // CHUNK)


def _recurrence(z, gf, gb, norm_g, batch, seq, heads, dv, cols, gb_off=0, name="", regroup=None):
    n_tok = batch * seq
    n_chunks = seq // CHUNK
    q_col, kf_col, kb_col, v_col, og_col = cols
    hp, n_items = _recurrence_steps(batch, seq, heads)

    def spec(width, col=0):
        width = width * hp
        assert col % width == 0
        return pl.BlockSpec((seq, width), lambda b, h, off=col // width: (b, h + off))

    in_specs = [spec(HEAD_DK, q_col), spec(HEAD_DK, kf_col), spec(HEAD_DK, kb_col), spec(dv, v_col),
                spec(HEAD_DK), spec(HEAD_DK, gb_off * HEAD_DK), spec(dv, og_col),
                pl.BlockSpec((1, dv), lambda b, h: (0, 0))]
    operands = [z, z, z, z, gf, gb, z, norm_g.reshape(1, dv).astype(F32)]
    out_specs = [spec(dv)]
    out_shape = [jax.ShapeDtypeStruct((n_tok, heads * dv), BF16)]
    scratch = [pltpu.VMEM((seq, hp * dv), F32), pltpu.VMEM((seq, hp * dv), F32),
               pltpu.VMEM((2 * hp, dv, HEAD_DK), F32),
               pltpu.VMEM((2 * hp, CHUNK, CHUNK), BF16), pltpu.VMEM((2 * hp, CHUNK, HEAD_DK), BF16),
               pltpu.VMEM((2 * hp, CHUNK, HEAD_DK), BF16), pltpu.VMEM((2 * hp, SUBLANES, LANES), F32)]
    side, aliases = None, {}
    if regroup is not None:
        w_gate_up, e0, n_e, partial = regroup
        n_exp, d, two_f = w_gate_up.shape
        slab_w = 2 * _divisor_tile(two_f // 2, FFN_FF_TILE)
        rows = n_e * d // n_items
        assert rows * n_items == n_e * d and d % rows == 0 and rows % (2 * SUBLANES) == 0
        fill = () if partial is not None else tuple(e for e in range(n_exp) if not e0 <= e < e0 + n_e)
        side = (e0, rows, n_items, fill)
        in_specs += [pl.BlockSpec((GROUP, GROUP), lambda b, h: (0, 0)), pl.BlockSpec(memory_space=pl.ANY)]
        operands += [_regroup_permutation(), w_gate_up]
        if partial is not None:
            aliases = {len(operands): 1}
            in_specs.append(pl.BlockSpec(memory_space=pl.ANY))
            operands.append(partial)
        out_specs.append(pl.BlockSpec(memory_space=pl.ANY))
        out_shape.append(jax.ShapeDtypeStruct((n_exp, two_f // slab_w, d, slab_w), BF16))
        scratch += [pltpu.VMEM((REGROUP_LOOKAHEAD + 1, two_f // GROUP * rows, GROUP), F32),
                    pltpu.VMEM((2, two_f // slab_w, rows, slab_w), BF16),
                    pltpu.SemaphoreType.DMA((REGROUP_LOOKAHEAD + 1,)), pltpu.SemaphoreType.DMA((2,))]
        if fill:
            scratch += [pltpu.VMEM((d, slab_w), BF16), pltpu.SemaphoreType.DMA(())]

    res = pl.pallas_call(
        functools.partial(_recurrence_kernel, n_chunks, dv, side),
        grid=(batch, heads // hp),
        in_specs=in_specs,
        out_specs=out_specs,
        out_shape=out_shape,
        scratch_shapes=scratch,
        input_output_aliases=aliases,
        compiler_params=_params(("arbitrary", "arbitrary")),
        name=name,
    )(*operands)
    return res if regroup is not None else res[0]


MIX_SUB_ROWS = 256


def _mix_out_kernel(oa_ref, ob_ref, sga_ref, sgb_ref, x_ref, wpa_ref, wpb_ref, wo_ref, g2_ref, rw2_ref, rw1_ref,
                    rb_ref, h_ref, hn_ref, w4_ref, idx_ref, sel_ref):
    tm = x_ref.shape[0]
    sub = min(tm, MIX_SUB_ROWS)
    for r in range(tm // sub):
        _mix_out_rows(pl.ds(r * sub, sub), oa_ref, ob_ref, sga_ref, sgb_ref, x_ref, wpa_ref, wpb_ref, wo_ref,
                      g2_ref, rw2_ref, rw1_ref, rb_ref, h_ref, hn_ref, w4_ref, idx_ref, sel_ref)


def _mix_out_rows(rows, oa_ref, ob_ref, sga_ref, sgb_ref, x_ref, wpa_ref, wpb_ref, wo_ref, g2_ref, rw2_ref,
                  rw1_ref, rb_ref, h_ref, hn_ref, w4_ref, idx_ref, sel_ref):
    ya = jnp.dot(oa_ref[rows, :], wpa_ref[...], preferred_element_type=F32)
    yb = jnp.dot(ob_ref[rows, :], wpb_ref[...], preferred_element_type=F32)
    merged = sga_ref[rows, :].astype(F32) * ya + sgb_ref[rows, :].astype(F32) * yb
    h = x_ref[rows, :] + jnp.dot(merged.astype(BF16), wo_ref[...], preferred_element_type=F32)
    h_ref[rows, :] = h
    hn = _rms(h, g2_ref[...])
    hn_ref[rows, :] = _pack_bf16_pairs(hn)

    hn_hi = hn.astype(BF16)
    hn_lo = (hn - hn_hi.astype(F32)).astype(BF16)
    two = jnp.dot(hn_hi, rw2_ref[...], preferred_element_type=F32)
    logits = (two[:, :LANES] + two[:, LANES:] + jnp.dot(hn_lo, rw1_ref[...], preferred_element_type=F32)
              + rb_ref[...])
    lane = lax.broadcasted_iota(I32, logits.shape, 1)
    lane_f = lane.astype(F32)
    work = logits
    vals, idxs = [], []
    for _ in range(TOP_K):
        m = jnp.max(work, axis=-1, keepdims=True)
        idx = jnp.min(jnp.where(work == m, lane_f, float(LANES)), axis=-1, keepdims=True)
        vals.append(m)
        idxs.append(idx)
        work = jnp.where(lane_f == idx, -jnp.inf, work)
    es = [jnp.exp(v - vals[0]) for v in vals]
    denom = es[0]
    for e in es[1:]:
        denom = denom + e
    w4 = jnp.zeros(logits.shape, F32)
    i4 = jnp.zeros(logits.shape, F32)
    sel = jnp.zeros(logits.shape, F32)
    for k in range(TOP_K):
        w4 = jnp.where(lane == k, es[k] / denom, w4)
        i4 = jnp.where(lane == k, idxs[k], i4)
        sel = jnp.where(lane_f == idxs[k], 1.0, sel)
    w4_ref[rows, :] = w4
    idx_ref[rows, :] = i4.astype(I32)
    sel_ref[rows, :] = sel.astype(BF16)


def _mix_out(oa, ob, z, gate_col, x2, w_proj_a, w_proj_b, w_out, norm2_g, router_w, router_b):
    n_tok, d = x2.shape
    hv, gv = oa.shape[1], ob.shape[1]
    n_exp = router_w.shape[1]
    tm = _divisor_tile(n_tok, 2 * MIX_SUB_ROWS)
    rw = jnp.pad(router_w.astype(F32), ((0, 0), (0, LANES - n_exp)))
    rw_hi = rw.astype(BF16)
    rw_lo = (rw - rw_hi.astype(F32)).astype(BF16)
    rb = jnp.concatenate([router_b.astype(F32), jnp.full((LANES - n_exp,), NEG_BIG, F32)])[None, :]
    assert gate_col % d == 0
    gate_blk = gate_col // d

    def rows(width, off=0):
        return pl.BlockSpec((tm, width), lambda i, off=off: (i, off))

    def whole(r, c):
        return pl.BlockSpec((r, c), lambda i: (0, 0), pipeline_mode=pl.Buffered(1))

    return pl.pallas_call(
        _mix_out_kernel,
        grid=(n_tok // tm,),
        in_specs=[rows(hv), rows(gv), rows(d, gate_blk), rows(d, gate_blk + 1), rows(d),
                  whole(hv, d), whole(gv, d), whole(d, d), whole(1, d), whole(d, 2 * LANES), whole(d, LANES),
                  whole(1, LANES)],
        out_specs=[rows(d), rows(d // 2), rows(LANES), rows(LANES), rows(LANES)],
        out_shape=[jax.ShapeDtypeStruct((n_tok, d), F32),
                   jax.ShapeDtypeStruct((n_tok, d // 2), U32),
                   jax.ShapeDtypeStruct((n_tok, LANES), F32),
                   jax.ShapeDtypeStruct((n_tok, LANES), I32),
                   jax.ShapeDtypeStruct((n_tok, LANES), BF16)],
        compiler_params=_params(("arbitrary",)),
        name="mix_out",
    )(oa, ob, z, z, x2, w_proj_a.astype(BF16), w_proj_b.astype(BF16), w_out.astype(BF16),
      norm2_g.reshape(1, d).astype(F32), jnp.concatenate([rw_hi, rw_lo], axis=1), rw_hi, rb)


def _route_rank_kernel(row_block, sel_ref, idx_ref, tri_ref, dest_ref, cnt_ref, rank_ref, carry_ref, start_ref):
    p = pl.program_id(0)
    i = pl.program_id(1)
    tb = sel_ref.shape[0]
    rows = pl.ds(pl.multiple_of(i * tb, tb), tb)

    @pl.when((p == 0) & (i == 0))
    def _():
        carry_ref[...] = jnp.zeros_like(carry_ref)

    @pl.when(p == 0)
    def _():
        sel = sel_ref[...]
        before = jnp.dot(tri_ref[...], sel, preferred_element_type=F32)
        rank_ref[rows, :] = before + carry_ref[0:1, :]
        carry_ref[...] = carry_ref[...] + jnp.sum(sel.astype(F32), axis=0, keepdims=True)

    @pl.when((p == 1) & (i == 0))
    def _():
        counts = carry_ref[...]
        padded = jnp.ceil(counts * (1.0 / row_block)) * row_block
        lane = lax.broadcasted_iota(I32, padded.shape, 1)
        run = padded
        s = 1
        while s < LANES:
            run = run + jnp.where(lane >= s, pltpu.roll(run, s, 1), 0.0)
            s *= 2
        start_ref[...] = run - padded

    @pl.when(p == 1)
    def _():
        pos = rank_ref[rows, :] + start_ref[0:1, :]
        idx = idx_ref[...]
        lane = lax.broadcasted_iota(I32, pos.shape, 1)
        dest = jnp.zeros(pos.shape, F32)
        for k in range(TOP_K):
            hit = lane == idx[:, k:k + 1]
            val = jnp.sum(jnp.where(hit, pos, 0.0), axis=-1, keepdims=True)
            dest = jnp.where(lane == k, val, dest)
        dest_ref[...] = dest.astype(I32)
        cnt_ref[...] = carry_ref[...]


def _route_rank(sel, idx4, row_block):
    n_tok = sel.shape[0]
    tb = _divisor_tile(n_tok, 256)
    tri = (lax.broadcasted_iota(I32, (tb, tb), 0) > lax.broadcasted_iota(I32, (tb, tb), 1)).astype(BF16)
    return pl.pallas_call(
        functools.partial(_route_rank_kernel, row_block),
        grid=(2, n_tok // tb),
        in_specs=[pl.BlockSpec((tb, LANES), lambda p, i: (i, 0)),
                  pl.BlockSpec((tb, LANES), lambda p, i: (i, 0)),
                  pl.BlockSpec((tb, tb), lambda p, i: (0, 0))],
        out_specs=[pl.BlockSpec((tb, LANES), lambda p, i: (i * p, 0)),
                   pl.BlockSpec((SUBLANES, LANES), lambda p, i: (0, 0))],
        out_shape=[jax.ShapeDtypeStruct((n_tok, LANES), I32),
                   jax.ShapeDtypeStruct((SUBLANES, LANES), F32)],
        scratch_shapes=[pltpu.VMEM((n_tok, LANES), F32), pltpu.VMEM((SUBLANES, LANES), F32),
                        pltpu.VMEM((SUBLANES, LANES), F32)],
        compiler_params=_params(("arbitrary", "arbitrary")),
        name="route_rank",
    )(sel, idx4, tri)


def _dispatch_kernel(bv_ref, dest_ref, hn_ref, xs_ref, zero_ref, sem):
    tt = hn_ref.shape[0]
    row_block = zero_ref.shape[0]
    n_blocks = bv_ref.shape[0]

    @pl.when(pl.program_id(0) == 0)
    def _():
        zero_ref[...] = jnp.zeros_like(zero_ref)

        def zero_copy(blk):
            return pltpu.make_async_copy(zero_ref, xs_ref.at[pl.ds(blk * row_block, row_block), :], sem)

        for blk in range(n_blocks):
            pl.when(bv_ref[blk] < row_block)(zero_copy(blk).start)
        for blk in range(n_blocks):
            pl.when(bv_ref[blk] < row_block)(zero_copy(blk).wait)

    for t in range(tt):
        for k in range(TOP_K):
            pltpu.make_async_copy(hn_ref.at[pl.ds(t, 1), :], xs_ref.at[pl.ds(dest_ref[TOP_K * t + k], 1), :],
                                  sem).start(priority=(TOP_K * t + k) % 2)
    for k in range(TOP_K):
        pltpu.make_async_copy(hn_ref, xs_ref.at[pl.ds(0, tt), :], sem).wait()


def _dispatch(block_valid, dest_flat, hn_packed, row_block):
    n_tok, half = hn_packed.shape
    tt = _divisor_tile(n_tok, 512)
    n_rows = block_valid.shape[0] * row_block
    grid_spec = pltpu.PrefetchScalarGridSpec(
        num_scalar_prefetch=1,
        grid=(n_tok // tt,),
        in_specs=[pl.BlockSpec((tt * TOP_K,), lambda i, bv: (i,), memory_space=pltpu.SMEM),
                  pl.BlockSpec((tt, half), lambda i, bv: (i, 0))],
        out_specs=pl.BlockSpec(memory_space=pl.ANY),
        scratch_shapes=[pltpu.VMEM((row_block, half), U32), pltpu.SemaphoreType.DMA(())],
    )
    return pl.pallas_call(
        _dispatch_kernel,
        grid_spec=grid_spec,
        out_shape=jax.ShapeDtypeStruct((n_rows, half), U32),
        compiler_params=_params(("arbitrary",)),
        name="dispatch",
    )(block_valid, dest_flat, hn_packed)


GROUP = 2 * LANES


def _regroup_kernel(w_ref, p_ref, o_ref):
    p = p_ref[...]
    for g in range(w_ref.shape[1] // GROUP):
        cols = slice(GROUP * g, GROUP * (g + 1))
        o_ref[:, cols] = jnp.dot(w_ref[:, cols].astype(BF16), p, preferred_element_type=F32).astype(BF16)


FFN_FF_TILE = 512


def _regroup_split(w_shape, items_a, items_b):
    n_exp, d, _ = w_shape

    def slice_rows(n_e, items):
        rows = n_e * d // items
        ok = items >= 2 and rows * items == n_e * d and rows % (2 * SUBLANES) == 0 and d % rows == 0
        return rows if ok else None

    best = None
    for n_a in range(1, n_exp):
        ra, rb = slice_rows(n_a, items_a), slice_rows(n_exp - n_a, items_b)
        if ra is not None and rb is not None and (best is None or max(ra, rb) < best[0]):
            best = (max(ra, rb), n_a)
    return None if best is None else best[1]


def _regroup_permutation():
    src = lax.broadcasted_iota(I32, (GROUP, GROUP), 0)
    dst = lax.broadcasted_iota(I32, (GROUP, GROUP), 1)
    return (dst == (src // 2) + LANES * (src % 2)).astype(BF16)


def _regroup_gate_up(w_gate_up):
    n_exp, d, two_f = w_gate_up.shape
    tr = _divisor_tile(d, 2048)
    tc = 2 * _divisor_tile(two_f // 2, FFN_FF_TILE)
    perm = _regroup_permutation()
    return pl.pallas_call(
        _regroup_kernel,
        grid=(n_exp, d // tr, two_f // tc),
        in_specs=[pl.BlockSpec((None, tr, tc), lambda e, i, j: (e, i, j)),
                  pl.BlockSpec((GROUP, GROUP), lambda e, i, j: (0, 0))],
        out_specs=pl.BlockSpec((None, None, tr, tc), lambda e, i, j: (e, j, i, 0)),
        out_shape=jax.ShapeDtypeStruct((n_exp, two_f // tc, d, tc), BF16),
        compiler_params=_params(("arbitrary", "arbitrary", "arbitrary")),
        name="regroup_gate_up",
    )(w_gate_up, perm)


def _regroup_bias(b_gate_up):
    n_exp, two_f = b_gate_up.shape
    b = b_gate_up.reshape(n_exp, two_f // GROUP, LANES, 2)
    return jnp.swapaxes(b, 2, 3).reshape(n_exp, 1, two_f).astype(F32)


FFN_ROW_BLOCK = 1024
FFN_SUB_ROWS = 256


def _expert_ffn_kernel(n_f, be_ref, bv_ref, xs_ref, wgu_ref, wd_ref, bgu_ref, bd_ref,
                       out_ref, x_ref, acc_ref, wdb_ref):
    del be_ref
    b = pl.program_id(0)
    f = pl.program_id(1)
    n_rows = xs_ref.shape[0]
    sub = min(n_rows, FFN_SUB_ROWS)
    valid = bv_ref[b]
    n_sub = (valid + sub - 1) // sub
    full = n_sub == n_rows // sub
    partial = (n_sub > 0) & jnp.logical_not(full)
    first = f == 0
    last = f == n_f - 1

    def unpack():
        hi, lo = _unpack_bf16_pairs(xs_ref[...])
        return jnp.concatenate([hi.astype(BF16), lo.astype(BF16)], axis=1)

    def down(x, wd):
        gu = jnp.dot(x, wgu_ref[...], preferred_element_type=F32) + bgu_ref[...]
        acts = []
        for g in range(gu.shape[1] // GROUP):
            gate = jnp.minimum(gu[:, GROUP * g:GROUP * g + LANES], SWIGLU_LIMIT)
            up = jnp.clip(gu[:, GROUP * g + LANES:GROUP * (g + 1)], -SWIGLU_LIMIT, SWIGLU_LIMIT)
            acts.append(((up + 1.0) * (gate * jax.nn.sigmoid(SWIGLU_ALPHA * gate))).astype(BF16))
        return jnp.dot(jnp.concatenate(acts, axis=1), wd, preferred_element_type=F32)

    def full_step(is_first, is_last):
        def run():
            wd = wd_ref[...].astype(BF16)
            if is_first:
                x = unpack()
                x_ref[...] = x
                y = down(x, wd)
            else:
                y = acc_ref[...] + down(x_ref[...], wd)
            if is_last:
                out_ref[...] = _pack_bf16_pairs(y + bd_ref[...])
            else:
                acc_ref[...] = y
        return run

    if n_f == 1:
        pl.when(full)(full_step(True, True))
    else:
        pl.when(full & first)(full_step(True, False))
        pl.when(full & last)(full_step(False, True))
        if n_f > 2:
            pl.when(full & jnp.logical_not(first | last))(full_step(False, False))

    @pl.when(partial & first)
    def _():
        x_ref[...] = unpack()
        acc_ref[...] = jnp.zeros_like(acc_ref)

    @pl.when(partial)
    def _():
        wdb_ref[...] = wd_ref[...].astype(BF16)

        def body(s, carry):
            rows = pl.ds(pl.multiple_of(s * sub, sub), sub)
            acc_ref[rows, :] += down(x_ref[rows, :], wdb_ref[...])
            return carry
        lax.fori_loop(0, n_sub, body, 0)

    @pl.when(partial & last)
    def _():
        out_ref[...] = _pack_bf16_pairs(acc_ref[...] + bd_ref[...])

    @pl.when((valid == 0) & last)
    def _():
        out_ref[...] = jnp.zeros_like(out_ref)


def _expert_ffn(block_e, block_valid, xs, wgu, wd, bgu, bd, row_block):
    n_rows, half = xs.shape
    d = 2 * half
    d_ff = wd.shape[1]
    tf = wgu.shape[3] // 2
    n_blocks = n_rows // row_block
    grid_spec = pltpu.PrefetchScalarGridSpec(
        num_scalar_prefetch=2,
        grid=(n_blocks, d_ff // tf),
        in_specs=[
            pl.BlockSpec((row_block, half), lambda b, f, be, bv: (b, 0)),
            pl.BlockSpec((None, None, d, 2 * tf), lambda b, f, be, bv: (be[b], f, 0, 0)),
            pl.BlockSpec((None, tf, d), lambda b, f, be, bv: (be[b], f, 0)),
            pl.BlockSpec((None, 1, 2 * tf), lambda b, f, be, bv: (be[b], 0, f)),
            pl.BlockSpec((None, 1, d), lambda b, f, be, bv: (be[b], 0, 0)),
        ],
        out_specs=pl.BlockSpec((row_block, half), lambda b, f, be, bv: (b, 0)),
        scratch_shapes=[pltpu.VMEM((row_block, d), BF16), pltpu.VMEM((row_block, d), F32),
                        pltpu.VMEM((tf, d), BF16)],
    )
    return pl.pallas_call(
        functools.partial(_expert_ffn_kernel, d_ff // tf),
        grid_spec=grid_spec,
        out_shape=jax.ShapeDtypeStruct((n_rows, half), U32),
        compiler_params=_params(("arbitrary", "arbitrary")),
        name="expert_ffn",
    )(block_e, block_valid, xs, wgu, wd, bgu, bd)


def _combine_kernel(cur_ref, nxt_ref, h_ref, w4_ref, fg_ref, rows_hbm_ref, o_ref, buf_ref, sem):
    i = pl.program_id(0)
    n = pl.num_programs(0)
    tt = h_ref.shape[0]
    half = buf_ref.shape[3]

    def issue(dest_ref, s):
        for t in range(tt):
            for k in range(TOP_K):
                pltpu.make_async_copy(rows_hbm_ref.at[pl.ds(dest_ref[TOP_K * t + k], 1), :],
                                      buf_ref.at[s, k, pl.ds(t, 1), :],
                                      sem.at[s]).start(priority=(TOP_K * t + k) % 2)

    def drain(s):
        for k in range(TOP_K):
            pltpu.make_async_copy(rows_hbm_ref.at[pl.ds(0, tt), :], buf_ref.at[s, k], sem.at[s]).wait()

    @pl.when(i == 0)
    def _():
        issue(cur_ref, 0)

    def step(slot):
        drain(slot)
        issue(nxt_ref, 1 - slot)
        w4 = w4_ref[...]
        y_hi = jnp.zeros((tt, half), F32)
        y_lo = jnp.zeros((tt, half), F32)
        for k in range(TOP_K):
            hi, lo = _unpack_bf16_pairs(buf_ref[slot, k])
            wk = w4[:, k:k + 1]
            y_hi = y_hi + wk * hi
            y_lo = y_lo + wk * lo
        h = h_ref[...] + jnp.concatenate([y_hi, y_lo], axis=1)
        o_ref[...] = _rms(h, fg_ref[...])

        @pl.when(i == n - 1)
        def _():
            drain(1 - slot)

    for slot in range(2):
        pl.when(i % 2 == slot)(functools.partial(step, slot))


def _combine(dest_flat, h, w4, final_g, rows_packed):
    n_tok, d = h.shape
    half = d // 2
    tt = _divisor_tile(n_tok, 256)
    n_steps = n_tok // tt
    return pl.pallas_call(
        _combine_kernel,
        grid=(n_steps,),
        in_specs=[pl.BlockSpec((tt * TOP_K,), lambda i: (i,), memory_space=pltpu.SMEM),
                  pl.BlockSpec((tt * TOP_K,), lambda i: (jnp.minimum(i + 1, n_steps - 1),),
                               memory_space=pltpu.SMEM),
                  pl.BlockSpec((tt, d), lambda i: (i, 0)),
                  pl.BlockSpec((tt, LANES), lambda i: (i, 0)),
                  pl.BlockSpec((1, d), lambda i: (0, 0)),
                  pl.BlockSpec(memory_space=pl.ANY)],
        out_specs=pl.BlockSpec((tt, d), lambda i: (i, 0)),
        out_shape=jax.ShapeDtypeStruct((n_tok, d), F32),
        scratch_shapes=[pltpu.VMEM((2, TOP_K, tt, half), U32), pltpu.SemaphoreType.DMA((2,))],
        compiler_params=_params(("arbitrary",)),
        name="combine",
    )(dest_flat, dest_flat, h, w4, final_g.reshape(1, d).astype(F32), rows_packed)


def _layer(h2, batch, seq, lb, norm1_g, w_in, hg_norm_g, gate_w2, gate_b, gla_norm_g, w_proj_a, w_proj_b,
           w_out, norm2_g, router_w, router_b, w_gate_up, b_gate_up, w_down, b_down, out_norm_g):
    n_tok, d = h2.shape
    hf = lb.shape[1]
    hv = w_proj_a.shape[0]
    gk = gate_w2.shape[2]
    gv = w_proj_b.shape[0]
    rank = gate_w2.shape[1]
    hg_heads, gla_heads = hf // HEAD_DK, gk // HEAD_DK
    n_exp = router_w.shape[1]

    z, gf, gb, gg = _in_proj(h2, norm1_g, w_in, lb, gate_w2, gate_b, (hf, hv, gk, gv, rank))
    a_cols = (0, hf, 2 * hf, 3 * hf, 3 * hf + hv)
    b0 = 3 * hf + 2 * hv
    b_cols = (b0, b0 + gk, b0 + gk, b0 + 2 * gk, b0 + 2 * gk + gv)
    experts_a = _regroup_split(w_gate_up.shape, _recurrence_steps(batch, seq, hg_heads)[1],
                               _recurrence_steps(batch, seq, gla_heads)[1])
    if experts_a is None:
        oa = _recurrence(z, gf, gb, hg_norm_g, batch, seq, hg_heads, hv // hg_heads, a_cols,
                         name="hgrn_recurrence")
        ob = _recurrence(z, gg, gg, gla_norm_g, batch, seq, gla_heads, gv // gla_heads, b_cols,
                         gb_off=gla_heads, name="gla_recurrence")
        wgu = _regroup_gate_up(w_gate_up)
    else:
        oa, wgu = _recurrence(z, gf, gb, hg_norm_g, batch, seq, hg_heads, hv // hg_heads, a_cols,
                              name="hgrn_recurrence", regroup=(w_gate_up, 0, experts_a, None))
        ob, wgu = _recurrence(z, gg, gg, gla_norm_g, batch, seq, gla_heads, gv // gla_heads, b_cols,
                              gb_off=gla_heads, name="gla_recurrence",
                              regroup=(w_gate_up, experts_a, n_exp - experts_a, wgu))
    h, hn_packed, w4, idx4, sel = _mix_out(oa, ob, z, b0 + 2 * gk + 2 * gv, h2, w_proj_a, w_proj_b, w_out,
                                           norm2_g, router_w, router_b)

    n_pairs = n_tok * TOP_K
    row_block = _divisor_tile(n_pairs, FFN_ROW_BLOCK)
    n_blocks = n_pairs // row_block + n_exp
    dest4, counts = _route_rank(sel, idx4, row_block)
    dest_flat = dest4[:, :TOP_K].reshape(-1)
    cnt = counts[0, :n_exp].astype(I32)
    blocks_e = (cnt + row_block - 1) // row_block
    end_blk = jnp.cumsum(blocks_e)
    start_blk = end_blk - blocks_e
    bidx = jnp.arange(n_blocks, dtype=I32)[:, None]
    block_e = jnp.minimum(jnp.sum((end_blk[None, :] <= bidx).astype(I32), axis=1), n_exp - 1)
    owned = (bidx >= start_blk[None, :]) & (bidx < end_blk[None, :])
    rows_left = jnp.clip(cnt[None, :] - (bidx - start_blk[None, :]) * row_block, 0, row_block)
    block_valid = jnp.sum(jnp.where(owned, rows_left, 0), axis=1).astype(I32)

    xs = _dispatch(block_valid, dest_flat, hn_packed, row_block)
    rows = _expert_ffn(block_e, block_valid, xs, wgu, w_down.astype(F32),
                       _regroup_bias(b_gate_up), b_down[:, None, :].astype(F32), row_block)
    return _combine(dest_flat, h, w4, out_norm_g, rows)


def kernel(x, norm1_g, w_in, hg_lb_logits, hg_norm_g, gla_gate_w2, gla_gate_b, gla_norm_g, w_proj_a, w_proj_b,
           w_out, norm2_g, router_w, router_b, w_gate_up, b_gate_up, w_down, b_down, final_norm_g):
    batch, seq, d = x.shape
    depth = w_in.shape[0]
    assert depth == 1, "the final RMSNorm is fused into the last layer's combine kernel"
    lb_all = jnp.cumsum(jax.nn.softmax(hg_lb_logits.astype(F32), axis=1), axis=1)
    h2 = x.reshape(batch * seq, d)
    out = _layer(h2, batch, seq, lb_all[:, 0], norm1_g[0], w_in[0], hg_norm_g[0], gla_gate_w2[0], gla_gate_b[0],
                 gla_norm_g[0], w_proj_a[0], w_proj_b[0], w_out[0], norm2_g[0], router_w[0], router_b[0],
                 w_gate_up[0], b_gate_up[0], w_down[0], b_down[0], final_norm_g)
    return out.reshape(batch, seq, d)
```

```python
import functools

import jax
import jax.numpy as jnp
from jax import lax
from jax.experimental import pallas as pl
from jax.experimental.pallas import tpu as pltpu

F32 = jnp.float32
BF16 = jnp.bfloat16
U32 = jnp.uint32
I32 = jnp.int32

LANES = 128
SUBLANES = 8
VMEM_LIMIT_BYTES = 56 * 1024 * 1024

HEAD_DK = 128
GLA_DV = 256
HG_DV = 128
GATE_NORM = 16.0
TOP_K = 4
SWIGLU_LIMIT = 7.0
SWIGLU_ALPHA = 1.702
RMS_EPS = 1e-5
CHUNK = 64
NEG_BIG = -1e30
LOG2_E = 1.4426950408889634


def _divisor_tile(n, pref):
    t = min(n, pref)
    while n % t:
        t //= 2
    return t


def _params(sem, vmem=VMEM_LIMIT_BYTES):
    return pltpu.CompilerParams(dimension_semantics=sem, vmem_limit_bytes=vmem)


def _pack_bf16_pairs(a):
    n = a.shape[-1] // 2
    hi = lax.bitcast_convert_type(a[:, :n].astype(BF16).astype(F32), U32)
    lo = lax.bitcast_convert_type(a[:, n:].astype(BF16).astype(F32), U32)
    return hi | (lo >> 16)


def _unpack_bf16_pairs(p):
    hi = lax.bitcast_convert_type(p & jnp.uint32(0xFFFF0000), F32)
    lo = lax.bitcast_convert_type(p << 16, F32)
    return hi, lo


def _rms(x, g):
    ms = jnp.mean(x * x, axis=-1, keepdims=True)
    return x * lax.rsqrt(ms + RMS_EPS) * g


def _log_sigmoid(x):
    return jnp.minimum(x, 0.0) - jnp.log1p(jnp.exp(-jnp.abs(x)))


def _in_proj_kernel(bounds, x_ref, g1_ref, w_ref, lb_ref, w2_ref, b2_ref,
                    z_ref, gf_ref, gb_ref, gg_ref, xn_ref):
    j = pl.program_id(1)

    @pl.when(j == 0)
    def _():
        xn_ref[...] = _rms(x_ref[...], g1_ref[...]).astype(BF16)

    scale = HEAD_DK ** -0.5

    def project():
        return jnp.dot(xn_ref[...], w_ref[...], preferred_element_type=F32)

    def silu(z):
        return z * jax.nn.sigmoid(z)

    def store(fn):
        def epilogue():
            z_ref[...] = fn(project()).astype(BF16)
        return epilogue

    def forget(g_ref):
        def epilogue():
            lb = lb_ref[0:1, :]
            f = lb + (1.0 - lb) * jax.nn.sigmoid(project())
            z_ref[...] = (1.0 - f).astype(BF16)
            g_ref[...] = jnp.log2(f)
        return epilogue

    def low_rank_gate():
        lr = project()[:, :LANES].astype(BF16)
        logits = jnp.dot(lr, w2_ref[...], preferred_element_type=F32) + b2_ref[...]
        gg_ref[...] = _log_sigmoid(logits) * (LOG2_E / GATE_NORM)

    epilogues = [
        store(lambda z: silu(z) * scale),
        forget(gf_ref),
        forget(gb_ref),
        store(lambda z: z),
        store(silu),
        store(lambda z: z * scale),
        store(lambda z: z),
        store(lambda z: z),
        store(silu),
        store(jax.nn.sigmoid),
        store(jax.nn.sigmoid),
        low_rank_gate,
    ]
    for k, epilogue in enumerate(epilogues):
        pl.when((j >= bounds[k]) & (j < bounds[k + 1]))(epilogue)


def _in_proj(x2, norm1_g, w_in, lb, gate_w2, gate_b, dims):
    n_tok, d = x2.shape
    hf, hv, gk, gv, rank = dims
    widths = [hf, hf, hf, hv, hv, gk, gk, gv, gv]
    off_lr = sum(widths)
    n_main = off_lr + 2 * d
    tn = _divisor_tile(gk, 256)
    tm = _divisor_tile(n_tok, 1024)
    widths_all = widths + [d, d, tn]
    w_lr = jnp.pad(w_in[:, off_lr:off_lr + 2 * rank], ((0, 0), (0, tn - 2 * rank)))
    w_cat = jnp.concatenate([w_in[:, :off_lr], w_in[:, off_lr + 2 * rank:], w_lr], axis=1).astype(BF16)
    n_col = w_cat.shape[1]
    w_tiles = w_cat.reshape(d, n_col // tn, tn).transpose(1, 0, 2)
    lb_tab = jnp.zeros((SUBLANES, n_col), F32)
    lb_tab = lb_tab.at[0, hf:2 * hf].set(lb[0]).at[0, 2 * hf:3 * hf].set(lb[1])
    w2 = jnp.zeros((LANES, 2 * gk), F32)
    w2 = w2.at[:rank, :gk].set(gate_w2[0]).at[rank:2 * rank, gk:].set(gate_w2[1]).astype(BF16)
    b2 = jnp.concatenate([gate_b[0], gate_b[1]])[None, :].astype(F32)

    bounds = [0]
    for w in widths_all:
        bounds.append(bounds[-1] + w // tn)
    n_j = bounds[-1]

    def seg_spec(k):
        s, n = bounds[k], bounds[k + 1] - bounds[k]
        return pl.BlockSpec((tm, tn), lambda i, j, s=s, n=n: (i, jnp.clip(j - s, 0, n - 1)))

    return pl.pallas_call(
        functools.partial(_in_proj_kernel, tuple(bounds)),
        grid=(n_tok // tm, n_j),
        in_specs=[
            pl.BlockSpec((tm, d), lambda i, j: (i, 0)),
            pl.BlockSpec((1, d), lambda i, j: (0, 0)),
            pl.BlockSpec((None, d, tn), lambda i, j: (j, 0, 0)),
            pl.BlockSpec((SUBLANES, tn), lambda i, j: (0, j)),
            pl.BlockSpec((LANES, 2 * gk), lambda i, j: (0, 0)),
            pl.BlockSpec((1, 2 * gk), lambda i, j: (0, 0)),
        ],
        out_specs=[pl.BlockSpec((tm, tn), lambda i, j: (i, jnp.minimum(j, n_main // tn - 1))),
                   seg_spec(1), seg_spec(2),
                   pl.BlockSpec((tm, 2 * gk), lambda i, j: (i, 0))],
        out_shape=[jax.ShapeDtypeStruct((n_tok, n_main), BF16),
                   jax.ShapeDtypeStruct((n_tok, hf), F32),
                   jax.ShapeDtypeStruct((n_tok, hf), F32),
                   jax.ShapeDtypeStruct((n_tok, 2 * gk), F32)],
        scratch_shapes=[pltpu.VMEM((tm, d), BF16)],
        compiler_params=_params(("arbitrary", "arbitrary")),
        name="in_proj",
    )(x2, norm1_g.reshape(1, d), w_tiles, lb_tab, w2, b2)


_NT_DIMS = (((1,), (1,)), ((), ()))
_TN_DIMS = (((0,), (0,)), ((), ()))


def _chunk_prepare(q, k, g, rev):
    nt = CHUNK // SUBLANES
    row = lax.broadcasted_iota(I32, (SUBLANES, LANES), 0)
    lrow = (SUBLANES - 1 - row) if rev else row

    def phys(jl):
        return nt - 1 - jl if rev else jl

    def prow(x, rl):
        r = SUBLANES - 1 - rl if rev else rl
        return x[r:r + 1, :]

    def tile(x, jl):
        p = phys(jl)
        return x[SUBLANES * p:SUBLANES * (p + 1), :]

    def scan(x):
        for s in (1, 2, 4):
            if rev:
                x = x + jnp.where(row < SUBLANES - s, pltpu.roll(x, SUBLANES - s, 0), 0.0)
            else:
                x = x + jnp.where(row >= s, pltpu.roll(x, s, 0), 0.0)
        return x

    gt = [tile(g, jl) for jl in range(nt)]
    qt = [tile(q, jl) for jl in range(nt)]
    kt = [tile(k, jl) for jl in range(nt)]
    ct = [scan(x) for x in gt]
    tot = [prow(c, SUBLANES - 1) for c in ct]
    suf = [t - c for t, c in zip(tot, ct)]

    def assemble(tiles):
        out = [None] * nt
        for jl, x in enumerate(tiles):
            out[phys(jl)] = jnp.zeros((SUBLANES, LANES), F32) if x is None else x
        return jnp.concatenate(out, axis=0).astype(BF16)

    def plus(a, b):
        return a if b is None else a + b

    levels = []

    for nb in (8, 4, 2):
        half = nb // 2
        qs, ks = [None] * nt, [None] * nt
        for base in range(0, nt, nb):
            acc = None
            for jl in range(base + half, base + nb):
                qs[jl] = qt[jl] * jnp.exp2(plus(ct[jl], acc))
                acc = plus(tot[jl], acc)
            acc = None
            for jl in range(base + half - 1, base - 1, -1):
                ks[jl] = kt[jl] * jnp.exp2(plus(suf[jl], acc))
                acc = plus(tot[jl], acc)
        levels.append((SUBLANES * nb, assemble(qs), assemble(ks)))

    for size in (8, 4):
        half = size // 2
        is_q = (lrow & (size - 1)) >= half
        qs, ks = [], []
        for jl in range(nt):
            if size == 8:
                ref = prow(ct[jl], half - 1)
            else:
                ref = jnp.where(lrow >= size, prow(ct[jl], size + half - 1), prow(ct[jl], half - 1))
            d = ct[jl] - ref
            e = jnp.exp2(jnp.where(is_q, d, -d))
            qs.append(jnp.where(is_q, qt[jl] * e, 0.0))
            ks.append(jnp.where(is_q, 0.0, kt[jl] * e))
        levels.append((size, assemble(qs), assemble(ks)))

    odd = (lrow & 1) == 1
    qs = [jnp.where(odd, qt[jl] * jnp.exp2(gt[jl]), 0.0) for jl in range(nt)]
    ks = [jnp.where(odd, 0.0, kt[jl]) for jl in range(nt)]
    levels.append((2, assemble(qs), assemble(ks)))
    levels.append((1, q.astype(BF16), k.astype(BF16)))

    ti = lax.broadcasted_iota(I32, (CHUNK, CHUNK), 0)
    si = lax.broadcasted_iota(I32, (CHUNK, CHUNK), 1)
    blk = ti ^ si
    scores = None
    for size, qm, km in levels:
        s_m = lax.dot_general(qm, km, _NT_DIMS, preferred_element_type=F32)
        scores = s_m if scores is None else jnp.where(blk < size, s_m, scores)

    q_in, k_out = [None] * nt, [None] * nt
    acc = None
    for jl in range(nt):
        q_in[jl] = qt[jl] * jnp.exp2(plus(ct[jl], acc))
        acc = plus(tot[jl], acc)
    total = acc
    acc = None
    for jl in range(nt - 1, -1, -1):
        k_out[jl] = kt[jl] * jnp.exp2(plus(suf[jl], acc))
        acc = plus(tot[jl], acc)
    return scores.astype(BF16), assemble(q_in), assemble(k_out), jnp.exp2(total)


def _chunk_apply(scores, q_in, k_out, decay, v, st_ref):
    st = st_ref[...]
    out = (jnp.dot(scores, v, preferred_element_type=F32)
           + lax.dot_general(q_in, st.astype(BF16), _NT_DIMS, preferred_element_type=F32))
    st_ref[...] = st * decay + lax.dot_general(v, k_out, _TN_DIMS, preferred_element_type=F32)
    return out


REGROUP_LOOKAHEAD = 3


def _regroup_side_task(side, perm_ref, w_src, w_dst, inbuf, outbuf, in_sem, out_sem, zero_ref=None, zero_sem=None):
    e0, rows, n_items, fill = side
    d = w_src.shape[1]
    n_g = w_src.shape[2] // GROUP
    n_slab, slab_w = w_dst.shape[1], w_dst.shape[3]
    per_slab = slab_w // GROUP
    n_in = REGROUP_LOOKAHEAD + 1
    last = n_items - 1

    def coords(it):
        row = it * rows
        return e0 + row // d, pl.multiple_of(row % d, rows)

    def in_copies(it, slot):
        e, r = coords(it)
        return [pltpu.make_async_copy(w_src.at[e, pl.ds(r, rows), :], inbuf.at[slot], in_sem.at[slot])]

    def out_copies(it, slot):
        e, r = coords(it)
        return [pltpu.make_async_copy(outbuf.at[slot, j], w_dst.at[e, j, pl.ds(r, rows), :], out_sem.at[slot])
                for j in range(n_slab)]

    def prologue():
        outbuf[...] = jnp.zeros_like(outbuf)
        for it in range(min(REGROUP_LOOKAHEAD, n_items)):
            for cp in in_copies(it, it % n_in):
                cp.start()
        for slot in range(2):
            for cp in out_copies(min(slot, last), slot):
                cp.start()
        if fill:
            zero_ref[...] = jnp.zeros_like(zero_ref)
            for cp in fill_copies():
                cp.start()

    def fill_copies():
        return [pltpu.make_async_copy(zero_ref, w_dst.at[e, j], zero_sem) for e in fill for j in range(n_slab)]

    def item_wait(it):
        for cp in in_copies(it, it % n_in):
            cp.wait()
        for cp in out_copies(it, it % 2):
            cp.wait()

    def item_compute(it):
        x = inbuf[it % n_in]
        stacked = jnp.concatenate([x[:, GROUP * g:GROUP * (g + 1)] for g in range(n_g)], axis=0)
        res = jnp.dot(stacked.astype(BF16), perm_ref[...], preferred_element_type=F32).astype(BF16)
        for g in range(n_g):
            outbuf[it % 2, g // per_slab, :, pl.ds(GROUP * (g % per_slab), GROUP)] = res[g * rows:(g + 1) * rows, :]

    def item_start(it):
        ahead = jnp.minimum(it + REGROUP_LOOKAHEAD, last)
        for cp in in_copies(ahead, (it + REGROUP_LOOKAHEAD) % n_in):
            cp.start()
        for cp in out_copies(it, it % 2):
            cp.start()

    def epilogue():
        for slot in range(2):
            for cp in out_copies(last, slot):
                cp.wait()
        for k in range(1, REGROUP_LOOKAHEAD + 1):
            for cp in in_copies(last, (last + k) % n_in):
                cp.wait()
        if fill:
            for cp in fill_copies():
                cp.wait()

    return prologue, (item_wait, item_compute, item_start), epilogue


def _recurrence_kernel(n_chunks, dv, side, *refs):
    q_ref, kf_ref, kb_ref, v_ref, gf_ref, gb_ref, og_ref, ng_ref = refs[:8]
    refs = refs[8:]
    if side is not None:
        perm_ref, w_src = refs[:2]
        n_tail = 13 + (2 if side[3] else 0)
        refs = refs[len(refs) - n_tail:]
        o_ref, w_dst, of_ref, ob_ref, st_ref, sc_ref, qin_ref, kout_ref, dec_ref = refs[:9]
        side_prologue, side_item, side_epilogue = _regroup_side_task(side, perm_ref, w_src, w_dst, *refs[9:])
        step = pl.program_id(0) * pl.num_programs(1) + pl.program_id(1)
        n_steps = pl.num_programs(0) * pl.num_programs(1)
        pl.when(step == 0)(side_prologue)
    else:
        o_ref, of_ref, ob_ref, st_ref, sc_ref, qin_ref, kout_ref, dec_ref = refs
    n_heads = q_ref.shape[1] // HEAD_DK
    lanes = [(rev, k_ref, g_ref, out_ref, hd)
             for rev, k_ref, g_ref, out_ref in ((False, kf_ref, gf_ref, of_ref), (True, kb_ref, gb_ref, ob_ref))
             for hd in range(n_heads)]
    st_ref[...] = jnp.zeros_like(st_ref)

    def rows_of(c, rev):
        cc = (n_chunks - 1 - c) if rev else c
        return pl.ds(pl.multiple_of(cc * CHUNK, CHUNK), CHUNK)

    def kcols(hd):
        return slice(HEAD_DK * hd, HEAD_DK * (hd + 1))

    def vcols(hd):
        return slice(dv * hd, dv * (hd + 1))

    def prepare(c):
        res = []
        for rev, k_ref, g_ref, _, hd in lanes:
            rows = rows_of(c, rev)
            res.append(_chunk_prepare(q_ref[rows, kcols(hd)].astype(F32), k_ref[rows, kcols(hd)].astype(F32),
                                      g_ref[rows, kcols(hd)], rev))
        return res

    def stash(res):
        for d, (scores, q_in, k_out, decay) in enumerate(res):
            sc_ref[d] = scores
            qin_ref[d] = q_in
            kout_ref[d] = k_out
            dec_ref[d] = jnp.broadcast_to(decay, (SUBLANES, LANES))

    def unstash():
        return [(sc_ref[d], qin_ref[d], kout_ref[d], dec_ref[d][0:1, :]) for d in range(len(lanes))]

    def apply(c, staged):
        for d, ((rev, _, _, out_ref, hd), (scores, q_in, k_out, decay)) in enumerate(zip(lanes, staged)):
            rows = rows_of(c, rev)
            out_ref[rows, vcols(hd)] = _chunk_apply(scores, q_in, k_out, decay, v_ref[rows, vcols(hd)],
                                                    st_ref.at[d])

    stash(prepare(0))

    def body(c, carry):
        if side is not None:
            side_item[0](step * n_chunks + c)
        staged = unstash()
        nxt = prepare(c + 1)
        apply(c, staged)
        stash(nxt)
        if side is not None:
            side_item[1](step * n_chunks + c)
            side_item[2](step * n_chunks + c)
        return carry

    lax.fori_loop(0, n_chunks - 1, body, 0)
    if side is not None:
        side_item[0](step * n_chunks + n_chunks - 1)
        side_item[1](step * n_chunks + n_chunks - 1)
        side_item[2](step * n_chunks + n_chunks - 1)
    apply(n_chunks - 1, unstash())
    if side is not None:
        pl.when(step == n_steps - 1)(side_epilogue)
    for hd in range(n_heads):
        o = of_ref[:, vcols(hd)] + ob_ref[:, vcols(hd)]
        o_ref[:, vcols(hd)] = (_rms(o, ng_ref[...]) * og_ref[:, vcols(hd)].astype(F32)).astype(BF16)


RECURRENCE_HEADS_PER_STEP = 2


def _recurrence_steps(batch, seq, heads):
    hp = RECURRENCE_HEADS_PER_STEP if heads % RECURRENCE_HEADS_PER_STEP == 0 else 1
    return hp, batch * (heads // hp) * (seq // CHUNK)


def _recurrence(z, gf, gb, norm_g, batch, seq, heads, dv, cols, gb_off=0, name="", regroup=None):
    n_tok = batch * seq
    n_chunks = seq // CHUNK
    q_col, kf_col, kb_col, v_col, og_col = cols
    hp, n_items = _recurrence_steps(batch, seq, heads)

    def spec(width, col=0):
        width = width * hp
        assert col % width == 0
        return pl.BlockSpec((seq, width), lambda b, h, off=col // width: (b, h + off))

    in_specs = [spec(HEAD_DK, q_col), spec(HEAD_DK, kf_col), spec(HEAD_DK, kb_col), spec(dv, v_col),
                spec(HEAD_DK), spec(HEAD_DK, gb_off * HEAD_DK), spec(dv, og_col),
                pl.BlockSpec((1, dv), lambda b, h: (0, 0))]
    operands = [z, z, z, z, gf, gb, z, norm_g.reshape(1, dv).astype(F32)]
    out_specs = [spec(dv)]
    out_shape = [jax.ShapeDtypeStruct((n_tok, heads * dv), BF16)]
    scratch = [pltpu.VMEM((seq, hp * dv), F32), pltpu.VMEM((seq, hp * dv), F32),
               pltpu.VMEM((2 * hp, dv, HEAD_DK), F32),
               pltpu.VMEM((2 * hp, CHUNK, CHUNK), BF16), pltpu.VMEM((2 * hp, CHUNK, HEAD_DK), BF16),
               pltpu.VMEM((2 * hp, CHUNK, HEAD_DK), BF16), pltpu.VMEM((2 * hp, SUBLANES, LANES), F32)]
    side, aliases = None, {}
    if regroup is not None:
        w_gate_up, e0, n_e, partial = regroup
        n_exp, d, two_f = w_gate_up.shape
        slab_w = 2 * _divisor_tile(two_f // 2, FFN_FF_TILE)
        rows = n_e * d // n_items
        assert rows * n_items == n_e * d and d % rows == 0 and rows % (2 * SUBLANES) == 0
        fill = () if partial is not None else tuple(e for e in range(n_exp) if not e0 <= e < e0 + n_e)
        side = (e0, rows, n_items, fill)
        in_specs += [pl.BlockSpec((GROUP, GROUP), lambda b, h: (0, 0)), pl.BlockSpec(memory_space=pl.ANY)]
        operands += [_regroup_permutation(), w_gate_up]
        if partial is not None:
            aliases = {len(operands): 1}
            in_specs.append(pl.BlockSpec(memory_space=pl.ANY))
            operands.append(partial)
        out_specs.append(pl.BlockSpec(memory_space=pl.ANY))
        out_shape.append(jax.ShapeDtypeStruct((n_exp, two_f // slab_w, d, slab_w), BF16))
        scratch += [pltpu.VMEM((REGROUP_LOOKAHEAD + 1, rows, two_f), F32),
                    pltpu.VMEM((2, two_f // slab_w, rows, slab_w), BF16),
                    pltpu.SemaphoreType.DMA((REGROUP_LOOKAHEAD + 1,)), pltpu.SemaphoreType.DMA((2,))]
        if fill:
            scratch += [pltpu.VMEM((d, slab_w), BF16), pltpu.SemaphoreType.DMA(())]

    res = pl.pallas_call(
        functools.partial(_recurrence_kernel, n_chunks, dv, side),
        grid=(batch, heads // hp),
        in_specs=in_specs,
        out_specs=out_specs,
        out_shape=out_shape,
        scratch_shapes=scratch,
        input_output_aliases=aliases,
        compiler_params=_params(("arbitrary", "arbitrary")),
        name=name,
    )(*operands)
    return res if regroup is not None else res[0]


MIX_SUB_ROWS = 256


def _mix_out_kernel(oa_ref, ob_ref, sga_ref, sgb_ref, x_ref, wpa_ref, wpb_ref, wo_ref, g2_ref, rw2_ref, rw1_ref,
                    rb_ref, h_ref, hn_ref, w4_ref, idx_ref, sel_ref):
    tm = x_ref.shape[0]
    sub = min(tm, MIX_SUB_ROWS)
    for r in range(tm // sub):
        _mix_out_rows(pl.ds(r * sub, sub), oa_ref, ob_ref, sga_ref, sgb_ref, x_ref, wpa_ref, wpb_ref, wo_ref,
                      g2_ref, rw2_ref, rw1_ref, rb_ref, h_ref, hn_ref, w4_ref, idx_ref, sel_ref)


def _mix_out_rows(rows, oa_ref, ob_ref, sga_ref, sgb_ref, x_ref, wpa_ref, wpb_ref, wo_ref, g2_ref, rw2_ref,
                  rw1_ref, rb_ref, h_ref, hn_ref, w4_ref, idx_ref, sel_ref):
    ya = jnp.dot(oa_ref[rows, :], wpa_ref[...], preferred_element_type=F32)
    yb = jnp.dot(ob_ref[rows, :], wpb_ref[...], preferred_element_type=F32)
    merged = sga_ref[rows, :].astype(F32) * ya + sgb_ref[rows, :].astype(F32) * yb
    h = x_ref[rows, :] + jnp.dot(merged.astype(BF16), wo_ref[...], preferred_element_type=F32)
    h_ref[rows, :] = h
    hn = _rms(h, g2_ref[...])
    hn_ref[rows, :] = _pack_bf16_pairs(hn)

    hn_hi = hn.astype(BF16)
    hn_lo = (hn - hn_hi.astype(F32)).astype(BF16)
    two = jnp.dot(hn_hi, rw2_ref[...], preferred_element_type=F32)
    logits = (two[:, :LANES] + two[:, LANES:] + jnp.dot(hn_lo, rw1_ref[...], preferred_element_type=F32)
              + rb_ref[...])
    lane = lax.broadcasted_iota(I32, logits.shape, 1)
    lane_f = lane.astype(F32)
    work = logits
    vals, idxs = [], []
    for _ in range(TOP_K):
        m = jnp.max(work, axis=-1, keepdims=True)
        idx = jnp.min(jnp.where(work == m, lane_f, float(LANES)), axis=-1, keepdims=True)
        vals.append(m)
        idxs.append(idx)
        work = jnp.where(lane_f == idx, -jnp.inf, work)
    es = [jnp.exp(v - vals[0]) for v in vals]
    denom = es[0]
    for e in es[1:]:
        denom = denom + e
    w4 = jnp.zeros(logits.shape, F32)
    i4 = jnp.zeros(logits.shape, F32)
    sel = jnp.zeros(logits.shape, F32)
    for k in range(TOP_K):
        w4 = jnp.where(lane == k, es[k] / denom, w4)
        i4 = jnp.where(lane == k, idxs[k], i4)
        sel = jnp.where(lane_f == idxs[k], 1.0, sel)
    w4_ref[rows, :] = w4
    idx_ref[rows, :] = i4.astype(I32)
    sel_ref[rows, :] = sel.astype(BF16)


def _mix_out(oa, ob, z, gate_col, x2, w_proj_a, w_proj_b, w_out, norm2_g, router_w, router_b):
    n_tok, d = x2.shape
    hv, gv = oa.shape[1], ob.shape[1]
    n_exp = router_w.shape[1]
    tm = _divisor_tile(n_tok, 2 * MIX_SUB_ROWS)
    rw = jnp.pad(router_w.astype(F32), ((0, 0), (0, LANES - n_exp)))
    rw_hi = rw.astype(BF16)
    rw_lo = (rw - rw_hi.astype(F32)).astype(BF16)
    rb = jnp.concatenate([router_b.astype(F32), jnp.full((LANES - n_exp,), NEG_BIG, F32)])[None, :]
    assert gate_col % d == 0
    gate_blk = gate_col // d

    def rows(width, off=0):
        return pl.BlockSpec((tm, width), lambda i, off=off: (i, off))

    def whole(r, c):
        return pl.BlockSpec((r, c), lambda i: (0, 0), pipeline_mode=pl.Buffered(1))

    return pl.pallas_call(
        _mix_out_kernel,
        grid=(n_tok // tm,),
        in_specs=[rows(hv), rows(gv), rows(d, gate_blk), rows(d, gate_blk + 1), rows(d),
                  whole(hv, d), whole(gv, d), whole(d, d), whole(1, d), whole(d, 2 * LANES), whole(d, LANES),
                  whole(1, LANES)],
        out_specs=[rows(d), rows(d // 2), rows(LANES), rows(LANES), rows(LANES)],
        out_shape=[jax.ShapeDtypeStruct((n_tok, d), F32),
                   jax.ShapeDtypeStruct((n_tok, d // 2), U32),
                   jax.ShapeDtypeStruct((n_tok, LANES), F32),
                   jax.ShapeDtypeStruct((n_tok, LANES), I32),
                   jax.ShapeDtypeStruct((n_tok, LANES), BF16)],
        compiler_params=_params(("arbitrary",)),
        name="mix_out",
    )(oa, ob, z, z, x2, w_proj_a.astype(BF16), w_proj_b.astype(BF16), w_out.astype(BF16),
      norm2_g.reshape(1, d).astype(F32), jnp.concatenate([rw_hi, rw_lo], axis=1), rw_hi, rb)


def _route_rank_kernel(row_block, sel_ref, idx_ref, tri_ref, dest_ref, cnt_ref, rank_ref, carry_ref, start_ref):
    p = pl.program_id(0)
    i = pl.program_id(1)
    tb = sel_ref.shape[0]
    rows = pl.ds(pl.multiple_of(i * tb, tb), tb)

    @pl.when((p == 0) & (i == 0))
    def _():
        carry_ref[...] = jnp.zeros_like(carry_ref)

    @pl.when(p == 0)
    def _():
        sel = sel_ref[...]
        before = jnp.dot(tri_ref[...], sel, preferred_element_type=F32)
        rank_ref[rows, :] = before + carry_ref[0:1, :]
        carry_ref[...] = carry_ref[...] + jnp.sum(sel.astype(F32), axis=0, keepdims=True)

    @pl.when((p == 1) & (i == 0))
    def _():
        counts = carry_ref[...]
        padded = jnp.ceil(counts * (1.0 / row_block)) * row_block
        lane = lax.broadcasted_iota(I32, padded.shape, 1)
        run = padded
        s = 1
        while s < LANES:
            run = run + jnp.where(lane >= s, pltpu.roll(run, s, 1), 0.0)
            s *= 2
        start_ref[...] = run - padded

    @pl.when(p == 1)
    def _():
        pos = rank_ref[rows, :] + start_ref[0:1, :]
        idx = idx_ref[...]
        lane = lax.broadcasted_iota(I32, pos.shape, 1)
        dest = jnp.zeros(pos.shape, F32)
        for k in range(TOP_K):
            hit = lane == idx[:, k:k + 1]
            val = jnp.sum(jnp.where(hit, pos, 0.0), axis=-1, keepdims=True)
            dest = jnp.where(lane == k, val, dest)
        dest_ref[...] = dest.astype(I32)
        cnt_ref[...] = carry_ref[...]


def _route_rank(sel, idx4, row_block):
    n_tok = sel.shape[0]
    tb = _divisor_tile(n_tok, 256)
    tri = (lax.broadcasted_iota(I32, (tb, tb), 0) > lax.broadcasted_iota(I32, (tb, tb), 1)).astype(BF16)
    return pl.pallas_call(
        functools.partial(_route_rank_kernel, row_block),
        grid=(2, n_tok // tb),
        in_specs=[pl.BlockSpec((tb, LANES), lambda p, i: (i, 0)),
                  pl.BlockSpec((tb, LANES), lambda p, i: (i, 0)),
                  pl.BlockSpec((tb, tb), lambda p, i: (0, 0))],
        out_specs=[pl.BlockSpec((tb, LANES), lambda p, i: (i * p, 0)),
                   pl.BlockSpec((SUBLANES, LANES), lambda p, i: (0, 0))],
        out_shape=[jax.ShapeDtypeStruct((n_tok, LANES), I32),
                   jax.ShapeDtypeStruct((SUBLANES, LANES), F32)],
        scratch_shapes=[pltpu.VMEM((n_tok, LANES), F32), pltpu.VMEM((SUBLANES, LANES), F32),
                        pltpu.VMEM((SUBLANES, LANES), F32)],
        compiler_params=_params(("arbitrary", "arbitrary")),
        name="route_rank",
    )(sel, idx4, tri)


def _dispatch_kernel(bv_ref, dest_ref, hn_ref, xs_ref, zero_ref, sem):
    tt = hn_ref.shape[0]
    row_block = zero_ref.shape[0]
    n_blocks = bv_ref.shape[0]

    @pl.when(pl.program_id(0) == 0)
    def _():
        zero_ref[...] = jnp.zeros_like(zero_ref)

        def zero_copy(blk):
            return pltpu.make_async_copy(zero_ref, xs_ref.at[pl.ds(blk * row_block, row_block), :], sem)

        for blk in range(n_blocks):
            pl.when(bv_ref[blk] < row_block)(zero_copy(blk).start)
        for blk in range(n_blocks):
            pl.when(bv_ref[blk] < row_block)(zero_copy(blk).wait)

    for t in range(tt):
        for k in range(TOP_K):
            pltpu.make_async_copy(hn_ref.at[pl.ds(t, 1), :], xs_ref.at[pl.ds(dest_ref[TOP_K * t + k], 1), :],
                                  sem).start(priority=(TOP_K * t + k) % 2)
    for k in range(TOP_K):
        pltpu.make_async_copy(hn_ref, xs_ref.at[pl.ds(0, tt), :], sem).wait()


def _dispatch(block_valid, dest_flat, hn_packed, row_block):
    n_tok, half = hn_packed.shape
    tt = _divisor_tile(n_tok, 512)
    n_rows = block_valid.shape[0] * row_block
    grid_spec = pltpu.PrefetchScalarGridSpec(
        num_scalar_prefetch=1,
        grid=(n_tok // tt,),
        in_specs=[pl.BlockSpec((tt * TOP_K,), lambda i, bv: (i,), memory_space=pltpu.SMEM),
                  pl.BlockSpec((tt, half), lambda i, bv: (i, 0))],
        out_specs=pl.BlockSpec(memory_space=pl.ANY),
        scratch_shapes=[pltpu.VMEM((row_block, half), U32), pltpu.SemaphoreType.DMA(())],
    )
    return pl.pallas_call(
        _dispatch_kernel,
        grid_spec=grid_spec,
        out_shape=jax.ShapeDtypeStruct((n_rows, half), U32),
        compiler_params=_params(("arbitrary",)),
        name="dispatch",
    )(block_valid, dest_flat, hn_packed)


GROUP = 2 * LANES


def _regroup_kernel(w_ref, p_ref, o_ref):
    p = p_ref[...]
    for g in range(w_ref.shape[1] // GROUP):
        cols = slice(GROUP * g, GROUP * (g + 1))
        o_ref[:, cols] = jnp.dot(w_ref[:, cols].astype(BF16), p, preferred_element_type=F32).astype(BF16)


FFN_FF_TILE = 512


def _regroup_split(w_shape, items_a, items_b):
    n_exp, d, _ = w_shape

    def slice_rows(n_e, items):
        rows = n_e * d // items
        ok = items >= 2 and rows * items == n_e * d and rows % (2 * SUBLANES) == 0 and d % rows == 0
        return rows if ok else None

    best = None
    for n_a in range(1, n_exp):
        ra, rb = slice_rows(n_a, items_a), slice_rows(n_exp - n_a, items_b)
        if ra is not None and rb is not None and (best is None or max(ra, rb) < best[0]):
            best = (max(ra, rb), n_a)
    return None if best is None else best[1]


def _regroup_permutation():
    src = lax.broadcasted_iota(I32, (GROUP, GROUP), 0)
    dst = lax.broadcasted_iota(I32, (GROUP, GROUP), 1)
    return (dst == (src // 2) + LANES * (src % 2)).astype(BF16)


def _regroup_gate_up(w_gate_up):
    n_exp, d, two_f = w_gate_up.shape
    tr = _divisor_tile(d, 2048)
    tc = 2 * _divisor_tile(two_f // 2, FFN_FF_TILE)
    perm = _regroup_permutation()
    return pl.pallas_call(
        _regroup_kernel,
        grid=(n_exp, d // tr, two_f // tc),
        in_specs=[pl.BlockSpec((None, tr, tc), lambda e, i, j: (e, i, j)),
                  pl.BlockSpec((GROUP, GROUP), lambda e, i, j: (0, 0))],
        out_specs=pl.BlockSpec((None, None, tr, tc), lambda e, i, j: (e, j, i, 0)),
        out_shape=jax.ShapeDtypeStruct((n_exp, two_f // tc, d, tc), BF16),
        compiler_params=_params(("arbitrary", "arbitrary", "arbitrary")),
        name="regroup_gate_up",
    )(w_gate_up, perm)


def _regroup_bias(b_gate_up):
    n_exp, two_f = b_gate_up.shape
    b = b_gate_up.reshape(n_exp, two_f // GROUP, LANES, 2)
    return jnp.swapaxes(b, 2, 3).reshape(n_exp, 1, two_f).astype(F32)


FFN_ROW_BLOCK = 1024
FFN_SUB_ROWS = 256


def _expert_ffn_kernel(n_f, be_ref, bv_ref, xs_ref, wgu_ref, wd_ref, bgu_ref, bd_ref,
                       out_ref, x_ref, acc_ref, wdb_ref):
    del be_ref
    b = pl.program_id(0)
    f = pl.program_id(1)
    n_rows = xs_ref.shape[0]
    sub = min(n_rows, FFN_SUB_ROWS)
    valid = bv_ref[b]
    n_sub = (valid + sub - 1) // sub
    full = n_sub == n_rows // sub
    partial = (n_sub > 0) & jnp.logical_not(full)
    first = f == 0
    last = f == n_f - 1

    def unpack():
        hi, lo = _unpack_bf16_pairs(xs_ref[...])
        return jnp.concatenate([hi.astype(BF16), lo.astype(BF16)], axis=1)

    def down(x, wd):
        gu = jnp.dot(x, wgu_ref[...], preferred_element_type=F32) + bgu_ref[...]
        acts = []
        for g in range(gu.shape[1] // GROUP):
            gate = jnp.minimum(gu[:, GROUP * g:GROUP * g + LANES], SWIGLU_LIMIT)
            up = jnp.clip(gu[:, GROUP * g + LANES:GROUP * (g + 1)], -SWIGLU_LIMIT, SWIGLU_LIMIT)
            acts.append(((up + 1.0) * (gate * jax.nn.sigmoid(SWIGLU_ALPHA * gate))).astype(BF16))
        return jnp.dot(jnp.concatenate(acts, axis=1), wd, preferred_element_type=F32)

    def full_step(is_first, is_last):
        def run():
            wd = wd_ref[...].astype(BF16)
            if is_first:
                x = unpack()
                x_ref[...] = x
                y = down(x, wd)
            else:
                y = acc_ref[...] + down(x_ref[...], wd)
            if is_last:
                out_ref[...] = _pack_bf16_pairs(y + bd_ref[...])
            else:
                acc_ref[...] = y
        return run

    if n_f == 1:
        pl.when(full)(full_step(True, True))
    else:
        pl.when(full & first)(full_step(True, False))
        pl.when(full & last)(full_step(False, True))
        if n_f > 2:
            pl.when(full & jnp.logical_not(first | last))(full_step(False, False))

    @pl.when(partial & first)
    def _():
        x_ref[...] = unpack()
        acc_ref[...] = jnp.zeros_like(acc_ref)

    @pl.when(partial)
    def _():
        wdb_ref[...] = wd_ref[...].astype(BF16)

        def body(s, carry):
            rows = pl.ds(pl.multiple_of(s * sub, sub), sub)
            acc_ref[rows, :] += down(x_ref[rows, :], wdb_ref[...])
            return carry
        lax.fori_loop(0, n_sub, body, 0)

    @pl.when(partial & last)
    def _():
        out_ref[...] = _pack_bf16_pairs(acc_ref[...] + bd_ref[...])

    @pl.when((valid == 0) & last)
    def _():
        out_ref[...] = jnp.zeros_like(out_ref)


def _expert_ffn(block_e, block_valid, xs, wgu, wd, bgu, bd, row_block):
    n_rows, half = xs.shape
    d = 2 * half
    d_ff = wd.shape[1]
    tf = wgu.shape[3] // 2
    n_blocks = n_rows // row_block
    grid_spec = pltpu.PrefetchScalarGridSpec(
        num_scalar_prefetch=2,
        grid=(n_blocks, d_ff // tf),
        in_specs=[
            pl.BlockSpec((row_block, half), lambda b, f, be, bv: (b, 0)),
            pl.BlockSpec((None, None, d, 2 * tf), lambda b, f, be, bv: (be[b], f, 0, 0)),
            pl.BlockSpec((None, tf, d), lambda b, f, be, bv: (be[b], f, 0)),
            pl.BlockSpec((None, 1, 2 * tf), lambda b, f, be, bv: (be[b], 0, f)),
            pl.BlockSpec((None, 1, d), lambda b, f, be, bv: (be[b], 0, 0)),
        ],
        out_specs=pl.BlockSpec((row_block, half), lambda b, f, be, bv: (b, 0)),
        scratch_shapes=[pltpu.VMEM((row_block, d), BF16), pltpu.VMEM((row_block, d), F32),
                        pltpu.VMEM((tf, d), BF16)],
    )
    return pl.pallas_call(
        functools.partial(_expert_ffn_kernel, d_ff // tf),
        grid_spec=grid_spec,
        out_shape=jax.ShapeDtypeStruct((n_rows, half), U32),
        compiler_params=_params(("arbitrary", "arbitrary")),
        name="expert_ffn",
    )(block_e, block_valid, xs, wgu, wd, bgu, bd)


def _combine_kernel(cur_ref, nxt_ref, h_ref, w4_ref, fg_ref, rows_hbm_ref, o_ref, buf_ref, sem):
    i = pl.program_id(0)
    n = pl.num_programs(0)
    tt = h_ref.shape[0]
    half = buf_ref.shape[3]

    def issue(dest_ref, s):
        for t in range(tt):
            for k in range(TOP_K):
                pltpu.make_async_copy(rows_hbm_ref.at[pl.ds(dest_ref[TOP_K * t + k], 1), :],
                                      buf_ref.at[s, k, pl.ds(t, 1), :],
                                      sem.at[s]).start(priority=(TOP_K * t + k) % 2)

    def drain(s):
        for k in range(TOP_K):
            pltpu.make_async_copy(rows_hbm_ref.at[pl.ds(0, tt), :], buf_ref.at[s, k], sem.at[s]).wait()

    @pl.when(i == 0)
    def _():
        issue(cur_ref, 0)

    def step(slot):
        drain(slot)
        issue(nxt_ref, 1 - slot)
        w4 = w4_ref[...]
        y_hi = jnp.zeros((tt, half), F32)
        y_lo = jnp.zeros((tt, half), F32)
        for k in range(TOP_K):
            hi, lo = _unpack_bf16_pairs(buf_ref[slot, k])
            wk = w4[:, k:k + 1]
            y_hi = y_hi + wk * hi
            y_lo = y_lo + wk * lo
        h = h_ref[...] + jnp.concatenate([y_hi, y_lo], axis=1)
        o_ref[...] = _rms(h, fg_ref[...])

        @pl.when(i == n - 1)
        def _():
            drain(1 - slot)

    for slot in range(2):
        pl.when(i % 2 == slot)(functools.partial(step, slot))


def _combine(dest_flat, h, w4, final_g, rows_packed):
    n_tok, d = h.shape
    half = d // 2
    tt = _divisor_tile(n_tok, 256)
    n_steps = n_tok // tt
    return pl.pallas_call(
        _combine_kernel,
        grid=(n_steps,),
        in_specs=[pl.BlockSpec((tt * TOP_K,), lambda i: (i,), memory_space=pltpu.SMEM),
                  pl.BlockSpec((tt * TOP_K,), lambda i: (jnp.minimum(i + 1, n_steps - 1),),
                               memory_space=pltpu.SMEM),
                  pl.BlockSpec((tt, d), lambda i: (i, 0)),
                  pl.BlockSpec((tt, LANES), lambda i: (i, 0)),
                  pl.BlockSpec((1, d), lambda i: (0, 0)),
                  pl.BlockSpec(memory_space=pl.ANY)],
        out_specs=pl.BlockSpec((tt, d), lambda i: (i, 0)),
        out_shape=jax.ShapeDtypeStruct((n_tok, d), F32),
        scratch_shapes=[pltpu.VMEM((2, TOP_K, tt, half), U32), pltpu.SemaphoreType.DMA((2,))],
        compiler_params=_params(("arbitrary",)),
        name="combine",
    )(dest_flat, dest_flat, h, w4, final_g.reshape(1, d).astype(F32), rows_packed)


def _layer(h2, batch, seq, lb, norm1_g, w_in, hg_norm_g, gate_w2, gate_b, gla_norm_g, w_proj_a, w_proj_b,
           w_out, norm2_g, router_w, router_b, w_gate_up, b_gate_up, w_down, b_down, out_norm_g):
    n_tok, d = h2.shape
    hf = lb.shape[1]
    hv = w_proj_a.shape[0]
    gk = gate_w2.shape[2]
    gv = w_proj_b.shape[0]
    rank = gate_w2.shape[1]
    hg_heads, gla_heads = hf // HEAD_DK, gk // HEAD_DK
    n_exp = router_w.shape[1]

    z, gf, gb, gg = _in_proj(h2, norm1_g, w_in, lb, gate_w2, gate_b, (hf, hv, gk, gv, rank))
    a_cols = (0, hf, 2 * hf, 3 * hf, 3 * hf + hv)
    b0 = 3 * hf + 2 * hv
    b_cols = (b0, b0 + gk, b0 + gk, b0 + 2 * gk, b0 + 2 * gk + gv)
    experts_a = _regroup_split(w_gate_up.shape, _recurrence_steps(batch, seq, hg_heads)[1],
                               _recurrence_steps(batch, seq, gla_heads)[1])
    if experts_a is None:
        oa = _recurrence(z, gf, gb, hg_norm_g, batch, seq, hg_heads, hv // hg_heads, a_cols,
                         name="hgrn_recurrence")
        ob = _recurrence(z, gg, gg, gla_norm_g, batch, seq, gla_heads, gv // gla_heads, b_cols,
                         gb_off=gla_heads, name="gla_recurrence")
        wgu = _regroup_gate_up(w_gate_up)
    else:
        oa, wgu = _recurrence(z, gf, gb, hg_norm_g, batch, seq, hg_heads, hv // hg_heads, a_cols,
                              name="hgrn_recurrence", regroup=(w_gate_up, 0, experts_a, None))
        ob, wgu = _recurrence(z, gg, gg, gla_norm_g, batch, seq, gla_heads, gv // gla_heads, b_cols,
                              gb_off=gla_heads, name="gla_recurrence",
                              regroup=(w_gate_up, experts_a, n_exp - experts_a, wgu))
    h, hn_packed, w4, idx4, sel = _mix_out(oa, ob, z, b0 + 2 * gk + 2 * gv, h2, w_proj_a, w_proj_b, w_out,
                                           norm2_g, router_w, router_b)

    n_pairs = n_tok * TOP_K
    row_block = _divisor_tile(n_pairs, FFN_ROW_BLOCK)
    n_blocks = n_pairs // row_block + n_exp
    dest4, counts = _route_rank(sel, idx4, row_block)
    dest_flat = dest4[:, :TOP_K].reshape(-1)
    cnt = counts[0, :n_exp].astype(I32)
    blocks_e = (cnt + row_block - 1) // row_block
    end_blk = jnp.cumsum(blocks_e)
    start_blk = end_blk - blocks_e
    bidx = jnp.arange(n_blocks, dtype=I32)[:, None]
    block_e = jnp.minimum(jnp.sum((end_blk[None, :] <= bidx).astype(I32), axis=1), n_exp - 1)
    owned = (bidx >= start_blk[None, :]) & (bidx < end_blk[None, :])
    rows_left = jnp.clip(cnt[None, :] - (bidx - start_blk[None, :]) * row_block, 0, row_block)
    block_valid = jnp.sum(jnp.where(owned, rows_left, 0), axis=1).astype(I32)

    xs = _dispatch(block_valid, dest_flat, hn_packed, row_block)
    rows = _expert_ffn(block_e, block_valid, xs, wgu, w_down.astype(F32),
                       _regroup_bias(b_gate_up), b_down[:, None, :].astype(F32), row_block)
    return _combine(dest_flat, h, w4, out_norm_g, rows)


def kernel(x, norm1_g, w_in, hg_lb_logits, hg_norm_g, gla_gate_w2, gla_gate_b, gla_norm_g, w_proj_a, w_proj_b,
           w_out, norm2_g, router_w, router_b, w_gate_up, b_gate_up, w_down, b_down, final_norm_g):
    batch, seq, d = x.shape
    depth = w_in.shape[0]
    assert depth == 1, "the final RMSNorm is fused into the last layer's combine kernel"
    lb_all = jnp.cumsum(jax.nn.softmax(hg_lb_logits.astype(F32), axis=1), axis=1)
    h2 = x.reshape(batch * seq, d)
    out = _layer(h2, batch, seq, lb_all[:, 0], norm1_g[0], w_in[0], hg_norm_g[0], gla_gate_w2[0], gla_gate_b[0],
                 gla_norm_g[0], w_proj_a[0], w_proj_b[0], w_out[0], norm2_g[0], router_w[0], router_b[0],
                 w_gate_up[0], b_gate_up[0], w_down[0], b_down[0], final_norm_g)
    return out.reshape(batch, seq, d)
```

```python
import functools

import jax
import jax.numpy as jnp
from jax import lax
from jax.experimental import pallas as pl
from jax.experimental.pallas import tpu as pltpu

F32 = jnp.float32
BF16 = jnp.bfloat16
U32 = jnp.uint32
I32 = jnp.int32

LANES = 128
SUBLANES = 8
VMEM_LIMIT_BYTES = 56 * 1024 * 1024

HEAD_DK = 128
GLA_DV = 256
HG_DV = 128
GATE_NORM = 16.0
TOP_K = 4
SWIGLU_LIMIT = 7.0
SWIGLU_ALPHA = 1.702
RMS_EPS = 1e-5
CHUNK = 64
NEG_BIG = -1e30
LOG2_E = 1.4426950408889634


def _divisor_tile(n, pref):
    t = min(n, pref)
    while n % t:
        t //= 2
    return t


def _params(sem, vmem=VMEM_LIMIT_BYTES):
    return pltpu.CompilerParams(dimension_semantics=sem, vmem_limit_bytes=vmem)


def _pack_bf16_pairs(a):
    n = a.shape[-1] // 2
    hi = lax.bitcast_convert_type(a[:, :n].astype(BF16).astype(F32), U32)
    lo = lax.bitcast_convert_type(a[:, n:].astype(BF16).astype(F32), U32)
    return hi | (lo >> 16)


def _unpack_bf16_pairs(p):
    hi = lax.bitcast_convert_type(p & jnp.uint32(0xFFFF0000), F32)
    lo = lax.bitcast_convert_type(p << 16, F32)
    return hi, lo


def _rms(x, g):
    ms = jnp.mean(x * x, axis=-1, keepdims=True)
    return x * lax.rsqrt(ms + RMS_EPS) * g


def _log_sigmoid(x):
    return jnp.minimum(x, 0.0) - jnp.log1p(jnp.exp(-jnp.abs(x)))


def _in_proj_kernel(bounds, x_ref, g1_ref, w_ref, lb_ref, w2_ref, b2_ref,
                    z_ref, gf_ref, gb_ref, gg_ref, xn_ref):
    j = pl.program_id(1)

    @pl.when(j == 0)
    def _():
        xn_ref[...] = _rms(x_ref[...], g1_ref[...]).astype(BF16)

    scale = HEAD_DK ** -0.5

    def project():
        return jnp.dot(xn_ref[...], w_ref[...], preferred_element_type=F32)

    def silu(z):
        return z * jax.nn.sigmoid(z)

    def store(fn):
        def epilogue():
            z_ref[...] = fn(project()).astype(BF16)
        return epilogue

    def forget(g_ref):
        def epilogue():
            lb = lb_ref[0:1, :]
            f = lb + (1.0 - lb) * jax.nn.sigmoid(project())
            z_ref[...] = (1.0 - f).astype(BF16)
            g_ref[...] = jnp.log2(f)
        return epilogue

    def low_rank_gate():
        lr = project()[:, :LANES].astype(BF16)
        logits = jnp.dot(lr, w2_ref[...], preferred_element_type=F32) + b2_ref[...]
        gg_ref[...] = _log_sigmoid(logits) * (LOG2_E / GATE_NORM)

    epilogues = [
        store(lambda z: silu(z) * scale),
        forget(gf_ref),
        forget(gb_ref),
        store(lambda z: z),
        store(silu),
        store(lambda z: z * scale),
        store(lambda z: z),
        store(lambda z: z),
        store(silu),
        store(jax.nn.sigmoid),
        store(jax.nn.sigmoid),
        low_rank_gate,
    ]
    for k, epilogue in enumerate(epilogues):
        pl.when((j >= bounds[k]) & (j < bounds[k + 1]))(epilogue)


def _in_proj(x2, norm1_g, w_in, lb, gate_w2, gate_b, dims):
    n_tok, d = x2.shape
    hf, hv, gk, gv, rank = dims
    widths = [hf, hf, hf, hv, hv, gk, gk, gv, gv]
    off_lr = sum(widths)
    n_main = off_lr + 2 * d
    tn = _divisor_tile(gk, 256)
    tm = _divisor_tile(n_tok, 1024)
    widths_all = widths + [d, d, tn]
    w_lr = jnp.pad(w_in[:, off_lr:off_lr + 2 * rank], ((0, 0), (0, tn - 2 * rank)))
    w_cat = jnp.concatenate([w_in[:, :off_lr], w_in[:, off_lr + 2 * rank:], w_lr], axis=1).astype(BF16)
    n_col = w_cat.shape[1]
    w_tiles = w_cat.reshape(d, n_col // tn, tn).transpose(1, 0, 2)
    lb_tab = jnp.zeros((SUBLANES, n_col), F32)
    lb_tab = lb_tab.at[0, hf:2 * hf].set(lb[0]).at[0, 2 * hf:3 * hf].set(lb[1])
    w2 = jnp.zeros((LANES, 2 * gk), F32)
    w2 = w2.at[:rank, :gk].set(gate_w2[0]).at[rank:2 * rank, gk:].set(gate_w2[1]).astype(BF16)
    b2 = jnp.concatenate([gate_b[0], gate_b[1]])[None, :].astype(F32)

    bounds = [0]
    for w in widths_all:
        bounds.append(bounds[-1] + w // tn)
    n_j = bounds[-1]

    def seg_spec(k):
        s, n = bounds[k], bounds[k + 1] - bounds[k]
        return pl.BlockSpec((tm, tn), lambda i, j, s=s, n=n: (i, jnp.clip(j - s, 0, n - 1)))

    return pl.pallas_call(
        functools.partial(_in_proj_kernel, tuple(bounds)),
        grid=(n_tok // tm, n_j),
        in_specs=[
            pl.BlockSpec((tm, d), lambda i, j: (i, 0)),
            pl.BlockSpec((1, d), lambda i, j: (0, 0)),
            pl.BlockSpec((None, d, tn), lambda i, j: (j, 0, 0)),
            pl.BlockSpec((SUBLANES, tn), lambda i, j: (0, j)),
            pl.BlockSpec((LANES, 2 * gk), lambda i, j: (0, 0)),
            pl.BlockSpec((1, 2 * gk), lambda i, j: (0, 0)),
        ],
        out_specs=[pl.BlockSpec((tm, tn), lambda i, j: (i, jnp.minimum(j, n_main // tn - 1))),
                   seg_spec(1), seg_spec(2),
                   pl.BlockSpec((tm, 2 * gk), lambda i, j: (i, 0))],
        out_shape=[jax.ShapeDtypeStruct((n_tok, n_main), BF16),
                   jax.ShapeDtypeStruct((n_tok, hf), F32),
                   jax.ShapeDtypeStruct((n_tok, hf), F32),
                   jax.ShapeDtypeStruct((n_tok, 2 * gk), F32)],
        scratch_shapes=[pltpu.VMEM((tm, d), BF16)],
        compiler_params=_params(("arbitrary", "arbitrary")),
        name="in_proj",
    )(x2, norm1_g.reshape(1, d), w_tiles, lb_tab, w2, b2)


_NT_DIMS = (((1,), (1,)), ((), ()))
_TN_DIMS = (((0,), (0,)), ((), ()))


def _chunk_prepare(q, k, g, rev):
    nt = CHUNK // SUBLANES
    row = lax.broadcasted_iota(I32, (SUBLANES, LANES), 0)
    lrow = (SUBLANES - 1 - row) if rev else row

    def phys(jl):
        return nt - 1 - jl if rev else jl

    def prow(x, rl):
        r = SUBLANES - 1 - rl if rev else rl
        return x[r:r + 1, :]

    def tile(x, jl):
        p = phys(jl)
        return x[SUBLANES * p:SUBLANES * (p + 1), :]

    def scan(x):
        for s in (1, 2, 4):
            if rev:
                x = x + jnp.where(row < SUBLANES - s, pltpu.roll(x, SUBLANES - s, 0), 0.0)
            else:
                x = x + jnp.where(row >= s, pltpu.roll(x, s, 0), 0.0)
        return x

    gt = [tile(g, jl) for jl in range(nt)]
    qt = [tile(q, jl) for jl in range(nt)]
    kt = [tile(k, jl) for jl in range(nt)]
    ct = [scan(x) for x in gt]
    tot = [prow(c, SUBLANES - 1) for c in ct]
    suf = [t - c for t, c in zip(tot, ct)]

    def assemble(tiles):
        out = [None] * nt
        for jl, x in enumerate(tiles):
            out[phys(jl)] = jnp.zeros((SUBLANES, LANES), F32) if x is None else x
        return jnp.concatenate(out, axis=0).astype(BF16)

    def plus(a, b):
        return a if b is None else a + b

    levels = []

    for nb in (8, 4, 2):
        half = nb // 2
        qs, ks = [None] * nt, [None] * nt
        for base in range(0, nt, nb):
            acc = None
            for jl in range(base + half, base + nb):
                qs[jl] = qt[jl] * jnp.exp2(plus(ct[jl], acc))
                acc = plus(tot[jl], acc)
            acc = None
            for jl in range(base + half - 1, base - 1, -1):
                ks[jl] = kt[jl] * jnp.exp2(plus(suf[jl], acc))
                acc = plus(tot[jl], acc)
        levels.append((SUBLANES * nb, assemble(qs), assemble(ks)))

    for size in (8, 4):
        half = size // 2
        is_q = (lrow & (size - 1)) >= half
        qs, ks = [], []
        for jl in range(nt):
            if size == 8:
                ref = prow(ct[jl], half - 1)
            else:
                ref = jnp.where(lrow >= size, prow(ct[jl], size + half - 1), prow(ct[jl], half - 1))
            d = ct[jl] - ref
            e = jnp.exp2(jnp.where(is_q, d, -d))
            qs.append(jnp.where(is_q, qt[jl] * e, 0.0))
            ks.append(jnp.where(is_q, 0.0, kt[jl] * e))
        levels.append((size, assemble(qs), assemble(ks)))

    odd = (lrow & 1) == 1
    qs = [jnp.where(odd, qt[jl] * jnp.exp2(gt[jl]), 0.0) for jl in range(nt)]
    ks = [jnp.where(odd, 0.0, kt[jl]) for jl in range(nt)]
    levels.append((2, assemble(qs), assemble(ks)))
    levels.append((1, q.astype(BF16), k.astype(BF16)))

    ti = lax.broadcasted_iota(I32, (CHUNK, CHUNK), 0)
    si = lax.broadcasted_iota(I32, (CHUNK, CHUNK), 1)
    blk = ti ^ si
    scores = None
    for size, qm, km in levels:
        s_m = lax.dot_general(qm, km, _NT_DIMS, preferred_element_type=F32)
        scores = s_m if scores is None else jnp.where(blk < size, s_m, scores)

    q_in, k_out = [None] * nt, [None] * nt
    acc = None
    for jl in range(nt):
        q_in[jl] = qt[jl] * jnp.exp2(plus(ct[jl], acc))
        acc = plus(tot[jl], acc)
    total = acc
    acc = None
    for jl in range(nt - 1, -1, -1):
        k_out[jl] = kt[jl] * jnp.exp2(plus(suf[jl], acc))
        acc = plus(tot[jl], acc)
    return scores.astype(BF16), assemble(q_in), assemble(k_out), jnp.exp2(total)


def _chunk_apply(scores, q_in, k_out, decay, v, st_ref):
    st = st_ref[...]
    out = (jnp.dot(scores, v, preferred_element_type=F32)
           + lax.dot_general(q_in, st.astype(BF16), _NT_DIMS, preferred_element_type=F32))
    st_ref[...] = st * decay + lax.dot_general(v, k_out, _TN_DIMS, preferred_element_type=F32)
    return out


REGROUP_LOOKAHEAD = 3


def _regroup_side_task(side, perm_ref, w_src, w_dst, inbuf, outbuf, in_sem, out_sem):
    rows, n_items = side
    d = w_src.shape[1]
    n_g = w_src.shape[2] // GROUP
    n_slab, slab_w = w_dst.shape[1], w_dst.shape[3]
    per_slab = slab_w // GROUP
    n_in = REGROUP_LOOKAHEAD + 1
    last = n_items - 1

    def coords(it):
        row = it * rows
        return row // d, pl.multiple_of(row % d, rows)

    def in_copies(it, slot):
        e, r = coords(it)
        return [pltpu.make_async_copy(w_src.at[e, pl.ds(r, rows), :], inbuf.at[slot], in_sem.at[slot])]

    def out_copies(it, slot):
        e, r = coords(it)
        return [pltpu.make_async_copy(outbuf.at[slot, j], w_dst.at[e, j, pl.ds(r, rows), :], out_sem.at[slot])
                for j in range(n_slab)]

    def prologue():
        outbuf[...] = jnp.zeros_like(outbuf)
        for it in range(min(REGROUP_LOOKAHEAD, n_items)):
            for cp in in_copies(it, it % n_in):
                cp.start()
        for slot in range(2):
            for cp in out_copies(min(slot, last), slot):
                cp.start()

    def item_wait(it):
        for cp in in_copies(it, it % n_in):
            cp.wait()
        for cp in out_copies(it, it % 2):
            cp.wait()

    def item_compute(it):
        x = inbuf[it % n_in]
        stacked = jnp.concatenate([x[:, GROUP * g:GROUP * (g + 1)] for g in range(n_g)], axis=0)
        res = jnp.dot(stacked.astype(BF16), perm_ref[...], preferred_element_type=F32).astype(BF16)
        for g in range(n_g):
            outbuf[it % 2, g // per_slab, :, pl.ds(GROUP * (g % per_slab), GROUP)] = res[g * rows:(g + 1) * rows, :]

    def item_start(it):
        ahead = jnp.minimum(it + REGROUP_LOOKAHEAD, last)
        for cp in in_copies(ahead, (it + REGROUP_LOOKAHEAD) % n_in):
            cp.start()
        for cp in out_copies(it, it % 2):
            cp.start()

    def epilogue():
        for slot in range(2):
            for cp in out_copies(last, slot):
                cp.wait()
        for k in range(1, REGROUP_LOOKAHEAD + 1):
            for cp in in_copies(last, (last + k) % n_in):
                cp.wait()

    return prologue, (item_wait, item_compute, item_start), epilogue


def _recurrence_kernel(n_chunks, dv, side, *refs):
    q_ref, kf_ref, kb_ref, v_ref, gf_ref, gb_ref, og_ref, ng_ref = refs[:8]
    refs = refs[8:]
    if side is not None:
        perm_ref, w_src, o_ref, w_dst, of_ref, ob_ref, st_ref, sc_ref, qin_ref, kout_ref, dec_ref = refs[:11]
        side_prologue, side_item, side_epilogue = _regroup_side_task(side, perm_ref, w_src, w_dst, *refs[11:])
        step = pl.program_id(0) * pl.num_programs(1) + pl.program_id(1)
        n_steps = pl.num_programs(0) * pl.num_programs(1)
        pl.when(step == 0)(side_prologue)
    else:
        o_ref, of_ref, ob_ref, st_ref, sc_ref, qin_ref, kout_ref, dec_ref = refs
    n_heads = q_ref.shape[1] // HEAD_DK
    lanes = [(rev, k_ref, g_ref, out_ref, hd)
             for rev, k_ref, g_ref, out_ref in ((False, kf_ref, gf_ref, of_ref), (True, kb_ref, gb_ref, ob_ref))
             for hd in range(n_heads)]
    st_ref[...] = jnp.zeros_like(st_ref)

    def rows_of(c, rev):
        cc = (n_chunks - 1 - c) if rev else c
        return pl.ds(pl.multiple_of(cc * CHUNK, CHUNK), CHUNK)

    def kcols(hd):
        return slice(HEAD_DK * hd, HEAD_DK * (hd + 1))

    def vcols(hd):
        return slice(dv * hd, dv * (hd + 1))

    def prepare(c):
        res = []
        for rev, k_ref, g_ref, _, hd in lanes:
            rows = rows_of(c, rev)
            res.append(_chunk_prepare(q_ref[rows, kcols(hd)].astype(F32), k_ref[rows, kcols(hd)].astype(F32),
                                      g_ref[rows, kcols(hd)], rev))
        return res

    def stash(res):
        for d, (scores, q_in, k_out, decay) in enumerate(res):
            sc_ref[d] = scores
            qin_ref[d] = q_in
            kout_ref[d] = k_out
            dec_ref[d] = jnp.broadcast_to(decay, (SUBLANES, LANES))

    def unstash():
        return [(sc_ref[d], qin_ref[d], kout_ref[d], dec_ref[d][0:1, :]) for d in range(len(lanes))]

    def apply(c, staged):
        for d, ((rev, _, _, out_ref, hd), (scores, q_in, k_out, decay)) in enumerate(zip(lanes, staged)):
            rows = rows_of(c, rev)
            out_ref[rows, vcols(hd)] = _chunk_apply(scores, q_in, k_out, decay, v_ref[rows, vcols(hd)],
                                                    st_ref.at[d])

    stash(prepare(0))

    def body(c, carry):
        if side is not None:
            side_item[0](step * n_chunks + c)
        staged = unstash()
        nxt = prepare(c + 1)
        apply(c, staged)
        stash(nxt)
        if side is not None:
            side_item[1](step * n_chunks + c)
            side_item[2](step * n_chunks + c)
        return carry

    lax.fori_loop(0, n_chunks - 1, body, 0)
    if side is not None:
        side_item[0](step * n_chunks + n_chunks - 1)
        side_item[1](step * n_chunks + n_chunks - 1)
        side_item[2](step * n_chunks + n_chunks - 1)
    apply(n_chunks - 1, unstash())
    if side is not None:
        pl.when(step == n_steps - 1)(side_epilogue)
    for hd in range(n_heads):
        o = of_ref[:, vcols(hd)] + ob_ref[:, vcols(hd)]
        o_ref[:, vcols(hd)] = (_rms(o, ng_ref[...]) * og_ref[:, vcols(hd)].astype(F32)).astype(BF16)


RECURRENCE_HEADS_PER_STEP = 2


def _recurrence_steps(batch, seq, heads):
    hp = RECURRENCE_HEADS_PER_STEP if heads % RECURRENCE_HEADS_PER_STEP == 0 else 1
    return hp, batch * (heads // hp) * (seq // CHUNK)


def _recurrence(z, gf, gb, norm_g, batch, seq, heads, dv, cols, gb_off=0, name="", regroup=None):
    n_tok = batch * seq
    n_chunks = seq // CHUNK
    q_col, kf_col, kb_col, v_col, og_col = cols
    hp, n_items = _recurrence_steps(batch, seq, heads)

    def spec(width, col=0):
        width = width * hp
        assert col % width == 0
        return pl.BlockSpec((seq, width), lambda b, h, off=col // width: (b, h + off))

    in_specs = [spec(HEAD_DK, q_col), spec(HEAD_DK, kf_col), spec(HEAD_DK, kb_col), spec(dv, v_col),
                spec(HEAD_DK), spec(HEAD_DK, gb_off * HEAD_DK), spec(dv, og_col),
                pl.BlockSpec((1, dv), lambda b, h: (0, 0))]
    operands = [z, z, z, z, gf, gb, z, norm_g.reshape(1, dv).astype(F32)]
    out_specs = [spec(dv)]
    out_shape = [jax.ShapeDtypeStruct((n_tok, heads * dv), BF16)]
    scratch = [pltpu.VMEM((seq, hp * dv), F32), pltpu.VMEM((seq, hp * dv), F32),
               pltpu.VMEM((2 * hp, dv, HEAD_DK), F32),
               pltpu.VMEM((2 * hp, CHUNK, CHUNK), BF16), pltpu.VMEM((2 * hp, CHUNK, HEAD_DK), BF16),
               pltpu.VMEM((2 * hp, CHUNK, HEAD_DK), BF16), pltpu.VMEM((2 * hp, SUBLANES, LANES), F32)]
    side = None
    if regroup is not None:
        n_exp, d, two_f = regroup.shape
        slab_w = 2 * _divisor_tile(two_f // 2, FFN_FF_TILE)
        side = (_regroup_slice_rows(regroup.shape, n_items), n_items)
        in_specs += [pl.BlockSpec((GROUP, GROUP), lambda b, h: (0, 0)), pl.BlockSpec(memory_space=pl.ANY)]
        operands += [_regroup_permutation(), regroup]
        out_specs.append(pl.BlockSpec(memory_space=pl.ANY))
        out_shape.append(jax.ShapeDtypeStruct((n_exp, two_f // slab_w, d, slab_w), BF16))
        scratch += [pltpu.VMEM((REGROUP_LOOKAHEAD + 1, side[0], two_f), F32),
                    pltpu.VMEM((2, two_f // slab_w, side[0], slab_w), BF16),
                    pltpu.SemaphoreType.DMA((REGROUP_LOOKAHEAD + 1,)), pltpu.SemaphoreType.DMA((2,))]

    res = pl.pallas_call(
        functools.partial(_recurrence_kernel, n_chunks, dv, side),
        grid=(batch, heads // hp),
        in_specs=in_specs,
        out_specs=out_specs,
        out_shape=out_shape,
        scratch_shapes=scratch,
        compiler_params=_params(("arbitrary", "arbitrary")),
        name=name,
    )(*operands)
    return res if regroup is not None else res[0]


MIX_SUB_ROWS = 256


def _mix_out_kernel(oa_ref, ob_ref, sga_ref, sgb_ref, x_ref, wpa_ref, wpb_ref, wo_ref, g2_ref, rw2_ref, rw1_ref,
                    rb_ref, h_ref, hn_ref, w4_ref, idx_ref, sel_ref):
    tm = x_ref.shape[0]
    sub = min(tm, MIX_SUB_ROWS)
    for r in range(tm // sub):
        _mix_out_rows(pl.ds(r * sub, sub), oa_ref, ob_ref, sga_ref, sgb_ref, x_ref, wpa_ref, wpb_ref, wo_ref,
                      g2_ref, rw2_ref, rw1_ref, rb_ref, h_ref, hn_ref, w4_ref, idx_ref, sel_ref)


def _mix_out_rows(rows, oa_ref, ob_ref, sga_ref, sgb_ref, x_ref, wpa_ref, wpb_ref, wo_ref, g2_ref, rw2_ref,
                  rw1_ref, rb_ref, h_ref, hn_ref, w4_ref, idx_ref, sel_ref):
    ya = jnp.dot(oa_ref[rows, :], wpa_ref[...], preferred_element_type=F32)
    yb = jnp.dot(ob_ref[rows, :], wpb_ref[...], preferred_element_type=F32)
    merged = sga_ref[rows, :].astype(F32) * ya + sgb_ref[rows, :].astype(F32) * yb
    h = x_ref[rows, :] + jnp.dot(merged.astype(BF16), wo_ref[...], preferred_element_type=F32)
    h_ref[rows, :] = h
    hn = _rms(h, g2_ref[...])
    hn_ref[rows, :] = _pack_bf16_pairs(hn)

    hn_hi = hn.astype(BF16)
    hn_lo = (hn - hn_hi.astype(F32)).astype(BF16)
    two = jnp.dot(hn_hi, rw2_ref[...], preferred_element_type=F32)
    logits = (two[:, :LANES] + two[:, LANES:] + jnp.dot(hn_lo, rw1_ref[...], preferred_element_type=F32)
              + rb_ref[...])
    lane = lax.broadcasted_iota(I32, logits.shape, 1)
    lane_f = lane.astype(F32)
    work = logits
    vals, idxs = [], []
    for _ in range(TOP_K):
        m = jnp.max(work, axis=-1, keepdims=True)
        idx = jnp.min(jnp.where(work == m, lane_f, float(LANES)), axis=-1, keepdims=True)
        vals.append(m)
        idxs.append(idx)
        work = jnp.where(lane_f == idx, -jnp.inf, work)
    es = [jnp.exp(v - vals[0]) for v in vals]
    denom = es[0]
    for e in es[1:]:
        denom = denom + e
    w4 = jnp.zeros(logits.shape, F32)
    i4 = jnp.zeros(logits.shape, F32)
    sel = jnp.zeros(logits.shape, F32)
    for k in range(TOP_K):
        w4 = jnp.where(lane == k, es[k] / denom, w4)
        i4 = jnp.where(lane == k, idxs[k], i4)
        sel = jnp.where(lane_f == idxs[k], 1.0, sel)
    w4_ref[rows, :] = w4
    idx_ref[rows, :] = i4.astype(I32)
    sel_ref[rows, :] = sel.astype(BF16)


def _mix_out(oa, ob, z, gate_col, x2, w_proj_a, w_proj_b, w_out, norm2_g, router_w, router_b):
    n_tok, d = x2.shape
    hv, gv = oa.shape[1], ob.shape[1]
    n_exp = router_w.shape[1]
    tm = _divisor_tile(n_tok, 2 * MIX_SUB_ROWS)
    rw = jnp.pad(router_w.astype(F32), ((0, 0), (0, LANES - n_exp)))
    rw_hi = rw.astype(BF16)
    rw_lo = (rw - rw_hi.astype(F32)).astype(BF16)
    rb = jnp.concatenate([router_b.astype(F32), jnp.full((LANES - n_exp,), NEG_BIG, F32)])[None, :]
    assert gate_col % d == 0
    gate_blk = gate_col // d

    def rows(width, off=0):
        return pl.BlockSpec((tm, width), lambda i, off=off: (i, off))

    def whole(r, c):
        return pl.BlockSpec((r, c), lambda i: (0, 0), pipeline_mode=pl.Buffered(1))

    return pl.pallas_call(
        _mix_out_kernel,
        grid=(n_tok // tm,),
        in_specs=[rows(hv), rows(gv), rows(d, gate_blk), rows(d, gate_blk + 1), rows(d),
                  whole(hv, d), whole(gv, d), whole(d, d), whole(1, d), whole(d, 2 * LANES), whole(d, LANES),
                  whole(1, LANES)],
        out_specs=[rows(d), rows(d // 2), rows(LANES), rows(LANES), rows(LANES)],
        out_shape=[jax.ShapeDtypeStruct((n_tok, d), F32),
                   jax.ShapeDtypeStruct((n_tok, d // 2), U32),
                   jax.ShapeDtypeStruct((n_tok, LANES), F32),
                   jax.ShapeDtypeStruct((n_tok, LANES), I32),
                   jax.ShapeDtypeStruct((n_tok, LANES), BF16)],
        compiler_params=_params(("arbitrary",)),
        name="mix_out",
    )(oa, ob, z, z, x2, w_proj_a.astype(BF16), w_proj_b.astype(BF16), w_out.astype(BF16),
      norm2_g.reshape(1, d).astype(F32), jnp.concatenate([rw_hi, rw_lo], axis=1), rw_hi, rb)


def _route_rank_kernel(row_block, sel_ref, idx_ref, tri_ref, dest_ref, cnt_ref, rank_ref, carry_ref, start_ref):
    p = pl.program_id(0)
    i = pl.program_id(1)
    tb = sel_ref.shape[0]
    rows = pl.ds(pl.multiple_of(i * tb, tb), tb)

    @pl.when((p == 0) & (i == 0))
    def _():
        carry_ref[...] = jnp.zeros_like(carry_ref)

    @pl.when(p == 0)
    def _():
        sel = sel_ref[...]
        before = jnp.dot(tri_ref[...], sel, preferred_element_type=F32)
        rank_ref[rows, :] = before + carry_ref[0:1, :]
        carry_ref[...] = carry_ref[...] + jnp.sum(sel.astype(F32), axis=0, keepdims=True)

    @pl.when((p == 1) & (i == 0))
    def _():
        counts = carry_ref[...]
        padded = jnp.ceil(counts * (1.0 / row_block)) * row_block
        lane = lax.broadcasted_iota(I32, padded.shape, 1)
        run = padded
        s = 1
        while s < LANES:
            run = run + jnp.where(lane >= s, pltpu.roll(run, s, 1), 0.0)
            s *= 2
        start_ref[...] = run - padded

    @pl.when(p == 1)
    def _():
        pos = rank_ref[rows, :] + start_ref[0:1, :]
        idx = idx_ref[...]
        lane = lax.broadcasted_iota(I32, pos.shape, 1)
        dest = jnp.zeros(pos.shape, F32)
        for k in range(TOP_K):
            hit = lane == idx[:, k:k + 1]
            val = jnp.sum(jnp.where(hit, pos, 0.0), axis=-1, keepdims=True)
            dest = jnp.where(lane == k, val, dest)
        dest_ref[...] = dest.astype(I32)
        cnt_ref[...] = carry_ref[...]


def _route_rank(sel, idx4, row_block):
    n_tok = sel.shape[0]
    tb = _divisor_tile(n_tok, 256)
    tri = (lax.broadcasted_iota(I32, (tb, tb), 0) > lax.broadcasted_iota(I32, (tb, tb), 1)).astype(BF16)
    return pl.pallas_call(
        functools.partial(_route_rank_kernel, row_block),
        grid=(2, n_tok // tb),
        in_specs=[pl.BlockSpec((tb, LANES), lambda p, i: (i, 0)),
                  pl.BlockSpec((tb, LANES), lambda p, i: (i, 0)),
                  pl.BlockSpec((tb, tb), lambda p, i: (0, 0))],
        out_specs=[pl.BlockSpec((tb, LANES), lambda p, i: (i * p, 0)),
                   pl.BlockSpec((SUBLANES, LANES), lambda p, i: (0, 0))],
        out_shape=[jax.ShapeDtypeStruct((n_tok, LANES), I32),
                   jax.ShapeDtypeStruct((SUBLANES, LANES), F32)],
        scratch_shapes=[pltpu.VMEM((n_tok, LANES), F32), pltpu.VMEM((SUBLANES, LANES), F32),
                        pltpu.VMEM((SUBLANES, LANES), F32)],
        compiler_params=_params(("arbitrary", "arbitrary")),
        name="route_rank",
    )(sel, idx4, tri)


def _dispatch_kernel(bv_ref, dest_ref, hn_ref, xs_ref, zero_ref, sem):
    tt = hn_ref.shape[0]
    row_block = zero_ref.shape[0]
    n_blocks = bv_ref.shape[0]

    @pl.when(pl.program_id(0) == 0)
    def _():
        zero_ref[...] = jnp.zeros_like(zero_ref)

        def zero_copy(blk):
            return pltpu.make_async_copy(zero_ref, xs_ref.at[pl.ds(blk * row_block, row_block), :], sem)

        for blk in range(n_blocks):
            pl.when(bv_ref[blk] < row_block)(zero_copy(blk).start)
        for blk in range(n_blocks):
            pl.when(bv_ref[blk] < row_block)(zero_copy(blk).wait)

    for t in range(tt):
        for k in range(TOP_K):
            pltpu.make_async_copy(hn_ref.at[pl.ds(t, 1), :], xs_ref.at[pl.ds(dest_ref[TOP_K * t + k], 1), :],
                                  sem).start(priority=(TOP_K * t + k) % 2)
    for k in range(TOP_K):
        pltpu.make_async_copy(hn_ref, xs_ref.at[pl.ds(0, tt), :], sem).wait()


def _dispatch(block_valid, dest_flat, hn_packed, row_block):
    n_tok, half = hn_packed.shape
    tt = _divisor_tile(n_tok, 512)
    n_rows = block_valid.shape[0] * row_block
    grid_spec = pltpu.PrefetchScalarGridSpec(
        num_scalar_prefetch=1,
        grid=(n_tok // tt,),
        in_specs=[pl.BlockSpec((tt * TOP_K,), lambda i, bv: (i,), memory_space=pltpu.SMEM),
                  pl.BlockSpec((tt, half), lambda i, bv: (i, 0))],
        out_specs=pl.BlockSpec(memory_space=pl.ANY),
        scratch_shapes=[pltpu.VMEM((row_block, half), U32), pltpu.SemaphoreType.DMA(())],
    )
    return pl.pallas_call(
        _dispatch_kernel,
        grid_spec=grid_spec,
        out_shape=jax.ShapeDtypeStruct((n_rows, half), U32),
        compiler_params=_params(("arbitrary",)),
        name="dispatch",
    )(block_valid, dest_flat, hn_packed)


GROUP = 2 * LANES


def _regroup_kernel(w_ref, p_ref, o_ref):
    p = p_ref[...]
    for g in range(w_ref.shape[1] // GROUP):
        cols = slice(GROUP * g, GROUP * (g + 1))
        o_ref[:, cols] = jnp.dot(w_ref[:, cols].astype(BF16), p, preferred_element_type=F32).astype(BF16)


FFN_FF_TILE = 512


def _regroup_slice_rows(w_shape, n_items):
    n_exp, d, _ = w_shape
    rows = n_exp * d // n_items
    ok = n_items >= 2 and rows * n_items == n_exp * d and rows % (2 * SUBLANES) == 0 and d % rows == 0
    return rows if ok else None


def _regroup_permutation():
    src = lax.broadcasted_iota(I32, (GROUP, GROUP), 0)
    dst = lax.broadcasted_iota(I32, (GROUP, GROUP), 1)
    return (dst == (src // 2) + LANES * (src % 2)).astype(BF16)


def _regroup_gate_up(w_gate_up):
    n_exp, d, two_f = w_gate_up.shape
    tr = _divisor_tile(d, 2048)
    tc = 2 * _divisor_tile(two_f // 2, FFN_FF_TILE)
    perm = _regroup_permutation()
    return pl.pallas_call(
        _regroup_kernel,
        grid=(n_exp, d // tr, two_f // tc),
        in_specs=[pl.BlockSpec((None, tr, tc), lambda e, i, j: (e, i, j)),
                  pl.BlockSpec((GROUP, GROUP), lambda e, i, j: (0, 0))],
        out_specs=pl.BlockSpec((None, None, tr, tc), lambda e, i, j: (e, j, i, 0)),
        out_shape=jax.ShapeDtypeStruct((n_exp, two_f // tc, d, tc), BF16),
        compiler_params=_params(("arbitrary", "arbitrary", "arbitrary")),
        name="regroup_gate_up",
    )(w_gate_up, perm)


def _regroup_bias(b_gate_up):
    n_exp, two_f = b_gate_up.shape
    b = b_gate_up.reshape(n_exp, two_f // GROUP, LANES, 2)
    return jnp.swapaxes(b, 2, 3).reshape(n_exp, 1, two_f).astype(F32)


FFN_ROW_BLOCK = 1024
FFN_SUB_ROWS = 256


def _expert_ffn_kernel(n_f, be_ref, bv_ref, xs_ref, wgu_ref, wd_ref, bgu_ref, bd_ref,
                       out_ref, x_ref, acc_ref, wdb_ref):
    del be_ref
    b = pl.program_id(0)
    f = pl.program_id(1)
    n_rows = xs_ref.shape[0]
    sub = min(n_rows, FFN_SUB_ROWS)
    valid = bv_ref[b]
    n_sub = (valid + sub - 1) // sub
    full = n_sub == n_rows // sub
    partial = (n_sub > 0) & jnp.logical_not(full)
    first = f == 0
    last = f == n_f - 1

    def unpack():
        hi, lo = _unpack_bf16_pairs(xs_ref[...])
        return jnp.concatenate([hi.astype(BF16), lo.astype(BF16)], axis=1)

    def down(x, wd):
        gu = jnp.dot(x, wgu_ref[...], preferred_element_type=F32) + bgu_ref[...]
        acts = []
        for g in range(gu.shape[1] // GROUP):
            gate = jnp.minimum(gu[:, GROUP * g:GROUP * g + LANES], SWIGLU_LIMIT)
            up = jnp.clip(gu[:, GROUP * g + LANES:GROUP * (g + 1)], -SWIGLU_LIMIT, SWIGLU_LIMIT)
            acts.append(((up + 1.0) * (gate * jax.nn.sigmoid(SWIGLU_ALPHA * gate))).astype(BF16))
        return jnp.dot(jnp.concatenate(acts, axis=1), wd, preferred_element_type=F32)

    def full_step(is_first, is_last):
        def run():
            wd = wd_ref[...].astype(BF16)
            if is_first:
                x = unpack()
                x_ref[...] = x
                y = down(x, wd)
            else:
                y = acc_ref[...] + down(x_ref[...], wd)
            if is_last:
                out_ref[...] = _pack_bf16_pairs(y + bd_ref[...])
            else:
                acc_ref[...] = y
        return run

    if n_f == 1:
        pl.when(full)(full_step(True, True))
    else:
        pl.when(full & first)(full_step(True, False))
        pl.when(full & last)(full_step(False, True))
        if n_f > 2:
            pl.when(full & jnp.logical_not(first | last))(full_step(False, False))

    @pl.when(partial & first)
    def _():
        x_ref[...] = unpack()
        acc_ref[...] = jnp.zeros_like(acc_ref)

    @pl.when(partial)
    def _():
        wdb_ref[...] = wd_ref[...].astype(BF16)

        def body(s, carry):
            rows = pl.ds(pl.multiple_of(s * sub, sub), sub)
            acc_ref[rows, :] += down(x_ref[rows, :], wdb_ref[...])
            return carry
        lax.fori_loop(0, n_sub, body, 0)

    @pl.when(partial & last)
    def _():
        out_ref[...] = _pack_bf16_pairs(acc_ref[...] + bd_ref[...])

    @pl.when((valid == 0) & last)
    def _():
        out_ref[...] = jnp.zeros_like(out_ref)


def _expert_ffn(block_e, block_valid, xs, wgu, wd, bgu, bd, row_block):
    n_rows, half = xs.shape
    d = 2 * half
    d_ff = wd.shape[1]
    tf = wgu.shape[3] // 2
    n_blocks = n_rows // row_block
    grid_spec = pltpu.PrefetchScalarGridSpec(
        num_scalar_prefetch=2,
        grid=(n_blocks, d_ff // tf),
        in_specs=[
            pl.BlockSpec((row_block, half), lambda b, f, be, bv: (b, 0)),
            pl.BlockSpec((None, None, d, 2 * tf), lambda b, f, be, bv: (be[b], f, 0, 0)),
            pl.BlockSpec((None, tf, d), lambda b, f, be, bv: (be[b], f, 0)),
            pl.BlockSpec((None, 1, 2 * tf), lambda b, f, be, bv: (be[b], 0, f)),
            pl.BlockSpec((None, 1, d), lambda b, f, be, bv: (be[b], 0, 0)),
        ],
        out_specs=pl.BlockSpec((row_block, half), lambda b, f, be, bv: (b, 0)),
        scratch_shapes=[pltpu.VMEM((row_block, d), BF16), pltpu.VMEM((row_block, d), F32),
                        pltpu.VMEM((tf, d), BF16)],
    )
    return pl.pallas_call(
        functools.partial(_expert_ffn_kernel, d_ff // tf),
        grid_spec=grid_spec,
        out_shape=jax.ShapeDtypeStruct((n_rows, half), U32),
        compiler_params=_params(("arbitrary", "arbitrary")),
        name="expert_ffn",
    )(block_e, block_valid, xs, wgu, wd, bgu, bd)


def _combine_kernel(cur_ref, nxt_ref, h_ref, w4_ref, fg_ref, rows_hbm_ref, o_ref, buf_ref, sem):
    i = pl.program_id(0)
    n = pl.num_programs(0)
    tt = h_ref.shape[0]
    half = buf_ref.shape[3]

    def issue(dest_ref, s):
        for t in range(tt):
            for k in range(TOP_K):
                pltpu.make_async_copy(rows_hbm_ref.at[pl.ds(dest_ref[TOP_K * t + k], 1), :],
                                      buf_ref.at[s, k, pl.ds(t, 1), :],
                                      sem.at[s]).start(priority=(TOP_K * t + k) % 2)

    def drain(s):
        for k in range(TOP_K):
            pltpu.make_async_copy(rows_hbm_ref.at[pl.ds(0, tt), :], buf_ref.at[s, k], sem.at[s]).wait()

    @pl.when(i == 0)
    def _():
        issue(cur_ref, 0)

    def step(slot):
        drain(slot)
        issue(nxt_ref, 1 - slot)
        w4 = w4_ref[...]
        y_hi = jnp.zeros((tt, half), F32)
        y_lo = jnp.zeros((tt, half), F32)
        for k in range(TOP_K):
            hi, lo = _unpack_bf16_pairs(buf_ref[slot, k])
            wk = w4[:, k:k + 1]
            y_hi = y_hi + wk * hi
            y_lo = y_lo + wk * lo
        h = h_ref[...] + jnp.concatenate([y_hi, y_lo], axis=1)
        o_ref[...] = _rms(h, fg_ref[...])

        @pl.when(i == n - 1)
        def _():
            drain(1 - slot)

    for slot in range(2):
        pl.when(i % 2 == slot)(functools.partial(step, slot))


def _combine(dest_flat, h, w4, final_g, rows_packed):
    n_tok, d = h.shape
    half = d // 2
    tt = _divisor_tile(n_tok, 256)
    n_steps = n_tok // tt
    return pl.pallas_call(
        _combine_kernel,
        grid=(n_steps,),
        in_specs=[pl.BlockSpec((tt * TOP_K,), lambda i: (i,), memory_space=pltpu.SMEM),
                  pl.BlockSpec((tt * TOP_K,), lambda i: (jnp.minimum(i + 1, n_steps - 1),),
                               memory_space=pltpu.SMEM),
                  pl.BlockSpec((tt, d), lambda i: (i, 0)),
                  pl.BlockSpec((tt, LANES), lambda i: (i, 0)),
                  pl.BlockSpec((1, d), lambda i: (0, 0)),
                  pl.BlockSpec(memory_space=pl.ANY)],
        out_specs=pl.BlockSpec((tt, d), lambda i: (i, 0)),
        out_shape=jax.ShapeDtypeStruct((n_tok, d), F32),
        scratch_shapes=[pltpu.VMEM((2, TOP_K, tt, half), U32), pltpu.SemaphoreType.DMA((2,))],
        compiler_params=_params(("arbitrary",)),
        name="combine",
    )(dest_flat, dest_flat, h, w4, final_g.reshape(1, d).astype(F32), rows_packed)


def _layer(h2, batch, seq, lb, norm1_g, w_in, hg_norm_g, gate_w2, gate_b, gla_norm_g, w_proj_a, w_proj_b,
           w_out, norm2_g, router_w, router_b, w_gate_up, b_gate_up, w_down, b_down, out_norm_g):
    n_tok, d = h2.shape
    hf = lb.shape[1]
    hv = w_proj_a.shape[0]
    gk = gate_w2.shape[2]
    gv = w_proj_b.shape[0]
    rank = gate_w2.shape[1]
    hg_heads, gla_heads = hf // HEAD_DK, gk // HEAD_DK
    n_exp = router_w.shape[1]

    z, gf, gb, gg = _in_proj(h2, norm1_g, w_in, lb, gate_w2, gate_b, (hf, hv, gk, gv, rank))
    a_cols = (0, hf, 2 * hf, 3 * hf, 3 * hf + hv)
    b0 = 3 * hf + 2 * hv
    b_cols = (b0, b0 + gk, b0 + gk, b0 + 2 * gk, b0 + 2 * gk + gv)
    hosted = _regroup_slice_rows(w_gate_up.shape, _recurrence_steps(batch, seq, hg_heads)[1]) is not None
    oa = _recurrence(z, gf, gb, hg_norm_g, batch, seq, hg_heads, hv // hg_heads, a_cols,
                     name="hgrn_recurrence", regroup=w_gate_up if hosted else None)
    oa, wgu = oa if hosted else (oa, _regroup_gate_up(w_gate_up))
    ob = _recurrence(z, gg, gg, gla_norm_g, batch, seq, gla_heads, gv // gla_heads, b_cols,
                     gb_off=gla_heads, name="gla_recurrence")
    h, hn_packed, w4, idx4, sel = _mix_out(oa, ob, z, b0 + 2 * gk + 2 * gv, h2, w_proj_a, w_proj_b, w_out,
                                           norm2_g, router_w, router_b)

    n_pairs = n_tok * TOP_K
    row_block = _divisor_tile(n_pairs, FFN_ROW_BLOCK)
    n_blocks = n_pairs // row_block + n_exp
    dest4, counts = _route_rank(sel, idx4, row_block)
    dest_flat = dest4[:, :TOP_K].reshape(-1)
    cnt = counts[0, :n_exp].astype(I32)
    blocks_e = (cnt + row_block - 1) // row_block
    end_blk = jnp.cumsum(blocks_e)
    start_blk = end_blk - blocks_e
    bidx = jnp.arange(n_blocks, dtype=I32)[:, None]
    block_e = jnp.minimum(jnp.sum((end_blk[None, :] <= bidx).astype(I32), axis=1), n_exp - 1)
    owned = (bidx >= start_blk[None, :]) & (bidx < end_blk[None, :])
    rows_left = jnp.clip(cnt[None, :] - (bidx - start_blk[None, :]) * row_block, 0, row_block)
    block_valid = jnp.sum(jnp.where(owned, rows_left, 0), axis=1).astype(I32)

    xs = _dispatch(block_valid, dest_flat, hn_packed, row_block)
    rows = _expert_ffn(block_e, block_valid, xs, wgu, w_down.astype(F32),
                       _regroup_bias(b_gate_up), b_down[:, None, :].astype(F32), row_block)
    return _combine(dest_flat, h, w4, out_norm_g, rows)


def kernel(x, norm1_g, w_in, hg_lb_logits, hg_norm_g, gla_gate_w2, gla_gate_b, gla_norm_g, w_proj_a, w_proj_b,
           w_out, norm2_g, router_w, router_b, w_gate_up, b_gate_up, w_down, b_down, final_norm_g):
    batch, seq, d = x.shape
    depth = w_in.shape[0]
    assert depth == 1, "the final RMSNorm is fused into the last layer's combine kernel"
    lb_all = jnp.cumsum(jax.nn.softmax(hg_lb_logits.astype(F32), axis=1), axis=1)
    h2 = x.reshape(batch * seq, d)
    out = _layer(h2, batch, seq, lb_all[:, 0], norm1_g[0], w_in[0], hg_norm_g[0], gla_gate_w2[0], gla_gate_b[0],
                 gla_norm_g[0], w_proj_a[0], w_proj_b[0], w_out[0], norm2_g[0], router_w[0], router_b[0],
                 w_gate_up[0], b_gate_up[0], w_down[0], b_down[0], final_norm_g)
    return out.reshape(batch, seq, d)
```

```python
import functools

import jax
import jax.numpy as jnp
from jax import lax
from jax.experimental import pallas as pl
from jax.experimental.pallas import tpu as pltpu

F32 = jnp.float32
BF16 = jnp.bfloat16
U32 = jnp.uint32
I32 = jnp.int32

LANES = 128
SUBLANES = 8
VMEM_LIMIT_BYTES = 56 * 1024 * 1024

HEAD_DK = 128
GLA_DV = 256
HG_DV = 128
GATE_NORM = 16.0
TOP_K = 4
SWIGLU_LIMIT = 7.0
SWIGLU_ALPHA = 1.702
RMS_EPS = 1e-5
CHUNK = 64
NEG_BIG = -1e30
LOG2_E = 1.4426950408889634


def _divisor_tile(n, pref):
    t = min(n, pref)
    while n % t:
        t //= 2
    return t


def _params(sem, vmem=VMEM_LIMIT_BYTES):
    return pltpu.CompilerParams(dimension_semantics=sem, vmem_limit_bytes=vmem)


def _pack_bf16_pairs(a):
    n = a.shape[-1] // 2
    hi = lax.bitcast_convert_type(a[:, :n].astype(BF16).astype(F32), U32)
    lo = lax.bitcast_convert_type(a[:, n:].astype(BF16).astype(F32), U32)
    return hi | (lo >> 16)


def _unpack_bf16_pairs(p):
    hi = lax.bitcast_convert_type(p & jnp.uint32(0xFFFF0000), F32)
    lo = lax.bitcast_convert_type(p << 16, F32)
    return hi, lo


def _rms(x, g):
    ms = jnp.mean(x * x, axis=-1, keepdims=True)
    return x * lax.rsqrt(ms + RMS_EPS) * g


def _log_sigmoid(x):
    return jnp.minimum(x, 0.0) - jnp.log1p(jnp.exp(-jnp.abs(x)))


def _in_proj_kernel(bounds, x_ref, g1_ref, w_ref, lb_ref, w2_ref, b2_ref,
                    z_ref, gf_ref, gb_ref, gg_ref, xn_ref):
    j = pl.program_id(1)

    @pl.when(j == 0)
    def _():
        xn_ref[...] = _rms(x_ref[...], g1_ref[...]).astype(BF16)

    scale = HEAD_DK ** -0.5

    def project():
        return jnp.dot(xn_ref[...], w_ref[...], preferred_element_type=F32)

    def silu(z):
        return z * jax.nn.sigmoid(z)

    def store(fn):
        def epilogue():
            z_ref[...] = fn(project()).astype(BF16)
        return epilogue

    def forget(g_ref):
        def epilogue():
            lb = lb_ref[0:1, :]
            f = lb + (1.0 - lb) * jax.nn.sigmoid(project())
            z_ref[...] = (1.0 - f).astype(BF16)
            g_ref[...] = jnp.log2(f)
        return epilogue

    def low_rank_gate():
        lr = project()[:, :LANES].astype(BF16)
        logits = jnp.dot(lr, w2_ref[...], preferred_element_type=F32) + b2_ref[...]
        gg_ref[...] = _log_sigmoid(logits) * (LOG2_E / GATE_NORM)

    epilogues = [
        store(lambda z: silu(z) * scale),
        forget(gf_ref),
        forget(gb_ref),
        store(lambda z: z),
        store(silu),
        store(lambda z: z * scale),
        store(lambda z: z),
        store(lambda z: z),
        store(silu),
        store(jax.nn.sigmoid),
        store(jax.nn.sigmoid),
        low_rank_gate,
    ]
    for k, epilogue in enumerate(epilogues):
        pl.when((j >= bounds[k]) & (j < bounds[k + 1]))(epilogue)


def _in_proj(x2, norm1_g, w_in, lb, gate_w2, gate_b, dims):
    n_tok, d = x2.shape
    hf, hv, gk, gv, rank = dims
    widths = [hf, hf, hf, hv, hv, gk, gk, gv, gv]
    off_lr = sum(widths)
    n_main = off_lr + 2 * d
    tn = _divisor_tile(gk, 512)
    tm = _divisor_tile(n_tok, 1024)
    widths_all = widths + [d, d, tn]
    w_lr = jnp.pad(w_in[:, off_lr:off_lr + 2 * rank], ((0, 0), (0, tn - 2 * rank)))
    w_cat = jnp.concatenate([w_in[:, :off_lr], w_in[:, off_lr + 2 * rank:], w_lr], axis=1).astype(BF16)
    n_col = w_cat.shape[1]
    w_tiles = w_cat.reshape(d, n_col // tn, tn).transpose(1, 0, 2)
    lb_tab = jnp.zeros((SUBLANES, n_col), F32)
    lb_tab = lb_tab.at[0, hf:2 * hf].set(lb[0]).at[0, 2 * hf:3 * hf].set(lb[1])
    w2 = jnp.zeros((LANES, 2 * gk), F32)
    w2 = w2.at[:rank, :gk].set(gate_w2[0]).at[rank:2 * rank, gk:].set(gate_w2[1]).astype(BF16)
    b2 = jnp.concatenate([gate_b[0], gate_b[1]])[None, :].astype(F32)

    bounds = [0]
    for w in widths_all:
        bounds.append(bounds[-1] + w // tn)
    n_j = bounds[-1]

    def seg_spec(k):
        s, n = bounds[k], bounds[k + 1] - bounds[k]
        return pl.BlockSpec((tm, tn), lambda i, j, s=s, n=n: (i, jnp.clip(j - s, 0, n - 1)))

    return pl.pallas_call(
        functools.partial(_in_proj_kernel, tuple(bounds)),
        grid=(n_tok // tm, n_j),
        in_specs=[
            pl.BlockSpec((tm, d), lambda i, j: (i, 0)),
            pl.BlockSpec((1, d), lambda i, j: (0, 0)),
            pl.BlockSpec((None, d, tn), lambda i, j: (j, 0, 0)),
            pl.BlockSpec((SUBLANES, tn), lambda i, j: (0, j)),
            pl.BlockSpec((LANES, 2 * gk), lambda i, j: (0, 0)),
            pl.BlockSpec((1, 2 * gk), lambda i, j: (0, 0)),
        ],
        out_specs=[pl.BlockSpec((tm, tn), lambda i, j: (i, jnp.minimum(j, n_main // tn - 1))),
                   seg_spec(1), seg_spec(2),
                   pl.BlockSpec((tm, 2 * gk), lambda i, j: (i, 0))],
        out_shape=[jax.ShapeDtypeStruct((n_tok, n_main), BF16),
                   jax.ShapeDtypeStruct((n_tok, hf), F32),
                   jax.ShapeDtypeStruct((n_tok, hf), F32),
                   jax.ShapeDtypeStruct((n_tok, 2 * gk), F32)],
        scratch_shapes=[pltpu.VMEM((tm, d), BF16)],
        compiler_params=_params(("arbitrary", "arbitrary")),
        name="in_proj",
    )(x2, norm1_g.reshape(1, d), w_tiles, lb_tab, w2, b2)


_NT_DIMS = (((1,), (1,)), ((), ()))
_TN_DIMS = (((0,), (0,)), ((), ()))


def _chunk_prepare(q, k, g, rev):
    nt = CHUNK // SUBLANES
    row = lax.broadcasted_iota(I32, (SUBLANES, LANES), 0)
    lrow = (SUBLANES - 1 - row) if rev else row

    def phys(jl):
        return nt - 1 - jl if rev else jl

    def prow(x, rl):
        r = SUBLANES - 1 - rl if rev else rl
        return x[r:r + 1, :]

    def tile(x, jl):
        p = phys(jl)
        return x[SUBLANES * p:SUBLANES * (p + 1), :]

    def scan(x):
        for s in (1, 2, 4):
            if rev:
                x = x + jnp.where(row < SUBLANES - s, pltpu.roll(x, SUBLANES - s, 0), 0.0)
            else:
                x = x + jnp.where(row >= s, pltpu.roll(x, s, 0), 0.0)
        return x

    gt = [tile(g, jl) for jl in range(nt)]
    qt = [tile(q, jl) for jl in range(nt)]
    kt = [tile(k, jl) for jl in range(nt)]
    ct = [scan(x) for x in gt]
    tot = [prow(c, SUBLANES - 1) for c in ct]
    suf = [t - c for t, c in zip(tot, ct)]

    def assemble(tiles):
        out = [None] * nt
        for jl, x in enumerate(tiles):
            out[phys(jl)] = jnp.zeros((SUBLANES, LANES), F32) if x is None else x
        return jnp.concatenate(out, axis=0).astype(BF16)

    def plus(a, b):
        return a if b is None else a + b

    levels = []

    for nb in (8, 4, 2):
        half = nb // 2
        qs, ks = [None] * nt, [None] * nt
        for base in range(0, nt, nb):
            acc = None
            for jl in range(base + half, base + nb):
                qs[jl] = qt[jl] * jnp.exp2(plus(ct[jl], acc))
                acc = plus(tot[jl], acc)
            acc = None
            for jl in range(base + half - 1, base - 1, -1):
                ks[jl] = kt[jl] * jnp.exp2(plus(suf[jl], acc))
                acc = plus(tot[jl], acc)
        levels.append((SUBLANES * nb, assemble(qs), assemble(ks)))

    for size in (8, 4):
        half = size // 2
        is_q = (lrow & (size - 1)) >= half
        qs, ks = [], []
        for jl in range(nt):
            if size == 8:
                ref = prow(ct[jl], half - 1)
            else:
                ref = jnp.where(lrow >= size, prow(ct[jl], size + half - 1), prow(ct[jl], half - 1))
            d = ct[jl] - ref
            e = jnp.exp2(jnp.where(is_q, d, -d))
            qs.append(jnp.where(is_q, qt[jl] * e, 0.0))
            ks.append(jnp.where(is_q, 0.0, kt[jl] * e))
        levels.append((size, assemble(qs), assemble(ks)))

    odd = (lrow & 1) == 1
    qs = [jnp.where(odd, qt[jl] * jnp.exp2(gt[jl]), 0.0) for jl in range(nt)]
    ks = [jnp.where(odd, 0.0, kt[jl]) for jl in range(nt)]
    levels.append((2, assemble(qs), assemble(ks)))
    levels.append((1, q.astype(BF16), k.astype(BF16)))

    ti = lax.broadcasted_iota(I32, (CHUNK, CHUNK), 0)
    si = lax.broadcasted_iota(I32, (CHUNK, CHUNK), 1)
    blk = ti ^ si
    scores = None
    for size, qm, km in levels:
        s_m = lax.dot_general(qm, km, _NT_DIMS, preferred_element_type=F32)
        scores = s_m if scores is None else jnp.where(blk < size, s_m, scores)

    q_in, k_out = [None] * nt, [None] * nt
    acc = None
    for jl in range(nt):
        q_in[jl] = qt[jl] * jnp.exp2(plus(ct[jl], acc))
        acc = plus(tot[jl], acc)
    total = acc
    acc = None
    for jl in range(nt - 1, -1, -1):
        k_out[jl] = kt[jl] * jnp.exp2(plus(suf[jl], acc))
        acc = plus(tot[jl], acc)
    return scores.astype(BF16), assemble(q_in), assemble(k_out), jnp.exp2(total)


def _chunk_apply(scores, q_in, k_out, decay, v, st_ref):
    st = st_ref[...]
    out = (jnp.dot(scores, v, preferred_element_type=F32)
           + lax.dot_general(q_in, st.astype(BF16), _NT_DIMS, preferred_element_type=F32))
    st_ref[...] = st * decay + lax.dot_general(v, k_out, _TN_DIMS, preferred_element_type=F32)
    return out


REGROUP_LOOKAHEAD = 3


def _regroup_side_task(side, perm_ref, w_src, w_dst, inbuf, outbuf, in_sem, out_sem):
    rows, n_items = side
    d = w_src.shape[1]
    n_g = w_src.shape[2] // GROUP
    n_slab, slab_w = w_dst.shape[1], w_dst.shape[3]
    per_slab = slab_w // GROUP
    n_in = REGROUP_LOOKAHEAD + 1
    last = n_items - 1

    def coords(it):
        row = it * rows
        return row // d, pl.multiple_of(row % d, rows)

    def in_copies(it, slot):
        e, r = coords(it)
        return [pltpu.make_async_copy(w_src.at[e, pl.ds(r, rows), :], inbuf.at[slot], in_sem.at[slot])]

    def out_copies(it, slot):
        e, r = coords(it)
        return [pltpu.make_async_copy(outbuf.at[slot, j], w_dst.at[e, j, pl.ds(r, rows), :], out_sem.at[slot])
                for j in range(n_slab)]

    def prologue():
        outbuf[...] = jnp.zeros_like(outbuf)
        for it in range(min(REGROUP_LOOKAHEAD, n_items)):
            for cp in in_copies(it, it % n_in):
                cp.start()
        for slot in range(2):
            for cp in out_copies(min(slot, last), slot):
                cp.start()

    def item_wait(it):
        for cp in in_copies(it, it % n_in):
            cp.wait()
        for cp in out_copies(it, it % 2):
            cp.wait()

    def item_compute(it):
        x = inbuf[it % n_in]
        stacked = jnp.concatenate([x[:, GROUP * g:GROUP * (g + 1)] for g in range(n_g)], axis=0)
        res = jnp.dot(stacked.astype(BF16), perm_ref[...], preferred_element_type=F32).astype(BF16)
        for g in range(n_g):
            outbuf[it % 2, g // per_slab, :, pl.ds(GROUP * (g % per_slab), GROUP)] = res[g * rows:(g + 1) * rows, :]

    def item_start(it):
        ahead = jnp.minimum(it + REGROUP_LOOKAHEAD, last)
        for cp in in_copies(ahead, (it + REGROUP_LOOKAHEAD) % n_in):
            cp.start()
        for cp in out_copies(it, it % 2):
            cp.start()

    def epilogue():
        for slot in range(2):
            for cp in out_copies(last, slot):
                cp.wait()
        for k in range(1, REGROUP_LOOKAHEAD + 1):
            for cp in in_copies(last, (last + k) % n_in):
                cp.wait()

    return prologue, (item_wait, item_compute, item_start), epilogue


def _recurrence_kernel(n_chunks, dv, side, *refs):
    q_ref, kf_ref, kb_ref, v_ref, gf_ref, gb_ref, og_ref, ng_ref = refs[:8]
    refs = refs[8:]
    if side is not None:
        perm_ref, w_src, o_ref, w_dst, of_ref, ob_ref, st_ref, sc_ref, qin_ref, kout_ref, dec_ref = refs[:11]
        side_prologue, side_item, side_epilogue = _regroup_side_task(side, perm_ref, w_src, w_dst, *refs[11:])
        step = pl.program_id(0) * pl.num_programs(1) + pl.program_id(1)
        n_steps = pl.num_programs(0) * pl.num_programs(1)
        pl.when(step == 0)(side_prologue)
    else:
        o_ref, of_ref, ob_ref, st_ref, sc_ref, qin_ref, kout_ref, dec_ref = refs
    n_heads = q_ref.shape[1] // HEAD_DK
    lanes = [(rev, k_ref, g_ref, out_ref, hd)
             for rev, k_ref, g_ref, out_ref in ((False, kf_ref, gf_ref, of_ref), (True, kb_ref, gb_ref, ob_ref))
             for hd in range(n_heads)]
    st_ref[...] = jnp.zeros_like(st_ref)

    def rows_of(c, rev):
        cc = (n_chunks - 1 - c) if rev else c
        return pl.ds(pl.multiple_of(cc * CHUNK, CHUNK), CHUNK)

    def kcols(hd):
        return slice(HEAD_DK * hd, HEAD_DK * (hd + 1))

    def vcols(hd):
        return slice(dv * hd, dv * (hd + 1))

    def prepare(c):
        res = []
        for rev, k_ref, g_ref, _, hd in lanes:
            rows = rows_of(c, rev)
            res.append(_chunk_prepare(q_ref[rows, kcols(hd)].astype(F32), k_ref[rows, kcols(hd)].astype(F32),
                                      g_ref[rows, kcols(hd)], rev))
        return res

    def stash(res):
        for d, (scores, q_in, k_out, decay) in enumerate(res):
            sc_ref[d] = scores
            qin_ref[d] = q_in
            kout_ref[d] = k_out
            dec_ref[d] = jnp.broadcast_to(decay, (SUBLANES, LANES))

    def unstash():
        return [(sc_ref[d], qin_ref[d], kout_ref[d], dec_ref[d][0:1, :]) for d in range(len(lanes))]

    def apply(c, staged):
        for d, ((rev, _, _, out_ref, hd), (scores, q_in, k_out, decay)) in enumerate(zip(lanes, staged)):
            rows = rows_of(c, rev)
            out_ref[rows, vcols(hd)] = _chunk_apply(scores, q_in, k_out, decay, v_ref[rows, vcols(hd)],
                                                    st_ref.at[d])

    stash(prepare(0))

    def body(c, carry):
        if side is not None:
            side_item[0](step * n_chunks + c)
        staged = unstash()
        nxt = prepare(c + 1)
        apply(c, staged)
        stash(nxt)
        if side is not None:
            side_item[1](step * n_chunks + c)
            side_item[2](step * n_chunks + c)
        return carry

    lax.fori_loop(0, n_chunks - 1, body, 0)
    if side is not None:
        side_item[0](step * n_chunks + n_chunks - 1)
        side_item[1](step * n_chunks + n_chunks - 1)
        side_item[2](step * n_chunks + n_chunks - 1)
    apply(n_chunks - 1, unstash())
    if side is not None:
        pl.when(step == n_steps - 1)(side_epilogue)
    for hd in range(n_heads):
        o = of_ref[:, vcols(hd)] + ob_ref[:, vcols(hd)]
        o_ref[:, vcols(hd)] = (_rms(o, ng_ref[...]) * og_ref[:, vcols(hd)].astype(F32)).astype(BF16)


RECURRENCE_HEADS_PER_STEP = 2


def _recurrence_steps(batch, seq, heads):
    hp = RECURRENCE_HEADS_PER_STEP if heads % RECURRENCE_HEADS_PER_STEP == 0 else 1
    return hp, batch * (heads // hp) * (seq // CHUNK)


def _recurrence(z, gf, gb, norm_g, batch, seq, heads, dv, cols, gb_off=0, name="", regroup=None):
    n_tok = batch * seq
    n_chunks = seq // CHUNK
    q_col, kf_col, kb_col, v_col, og_col = cols
    hp, n_items = _recurrence_steps(batch, seq, heads)

    def spec(width, col=0):
        width = width * hp
        assert col % width == 0
        return pl.BlockSpec((seq, width), lambda b, h, off=col // width: (b, h + off))

    in_specs = [spec(HEAD_DK, q_col), spec(HEAD_DK, kf_col), spec(HEAD_DK, kb_col), spec(dv, v_col),
                spec(HEAD_DK), spec(HEAD_DK, gb_off * HEAD_DK), spec(dv, og_col),
                pl.BlockSpec((1, dv), lambda b, h: (0, 0))]
    operands = [z, z, z, z, gf, gb, z, norm_g.reshape(1, dv).astype(F32)]
    out_specs = [spec(dv)]
    out_shape = [jax.ShapeDtypeStruct((n_tok, heads * dv), BF16)]
    scratch = [pltpu.VMEM((seq, hp * dv), F32), pltpu.VMEM((seq, hp * dv), F32),
               pltpu.VMEM((2 * hp, dv, HEAD_DK), F32),
               pltpu.VMEM((2 * hp, CHUNK, CHUNK), BF16), pltpu.VMEM((2 * hp, CHUNK, HEAD_DK), BF16),
               pltpu.VMEM((2 * hp, CHUNK, HEAD_DK), BF16), pltpu.VMEM((2 * hp, SUBLANES, LANES), F32)]
    side = None
    if regroup is not None:
        n_exp, d, two_f = regroup.shape
        slab_w = 2 * _divisor_tile(two_f // 2, FFN_FF_TILE)
        side = (_regroup_slice_rows(regroup.shape, n_items), n_items)
        in_specs += [pl.BlockSpec((GROUP, GROUP), lambda b, h: (0, 0)), pl.BlockSpec(memory_space=pl.ANY)]
        operands += [_regroup_permutation(), regroup]
        out_specs.append(pl.BlockSpec(memory_space=pl.ANY))
        out_shape.append(jax.ShapeDtypeStruct((n_exp, two_f // slab_w, d, slab_w), BF16))
        scratch += [pltpu.VMEM((REGROUP_LOOKAHEAD + 1, side[0], two_f), F32),
                    pltpu.VMEM((2, two_f // slab_w, side[0], slab_w), BF16),
                    pltpu.SemaphoreType.DMA((REGROUP_LOOKAHEAD + 1,)), pltpu.SemaphoreType.DMA((2,))]

    res = pl.pallas_call(
        functools.partial(_recurrence_kernel, n_chunks, dv, side),
        grid=(batch, heads // hp),
        in_specs=in_specs,
        out_specs=out_specs,
        out_shape=out_shape,
        scratch_shapes=scratch,
        compiler_params=_params(("arbitrary", "arbitrary")),
        name=name,
    )(*operands)
    return res if regroup is not None else res[0]


MIX_SUB_ROWS = 256


def _mix_out_kernel(oa_ref, ob_ref, sga_ref, sgb_ref, x_ref, wpa_ref, wpb_ref, wo_ref, g2_ref, rw2_ref, rw1_ref,
                    rb_ref, h_ref, hn_ref, w4_ref, idx_ref, sel_ref):
    tm = x_ref.shape[0]
    sub = min(tm, MIX_SUB_ROWS)
    for r in range(tm // sub):
        _mix_out_rows(pl.ds(r * sub, sub), oa_ref, ob_ref, sga_ref, sgb_ref, x_ref, wpa_ref, wpb_ref, wo_ref,
                      g2_ref, rw2_ref, rw1_ref, rb_ref, h_ref, hn_ref, w4_ref, idx_ref, sel_ref)


def _mix_out_rows(rows, oa_ref, ob_ref, sga_ref, sgb_ref, x_ref, wpa_ref, wpb_ref, wo_ref, g2_ref, rw2_ref,
                  rw1_ref, rb_ref, h_ref, hn_ref, w4_ref, idx_ref, sel_ref):
    ya = jnp.dot(oa_ref[rows, :], wpa_ref[...], preferred_element_type=F32)
    yb = jnp.dot(ob_ref[rows, :], wpb_ref[...], preferred_element_type=F32)
    merged = sga_ref[rows, :].astype(F32) * ya + sgb_ref[rows, :].astype(F32) * yb
    h = x_ref[rows, :] + jnp.dot(merged.astype(BF16), wo_ref[...], preferred_element_type=F32)
    h_ref[rows, :] = h
    hn = _rms(h, g2_ref[...])
    hn_ref[rows, :] = _pack_bf16_pairs(hn)

    hn_hi = hn.astype(BF16)
    hn_lo = (hn - hn_hi.astype(F32)).astype(BF16)
    two = jnp.dot(hn_hi, rw2_ref[...], preferred_element_type=F32)
    logits = (two[:, :LANES] + two[:, LANES:] + jnp.dot(hn_lo, rw1_ref[...], preferred_element_type=F32)
              + rb_ref[...])
    lane = lax.broadcasted_iota(I32, logits.shape, 1)
    lane_f = lane.astype(F32)
    work = logits
    vals, idxs = [], []
    for _ in range(TOP_K):
        m = jnp.max(work, axis=-1, keepdims=True)
        idx = jnp.min(jnp.where(work == m, lane_f, float(LANES)), axis=-1, keepdims=True)
        vals.append(m)
        idxs.append(idx)
        work = jnp.where(lane_f == idx, -jnp.inf, work)
    es = [jnp.exp(v - vals[0]) for v in vals]
    denom = es[0]
    for e in es[1:]:
        denom = denom + e
    w4 = jnp.zeros(logits.shape, F32)
    i4 = jnp.zeros(logits.shape, F32)
    sel = jnp.zeros(logits.shape, F32)
    for k in range(TOP_K):
        w4 = jnp.where(lane == k, es[k] / denom, w4)
        i4 = jnp.where(lane == k, idxs[k], i4)
        sel = jnp.where(lane_f == idxs[k], 1.0, sel)
    w4_ref[rows, :] = w4
    idx_ref[rows, :] = i4.astype(I32)
    sel_ref[rows, :] = sel.astype(BF16)


def _mix_out(oa, ob, z, gate_col, x2, w_proj_a, w_proj_b, w_out, norm2_g, router_w, router_b):
    n_tok, d = x2.shape
    hv, gv = oa.shape[1], ob.shape[1]
    n_exp = router_w.shape[1]
    tm = _divisor_tile(n_tok, 2 * MIX_SUB_ROWS)
    rw = jnp.pad(router_w.astype(F32), ((0, 0), (0, LANES - n_exp)))
    rw_hi = rw.astype(BF16)
    rw_lo = (rw - rw_hi.astype(F32)).astype(BF16)
    rb = jnp.concatenate([router_b.astype(F32), jnp.full((LANES - n_exp,), NEG_BIG, F32)])[None, :]
    assert gate_col % d == 0
    gate_blk = gate_col // d

    def rows(width, off=0):
        return pl.BlockSpec((tm, width), lambda i, off=off: (i, off))

    def whole(r, c):
        return pl.BlockSpec((r, c), lambda i: (0, 0), pipeline_mode=pl.Buffered(1))

    return pl.pallas_call(
        _mix_out_kernel,
        grid=(n_tok // tm,),
        in_specs=[rows(hv), rows(gv), rows(d, gate_blk), rows(d, gate_blk + 1), rows(d),
                  whole(hv, d), whole(gv, d), whole(d, d), whole(1, d), whole(d, 2 * LANES), whole(d, LANES),
                  whole(1, LANES)],
        out_specs=[rows(d), rows(d // 2), rows(LANES), rows(LANES), rows(LANES)],
        out_shape=[jax.ShapeDtypeStruct((n_tok, d), F32),
                   jax.ShapeDtypeStruct((n_tok, d // 2), U32),
                   jax.ShapeDtypeStruct((n_tok, LANES), F32),
                   jax.ShapeDtypeStruct((n_tok, LANES), I32),
                   jax.ShapeDtypeStruct((n_tok, LANES), BF16)],
        compiler_params=_params(("arbitrary",)),
        name="mix_out",
    )(oa, ob, z, z, x2, w_proj_a.astype(BF16), w_proj_b.astype(BF16), w_out.astype(BF16),
      norm2_g.reshape(1, d).astype(F32), jnp.concatenate([rw_hi, rw_lo], axis=1), rw_hi, rb)


def _route_rank_kernel(row_block, sel_ref, idx_ref, tri_ref, dest_ref, cnt_ref, rank_ref, carry_ref, start_ref):
    p = pl.program_id(0)
    i = pl.program_id(1)
    tb = sel_ref.shape[0]
    rows = pl.ds(pl.multiple_of(i * tb, tb), tb)

    @pl.when((p == 0) & (i == 0))
    def _():
        carry_ref[...] = jnp.zeros_like(carry_ref)

    @pl.when(p == 0)
    def _():
        sel = sel_ref[...]
        before = jnp.dot(tri_ref[...], sel, preferred_element_type=F32)
        rank_ref[rows, :] = before + carry_ref[0:1, :]
        carry_ref[...] = carry_ref[...] + jnp.sum(sel.astype(F32), axis=0, keepdims=True)

    @pl.when((p == 1) & (i == 0))
    def _():
        counts = carry_ref[...]
        padded = jnp.ceil(counts * (1.0 / row_block)) * row_block
        lane = lax.broadcasted_iota(I32, padded.shape, 1)
        run = padded
        s = 1
        while s < LANES:
            run = run + jnp.where(lane >= s, pltpu.roll(run, s, 1), 0.0)
            s *= 2
        start_ref[...] = run - padded

    @pl.when(p == 1)
    def _():
        pos = rank_ref[rows, :] + start_ref[0:1, :]
        idx = idx_ref[...]
        lane = lax.broadcasted_iota(I32, pos.shape, 1)
        dest = jnp.zeros(pos.shape, F32)
        for k in range(TOP_K):
            hit = lane == idx[:, k:k + 1]
            val = jnp.sum(jnp.where(hit, pos, 0.0), axis=-1, keepdims=True)
            dest = jnp.where(lane == k, val, dest)
        dest_ref[...] = dest.astype(I32)
        cnt_ref[...] = carry_ref[...]


def _route_rank(sel, idx4, row_block):
    n_tok = sel.shape[0]
    tb = _divisor_tile(n_tok, 256)
    tri = (lax.broadcasted_iota(I32, (tb, tb), 0) > lax.broadcasted_iota(I32, (tb, tb), 1)).astype(BF16)
    return pl.pallas_call(
        functools.partial(_route_rank_kernel, row_block),
        grid=(2, n_tok // tb),
        in_specs=[pl.BlockSpec((tb, LANES), lambda p, i: (i, 0)),
                  pl.BlockSpec((tb, LANES), lambda p, i: (i, 0)),
                  pl.BlockSpec((tb, tb), lambda p, i: (0, 0))],
        out_specs=[pl.BlockSpec((tb, LANES), lambda p, i: (i * p, 0)),
                   pl.BlockSpec((SUBLANES, LANES), lambda p, i: (0, 0))],
        out_shape=[jax.ShapeDtypeStruct((n_tok, LANES), I32),
                   jax.ShapeDtypeStruct((SUBLANES, LANES), F32)],
        scratch_shapes=[pltpu.VMEM((n_tok, LANES), F32), pltpu.VMEM((SUBLANES, LANES), F32),
                        pltpu.VMEM((SUBLANES, LANES), F32)],
        compiler_params=_params(("arbitrary", "arbitrary")),
        name="route_rank",
    )(sel, idx4, tri)


def _dispatch_kernel(bv_ref, dest_ref, hn_ref, xs_ref, zero_ref, sem):
    tt = hn_ref.shape[0]
    row_block = zero_ref.shape[0]
    n_blocks = bv_ref.shape[0]

    @pl.when(pl.program_id(0) == 0)
    def _():
        zero_ref[...] = jnp.zeros_like(zero_ref)

        def zero_copy(blk):
            return pltpu.make_async_copy(zero_ref, xs_ref.at[pl.ds(blk * row_block, row_block), :], sem)

        for blk in range(n_blocks):
            pl.when(bv_ref[blk] < row_block)(zero_copy(blk).start)
        for blk in range(n_blocks):
            pl.when(bv_ref[blk] < row_block)(zero_copy(blk).wait)

    for t in range(tt):
        for k in range(TOP_K):
            pltpu.make_async_copy(hn_ref.at[pl.ds(t, 1), :], xs_ref.at[pl.ds(dest_ref[TOP_K * t + k], 1), :],
                                  sem).start(priority=(TOP_K * t + k) % 2)
    for k in range(TOP_K):
        pltpu.make_async_copy(hn_ref, xs_ref.at[pl.ds(0, tt), :], sem).wait()


def _dispatch(block_valid, dest_flat, hn_packed, row_block):
    n_tok, half = hn_packed.shape
    tt = _divisor_tile(n_tok, 512)
    n_rows = block_valid.shape[0] * row_block
    grid_spec = pltpu.PrefetchScalarGridSpec(
        num_scalar_prefetch=1,
        grid=(n_tok // tt,),
        in_specs=[pl.BlockSpec((tt * TOP_K,), lambda i, bv: (i,), memory_space=pltpu.SMEM),
                  pl.BlockSpec((tt, half), lambda i, bv: (i, 0))],
        out_specs=pl.BlockSpec(memory_space=pl.ANY),
        scratch_shapes=[pltpu.VMEM((row_block, half), U32), pltpu.SemaphoreType.DMA(())],
    )
    return pl.pallas_call(
        _dispatch_kernel,
        grid_spec=grid_spec,
        out_shape=jax.ShapeDtypeStruct((n_rows, half), U32),
        compiler_params=_params(("arbitrary",)),
        name="dispatch",
    )(block_valid, dest_flat, hn_packed)


GROUP = 2 * LANES


def _regroup_kernel(w_ref, p_ref, o_ref):
    p = p_ref[...]
    for g in range(w_ref.shape[1] // GROUP):
        cols = slice(GROUP * g, GROUP * (g + 1))
        o_ref[:, cols] = jnp.dot(w_ref[:, cols].astype(BF16), p, preferred_element_type=F32).astype(BF16)


FFN_FF_TILE = 512


def _regroup_slice_rows(w_shape, n_items):
    n_exp, d, _ = w_shape
    rows = n_exp * d // n_items
    ok = n_items >= 2 and rows * n_items == n_exp * d and rows % (2 * SUBLANES) == 0 and d % rows == 0
    return rows if ok else None


def _regroup_permutation():
    src = lax.broadcasted_iota(I32, (GROUP, GROUP), 0)
    dst = lax.broadcasted_iota(I32, (GROUP, GROUP), 1)
    return (dst == (src // 2) + LANES * (src % 2)).astype(BF16)


def _regroup_gate_up(w_gate_up):
    n_exp, d, two_f = w_gate_up.shape
    tr = _divisor_tile(d, 2048)
    tc = 2 * _divisor_tile(two_f // 2, FFN_FF_TILE)
    perm = _regroup_permutation()
    return pl.pallas_call(
        _regroup_kernel,
        grid=(n_exp, d // tr, two_f // tc),
        in_specs=[pl.BlockSpec((None, tr, tc), lambda e, i, j: (e, i, j)),
                  pl.BlockSpec((GROUP, GROUP), lambda e, i, j: (0, 0))],
        out_specs=pl.BlockSpec((None, None, tr, tc), lambda e, i, j: (e, j, i, 0)),
        out_shape=jax.ShapeDtypeStruct((n_exp, two_f // tc, d, tc), BF16),
        compiler_params=_params(("arbitrary", "arbitrary", "arbitrary")),
        name="regroup_gate_up",
    )(w_gate_up, perm)


def _regroup_bias(b_gate_up):
    n_exp, two_f = b_gate_up.shape
    b = b_gate_up.reshape(n_exp, two_f // GROUP, LANES, 2)
    return jnp.swapaxes(b, 2, 3).reshape(n_exp, 1, two_f).astype(F32)


FFN_ROW_BLOCK = 1024
FFN_SUB_ROWS = 256


def _expert_ffn_kernel(n_f, be_ref, bv_ref, xs_ref, wgu_ref, wd_ref, bgu_ref, bd_ref,
                       out_ref, x_ref, acc_ref, wdb_ref):
    del be_ref
    b = pl.program_id(0)
    f = pl.program_id(1)
    n_rows = xs_ref.shape[0]
    sub = min(n_rows, FFN_SUB_ROWS)
    valid = bv_ref[b]
    n_sub = (valid + sub - 1) // sub
    full = n_sub == n_rows // sub
    partial = (n_sub > 0) & jnp.logical_not(full)
    first = f == 0
    last = f == n_f - 1

    def unpack():
        hi, lo = _unpack_bf16_pairs(xs_ref[...])
        return jnp.concatenate([hi.astype(BF16), lo.astype(BF16)], axis=1)

    def down(x, wd):
        gu = jnp.dot(x, wgu_ref[...], preferred_element_type=F32) + bgu_ref[...]
        acts = []
        for g in range(gu.shape[1] // GROUP):
            gate = jnp.minimum(gu[:, GROUP * g:GROUP * g + LANES], SWIGLU_LIMIT)
            up = jnp.clip(gu[:, GROUP * g + LANES:GROUP * (g + 1)], -SWIGLU_LIMIT, SWIGLU_LIMIT)
            acts.append(((up + 1.0) * (gate * jax.nn.sigmoid(SWIGLU_ALPHA * gate))).astype(BF16))
        return jnp.dot(jnp.concatenate(acts, axis=1), wd, preferred_element_type=F32)

    def full_step(is_first, is_last):
        def run():
            wd = wd_ref[...].astype(BF16)
            if is_first:
                x = unpack()
                x_ref[...] = x
                y = down(x, wd)
            else:
                y = acc_ref[...] + down(x_ref[...], wd)
            if is_last:
                out_ref[...] = _pack_bf16_pairs(y + bd_ref[...])
            else:
                acc_ref[...] = y
        return run

    if n_f == 1:
        pl.when(full)(full_step(True, True))
    else:
        pl.when(full & first)(full_step(True, False))
        pl.when(full & last)(full_step(False, True))
        if n_f > 2:
            pl.when(full & jnp.logical_not(first | last))(full_step(False, False))

    @pl.when(partial & first)
    def _():
        x_ref[...] = unpack()
        acc_ref[...] = jnp.zeros_like(acc_ref)

    @pl.when(partial)
    def _():
        wdb_ref[...] = wd_ref[...].astype(BF16)

        def body(s, carry):
            rows = pl.ds(pl.multiple_of(s * sub, sub), sub)
            acc_ref[rows, :] += down(x_ref[rows, :], wdb_ref[...])
            return carry
        lax.fori_loop(0, n_sub, body, 0)

    @pl.when(partial & last)
    def _():
        out_ref[...] = _pack_bf16_pairs(acc_ref[...] + bd_ref[...])

    @pl.when((valid == 0) & last)
    def _():
        out_ref[...] = jnp.zeros_like(out_ref)


def _expert_ffn(block_e, block_valid, xs, wgu, wd, bgu, bd, row_block):
    n_rows, half = xs.shape
    d = 2 * half
    d_ff = wd.shape[1]
    tf = wgu.shape[3] // 2
    n_blocks = n_rows // row_block
    grid_spec = pltpu.PrefetchScalarGridSpec(
        num_scalar_prefetch=2,
        grid=(n_blocks, d_ff // tf),
        in_specs=[
            pl.BlockSpec((row_block, half), lambda b, f, be, bv: (b, 0)),
            pl.BlockSpec((None, None, d, 2 * tf), lambda b, f, be, bv: (be[b], f, 0, 0)),
            pl.BlockSpec((None, tf, d), lambda b, f, be, bv: (be[b], f, 0)),
            pl.BlockSpec((None, 1, 2 * tf), lambda b, f, be, bv: (be[b], 0, f)),
            pl.BlockSpec((None, 1, d), lambda b, f, be, bv: (be[b], 0, 0)),
        ],
        out_specs=pl.BlockSpec((row_block, half), lambda b, f, be, bv: (b, 0)),
        scratch_shapes=[pltpu.VMEM((row_block, d), BF16), pltpu.VMEM((row_block, d), F32),
                        pltpu.VMEM((tf, d), BF16)],
    )
    return pl.pallas_call(
        functools.partial(_expert_ffn_kernel, d_ff // tf),
        grid_spec=grid_spec,
        out_shape=jax.ShapeDtypeStruct((n_rows, half), U32),
        compiler_params=_params(("arbitrary", "arbitrary")),
        name="expert_ffn",
    )(block_e, block_valid, xs, wgu, wd, bgu, bd)


def _combine_kernel(cur_ref, nxt_ref, h_ref, w4_ref, fg_ref, rows_hbm_ref, o_ref, buf_ref, sem):
    i = pl.program_id(0)
    n = pl.num_programs(0)
    tt = h_ref.shape[0]
    half = buf_ref.shape[3]

    def issue(dest_ref, s):
        for t in range(tt):
            for k in range(TOP_K):
                pltpu.make_async_copy(rows_hbm_ref.at[pl.ds(dest_ref[TOP_K * t + k], 1), :],
                                      buf_ref.at[s, k, pl.ds(t, 1), :],
                                      sem.at[s]).start(priority=(TOP_K * t + k) % 2)

    def drain(s):
        for k in range(TOP_K):
            pltpu.make_async_copy(rows_hbm_ref.at[pl.ds(0, tt), :], buf_ref.at[s, k], sem.at[s]).wait()

    @pl.when(i == 0)
    def _():
        issue(cur_ref, 0)

    def step(slot):
        drain(slot)
        issue(nxt_ref, 1 - slot)
        w4 = w4_ref[...]
        y_hi = jnp.zeros((tt, half), F32)
        y_lo = jnp.zeros((tt, half), F32)
        for k in range(TOP_K):
            hi, lo = _unpack_bf16_pairs(buf_ref[slot, k])
            wk = w4[:, k:k + 1]
            y_hi = y_hi + wk * hi
            y_lo = y_lo + wk * lo
        h = h_ref[...] + jnp.concatenate([y_hi, y_lo], axis=1)
        o_ref[...] = _rms(h, fg_ref[...])

        @pl.when(i == n - 1)
        def _():
            drain(1 - slot)

    for slot in range(2):
        pl.when(i % 2 == slot)(functools.partial(step, slot))


def _combine(dest_flat, h, w4, final_g, rows_packed):
    n_tok, d = h.shape
    half = d // 2
    tt = _divisor_tile(n_tok, 256)
    n_steps = n_tok // tt
    return pl.pallas_call(
        _combine_kernel,
        grid=(n_steps,),
        in_specs=[pl.BlockSpec((tt * TOP_K,), lambda i: (i,), memory_space=pltpu.SMEM),
                  pl.BlockSpec((tt * TOP_K,), lambda i: (jnp.minimum(i + 1, n_steps - 1),),
                               memory_space=pltpu.SMEM),
                  pl.BlockSpec((tt, d), lambda i: (i, 0)),
                  pl.BlockSpec((tt, LANES), lambda i: (i, 0)),
                  pl.BlockSpec((1, d), lambda i: (0, 0)),
                  pl.BlockSpec(memory_space=pl.ANY)],
        out_specs=pl.BlockSpec((tt, d), lambda i: (i, 0)),
        out_shape=jax.ShapeDtypeStruct((n_tok, d), F32),
        scratch_shapes=[pltpu.VMEM((2, TOP_K, tt, half), U32), pltpu.SemaphoreType.DMA((2,))],
        compiler_params=_params(("arbitrary",)),
        name="combine",
    )(dest_flat, dest_flat, h, w4, final_g.reshape(1, d).astype(F32), rows_packed)


def _layer(h2, batch, seq, lb, norm1_g, w_in, hg_norm_g, gate_w2, gate_b, gla_norm_g, w_proj_a, w_proj_b,
           w_out, norm2_g, router_w, router_b, w_gate_up, b_gate_up, w_down, b_down, out_norm_g):
    n_tok, d = h2.shape
    hf = lb.shape[1]
    hv = w_proj_a.shape[0]
    gk = gate_w2.shape[2]
    gv = w_proj_b.shape[0]
    rank = gate_w2.shape[1]
    hg_heads, gla_heads = hf // HEAD_DK, gk // HEAD_DK
    n_exp = router_w.shape[1]

    z, gf, gb, gg = _in_proj(h2, norm1_g, w_in, lb, gate_w2, gate_b, (hf, hv, gk, gv, rank))
    a_cols = (0, hf, 2 * hf, 3 * hf, 3 * hf + hv)
    b0 = 3 * hf + 2 * hv
    b_cols = (b0, b0 + gk, b0 + gk, b0 + 2 * gk, b0 + 2 * gk + gv)
    hosted = _regroup_slice_rows(w_gate_up.shape, _recurrence_steps(batch, seq, hg_heads)[1]) is not None
    oa = _recurrence(z, gf, gb, hg_norm_g, batch, seq, hg_heads, hv // hg_heads, a_cols,
                     name="hgrn_recurrence", regroup=w_gate_up if hosted else None)
    oa, wgu = oa if hosted else (oa, _regroup_gate_up(w_gate_up))
    ob = _recurrence(z, gg, gg, gla_norm_g, batch, seq, gla_heads, gv // gla_heads, b_cols,
                     gb_off=gla_heads, name="gla_recurrence")
    h, hn_packed, w4, idx4, sel = _mix_out(oa, ob, z, b0 + 2 * gk + 2 * gv, h2, w_proj_a, w_proj_b, w_out,
                                           norm2_g, router_w, router_b)

    n_pairs = n_tok * TOP_K
    row_block = _divisor_tile(n_pairs, FFN_ROW_BLOCK)
    n_blocks = n_pairs // row_block + n_exp
    dest4, counts = _route_rank(sel, idx4, row_block)
    dest_flat = dest4[:, :TOP_K].reshape(-1)
    cnt = counts[0, :n_exp].astype(I32)
    blocks_e = (cnt + row_block - 1) // row_block
    end_blk = jnp.cumsum(blocks_e)
    start_blk = end_blk - blocks_e
    bidx = jnp.arange(n_blocks, dtype=I32)[:, None]
    block_e = jnp.minimum(jnp.sum((end_blk[None, :] <= bidx).astype(I32), axis=1), n_exp - 1)
    owned = (bidx >= start_blk[None, :]) & (bidx < end_blk[None, :])
    rows_left = jnp.clip(cnt[None, :] - (bidx - start_blk[None, :]) * row_block, 0, row_block)
    block_valid = jnp.sum(jnp.where(owned, rows_left, 0), axis=1).astype(I32)

    xs = _dispatch(block_valid, dest_flat, hn_packed, row_block)
    rows = _expert_ffn(block_e, block_valid, xs, wgu, w_down.astype(F32),
                       _regroup_bias(b_gate_up), b_down[:, None, :].astype(F32), row_block)
    return _combine(dest_flat, h, w4, out_norm_g, rows)


def kernel(x, norm1_g, w_in, hg_lb_logits, hg_norm_g, gla_gate_w2, gla_gate_b, gla_norm_g, w_proj_a, w_proj_b,
           w_out, norm2_g, router_w, router_b, w_gate_up, b_gate_up, w_down, b_down, final_norm_g):
    batch, seq, d = x.shape
    depth = w_in.shape[0]
    assert depth == 1, "the final RMSNorm is fused into the last layer's combine kernel"
    lb_all = jnp.cumsum(jax.nn.softmax(hg_lb_logits.astype(F32), axis=1), axis=1)
    h2 = x.reshape(batch * seq, d)
    out = _layer(h2, batch, seq, lb_all[:, 0], norm1_g[0], w_in[0], hg_norm_g[0], gla_gate_w2[0], gla_gate_b[0],
                 gla_norm_g[0], w_proj_a[0], w_proj_b[0], w_out[0], norm2_g[0], router_w[0], router_b[0],
                 w_gate_up[0], b_gate_up[0], w_down[0], b_down[0], final_norm_g)
    return out.reshape(batch, seq, d)
```
